```python
import math
import jax, jax.numpy as jnp
from jax import lax
import numpy as np

D_MODEL = 4096
BATCH = 1
SEQ = 8192
DEPTH = 1
DEC_BATCH = 8
DEC_SEQ = 16
PAST_LEN = 2048

CHUNK = 64
HEAD_DIM = 128
N_HEADS = D_MODEL // HEAD_DIM
N_KV_HEADS = 8
KV_GROUP = N_HEADS // N_KV_HEADS
ATT_WIDTH = N_HEADS * HEAD_DIM
IDX_HEADS = 16
IDX_DIM = 64
TOPK_MAX = 256
Q_BLOCK = 128
ROPE_THETA = 500000.0
LRU_WIDTH = D_MODEL
LRU_BLOCKS = 16
LRU_BLOCK_W = LRU_WIDTH // LRU_BLOCKS
CONV_W = 4
LRU_C = 8.0
N_MEM = 256
MEM_HEADS = 4
MEM_HEAD_DIM = 256
MEM_WIDTH = MEM_HEADS * MEM_HEAD_DIM
PEER_HEADS = 8
PEER_NKEYS = 128
PEER_N = PEER_NKEYS * PEER_NKEYS
PEER_HALF = 128
PEER_KEY_DIM = 2 * PEER_HALF
PEER_TOPK = 16
PEER_BLOCK = 128
EPS = 1e-6

IN_SPLITS = (ATT_WIDTH, N_KV_HEADS * HEAD_DIM, N_KV_HEADS * HEAD_DIM, IDX_HEADS * IDX_DIM, IDX_DIM, IDX_HEADS,
             LRU_WIDTH, LRU_WIDTH, ATT_WIDTH, LRU_WIDTH)
IN_WIDTH = sum(IN_SPLITS)

kernel_name = "chunk_causal_dsa_rglru_peer_step"


def rms_norm(x, g):
    xf = x.astype(jnp.float32)
    y = xf * lax.rsqrt(jnp.mean(xf * xf, axis=-1, keepdims=True) + EPS)
    return (y * g.astype(jnp.float32)).astype(x.dtype)


def split_cols(y, sizes):
    outs, off = [], 0
    for s in sizes:
        outs.append(y[..., off:off + s])
        off += s
    return outs


def rope_partial(x, pos):
    rot = x.shape[-1] // 4
    half = rot // 2
    inv = ROPE_THETA ** (-jnp.arange(half, dtype=jnp.float32) / half)
    ang = pos.astype(jnp.float32)[:, None] * inv[None, :]
    cos = jnp.cos(ang)[None, :, None, :]
    sin = jnp.sin(ang)[None, :, None, :]
    x1 = x[..., :half].astype(jnp.float32)
    x2 = x[..., half:rot].astype(jnp.float32)
    r = jnp.concatenate([x1 * cos - x2 * sin, x2 * cos + x1 * sin], axis=-1).astype(x.dtype)
    return jnp.concatenate([r, x[..., rot:]], axis=-1)


def sparse_attention(q, k_all, v_all, qi, ki_all, wi, q_pos, k_pos, n_sel):
    B, T = q.shape[0], q.shape[1]
    qb = min(Q_BLOCK, T)
    nb = T // qb
    k_chunk = k_pos // CHUNK
    ki_f = ki_all.astype(jnp.float32)
    scale = HEAD_DIM ** -0.5
    idx_scale = IDX_DIM ** -0.5

    def blocks(a):
        return jnp.swapaxes(a.reshape((B, nb, qb) + a.shape[2:]), 0, 1)

    def attend(args):
        q_b, qi_b, wi_b, pos_b = args
        visible = k_chunk[None, :] <= (pos_b // CHUNK)[:, None]
        dots = jnp.einsum('bthd,bsd->bths', qi_b.astype(jnp.float32), ki_f) * idx_scale
        score = jnp.einsum('bths,bth->bts', jax.nn.relu(dots), wi_b.astype(jnp.float32))
        score = jnp.where(visible[None], score, -jnp.inf)
        top_val, top_idx = lax.top_k(score, n_sel)
        ok = jnp.isfinite(top_val)
        kg = jax.vmap(lambda kb, ib: kb[ib])(k_all, top_idx)
        vg = jax.vmap(lambda vb, ib: vb[ib])(v_all, top_idx)
        qg = q_b.reshape(B, qb, N_KV_HEADS, KV_GROUP, HEAD_DIM)
        s = jnp.einsum('btkgd,btskd->btkgs', qg, kg).astype(jnp.float32) * scale
        s = jnp.where(ok[:, :, None, None, :], s, -jnp.inf)
        p = jax.nn.softmax(s, axis=-1).astype(v_all.dtype)
        o = jnp.einsum('btkgs,btskd->btkgd', p, vg)
        return o.reshape(B, qb, N_HEADS, HEAD_DIM)

    out = lax.map(attend, (blocks(q), blocks(qi), blocks(wi), q_pos.reshape(nb, qb)))
    return jnp.swapaxes(out, 0, 1).reshape(B, T, N_HEADS, HEAD_DIM)


def rg_lru_branch(xb, gate, conv_prev, h_prev, conv_w, conv_b, lru_wa, lru_ba, lru_wi, lru_bi, lru_lambda):
    B, T, C = xb.shape
    xp = jnp.concatenate([conv_prev.astype(xb.dtype), xb], axis=1)
    xc = conv_b + xp[:, 0:T] * conv_w[0]
    for j in range(1, CONV_W):
        xc = xc + xp[:, j:j + T] * conv_w[j]
    conv_new = xp[:, xp.shape[1] - (CONV_W - 1):]
    xblk = xc.reshape(B, T, LRU_BLOCKS, LRU_BLOCK_W)
    r = jax.nn.sigmoid(jnp.einsum('btnd,nde->btne', xblk, lru_wa).reshape(B, T, C) + lru_ba)
    i = jax.nn.sigmoid(jnp.einsum('btnd,nde->btne', xblk, lru_wi).reshape(B, T, C) + lru_bi)
    log_a = -LRU_C * r.astype(jnp.float32) * jax.nn.softplus(-lru_lambda.astype(jnp.float32))
    a = jnp.exp(log_a)
    b = jnp.sqrt(-jnp.expm1(2.0 * log_a)) * (i * xc).astype(jnp.float32)

    def combine(c1, c2):
        a1, b1 = c1
        a2, b2 = c2
        return a1 * a2, a2 * b1 + b2

    a_cum, b_cum = lax.associative_scan(combine, (a, b), axis=1)
    h = a_cum * h_prev.astype(jnp.float32)[:, None, :] + b_cum
    out = h.astype(xb.dtype) * jax.nn.gelu(gate, approximate=False)
    return out, conv_new, h[:, -1].astype(h_prev.dtype)


def mixer_block(xn, pos, k_past, v_past, kidx_past, conv_prev, h_prev, w_in, conv_w, conv_b,
                lru_wa, lru_ba, lru_wi, lru_bi, lru_lambda, w_out):
    B, T, _ = xn.shape
    q, k, v, qi, ki, wi, xl, gl, ga, gb = split_cols(xn @ w_in, IN_SPLITS)
    q = rope_partial(q.reshape(B, T, N_HEADS, HEAD_DIM), pos)
    k = rope_partial(k.reshape(B, T, N_KV_HEADS, HEAD_DIM), pos)
    v = v.reshape(B, T, N_KV_HEADS, HEAD_DIM)
    qi = rope_partial(qi.reshape(B, T, IDX_HEADS, IDX_DIM), pos)
    ki = rope_partial(ki[:, :, None, :], pos)[:, :, 0]
    wi = wi * (IDX_HEADS ** -0.5)
    P = k_past.shape[1]
    k_all = jnp.concatenate([k_past.astype(k.dtype), k], axis=1)
    v_all = jnp.concatenate([v_past.astype(v.dtype), v], axis=1)
    ki_all = jnp.concatenate([kidx_past.astype(ki.dtype), ki], axis=1)
    k_pos = jnp.concatenate([jnp.arange(P, dtype=jnp.int32), pos])
    n_sel = min(TOPK_MAX, (P + T) // 4)
    att = sparse_attention(q, k_all, v_all, qi, ki_all, wi, pos, k_pos, n_sel).reshape(B, T, ATT_WIDTH)
    lru, conv_new, h_new = rg_lru_branch(xl, gl, conv_prev, h_prev, conv_w, conv_b,
                                         lru_wa, lru_ba, lru_wi, lru_bi, lru_lambda)
    merged = jax.nn.sigmoid(ga) * att + jax.nn.sigmoid(gb) * lru
    return merged @ w_out, k, v, ki, conv_new, h_new


def mem_kv(mem, g, wk, wv):
    B = mem.shape[0]
    mn = rms_norm(mem, g)
    mk = (mn @ wk).reshape(B, N_MEM, MEM_HEADS, MEM_HEAD_DIM)
    mv = (mn @ wv).reshape(B, N_MEM, MEM_HEADS, MEM_HEAD_DIM)
    return mk, mv


def mem_attention(xn, mk, mv, wq, wo):
    B, T, _ = xn.shape
    q = (xn @ wq).reshape(B, T, MEM_HEADS, MEM_HEAD_DIM)
    s = jnp.einsum('bthd,bmhd->bhtm', q, mk.astype(q.dtype)).astype(jnp.float32) * (MEM_HEAD_DIM ** -0.5)
    p = jax.nn.softmax(s, axis=-1).astype(q.dtype)
    o = jnp.einsum('bhtm,bmhd->bthd', p, mv.astype(q.dtype)).reshape(B, T, MEM_WIDTH)
    return o @ wo


def peer_ffn(xn, peer_wq, peer_subkeys, peer_u, peer_v):
    B, T, D = xn.shape
    n = B * T
    nb = -(-n // PEER_BLOCK)
    x = jnp.pad(xn.reshape(n, D), ((0, nb * PEER_BLOCK - n), (0, 0))).reshape(nb, PEER_BLOCK, D)

    def block(xb):
        q = (xb @ peer_wq).reshape(PEER_BLOCK, PEER_HEADS, 2, PEER_HALF)
        s = jnp.einsum('nhcd,hckd->nhck', q, peer_subkeys).astype(jnp.float32)
        sv, si = lax.top_k(s, PEER_TOPK)
        cand = (sv[:, :, 0, :, None] + sv[:, :, 1, None, :]).reshape(PEER_BLOCK, PEER_HEADS, PEER_TOPK * PEER_TOPK)
        cidx = (si[:, :, 0, :, None] * PEER_NKEYS + si[:, :, 1, None, :]).reshape(PEER_BLOCK, PEER_HEADS, PEER_TOPK * PEER_TOPK)
        tv, tp = lax.top_k(cand, PEER_TOPK)
        eidx = jnp.take_along_axis(cidx, tp, axis=-1)
        g = jax.nn.softmax(tv, axis=-1)
        act = jax.nn.gelu(jnp.einsum('nhed,nd->nhe', peer_u[eidx], xb).astype(jnp.float32), approximate=False)
        coef = (g * act).astype(xb.dtype)
        return jnp.einsum('nhe,nhed->nd', coef, peer_v[eidx])

    y = lax.map(block, x).reshape(nb * PEER_BLOCK, D)[:n]
    return y.reshape(B, T, D)


def layer_forward(x, pos, k_past, v_past, kidx_past, conv_prev, h_prev, mk, mv,
                  norm_mix_g, w_in, conv_w, conv_b, lru_wa, lru_ba, lru_wi, lru_bi, lru_lambda, w_out,
                  norm_mem_g, mem_wq, mem_wo, norm_ffn_g, peer_wq, peer_subkeys, peer_u, peer_v):
    mix, k, v, ki, conv_new, h_new = mixer_block(rms_norm(x, norm_mix_g), pos, k_past, v_past, kidx_past,
                                                 conv_prev, h_prev, w_in, conv_w, conv_b,
                                                 lru_wa, lru_ba, lru_wi, lru_bi, lru_lambda, w_out)
    x = x + mix
    x = x + mem_attention(rms_norm(x, norm_mem_g), mk, mv, mem_wq, mem_wo)
    x = x + peer_ffn(rms_norm(x, norm_ffn_g), peer_wq, peer_subkeys, peer_u, peer_v)
    return x, k, v, ki, conv_new, h_new


def setup_inputs(seed: int = 0) -> dict:
    key = jax.random.key(seed)
    ks = iter(jax.random.split(key, 40))

    def nrm(shape, s=1.0):
        return jax.random.normal(next(ks), shape, jnp.float32) * s

    def gain():
        return 1.0 + nrm((DEPTH, D_MODEL), 0.01)

    a8 = jax.random.uniform(next(ks), (DEPTH, LRU_WIDTH), jnp.float32, minval=0.9, maxval=0.999)
    sig = a8 ** (1.0 / LRU_C)
    lam = jnp.log(sig) - jnp.log1p(-sig)
    return {
        "x_prompt": nrm((BATCH, SEQ, D_MODEL)),
        "x_sample": nrm((DEC_BATCH, DEC_SEQ, D_MODEL)),
        "mem_prompt": nrm((BATCH, N_MEM, D_MODEL)),
        "cache_k": nrm((DEPTH, DEC_BATCH, PAST_LEN, N_KV_HEADS, HEAD_DIM)),
        "cache_v": nrm((DEPTH, DEC_BATCH, PAST_LEN, N_KV_HEADS, HEAD_DIM)),
        "cache_kidx": nrm((DEPTH, DEC_BATCH, PAST_LEN, IDX_DIM)),
        "state_conv": nrm((DEPTH, DEC_BATCH, CONV_W - 1, LRU_WIDTH)),
        "state_lru": nrm((DEPTH, DEC_BATCH, LRU_WIDTH), 0.5),
        "cache_mem_k": nrm((DEPTH, DEC_BATCH, N_MEM, MEM_HEADS, MEM_HEAD_DIM)),
        "cache_mem_v": nrm((DEPTH, DEC_BATCH, N_MEM, MEM_HEADS, MEM_HEAD_DIM)),
        "norm_mix_g": gain(),
        "w_in": nrm((DEPTH, D_MODEL, IN_WIDTH), D_MODEL ** -0.5),
        "conv_w": nrm((DEPTH, CONV_W, LRU_WIDTH), CONV_W ** -0.5),
        "conv_b": nrm((DEPTH, LRU_WIDTH), 0.01),
        "lru_wa": nrm((DEPTH, LRU_BLOCKS, LRU_BLOCK_W, LRU_BLOCK_W), LRU_BLOCK_W ** -0.5),
        "lru_ba": nrm((DEPTH, LRU_WIDTH), 0.01),
        "lru_wi": nrm((DEPTH, LRU_BLOCKS, LRU_BLOCK_W, LRU_BLOCK_W), LRU_BLOCK_W ** -0.5),
        "lru_bi": nrm((DEPTH, LRU_WIDTH), 0.01),
        "lru_lambda": lam,
        "w_out": nrm((DEPTH, D_MODEL, D_MODEL), D_MODEL ** -0.5),
        "norm_mem_g": gain(),
        "norm_memkv_g": gain(),
        "mem_wq": nrm((DEPTH, D_MODEL, MEM_WIDTH), D_MODEL ** -0.5),
        "mem_wk": nrm((DEPTH, D_MODEL, MEM_WIDTH), D_MODEL ** -0.5),
        "mem_wv": nrm((DEPTH, D_MODEL, MEM_WIDTH), D_MODEL ** -0.5),
        "mem_wo": nrm((DEPTH, MEM_WIDTH, D_MODEL), MEM_WIDTH ** -0.5),
        "norm_ffn_g": gain(),
        "peer_wq": nrm((DEPTH, D_MODEL, PEER_HEADS * PEER_KEY_DIM), D_MODEL ** -0.5),
        "peer_subkeys": nrm((DEPTH, PEER_HEADS, 2, PEER_NKEYS, PEER_HALF), PEER_HALF ** -0.5),
        "peer_u": nrm((DEPTH, PEER_N, D_MODEL), D_MODEL ** -0.5),
        "peer_v": nrm((DEPTH, PEER_N, D_MODEL), 0.5),
        "norm_final_g": 1.0 + nrm((D_MODEL,), 0.01),
    }


def reference(x_prompt, x_sample, mem_prompt, cache_k, cache_v, cache_kidx, state_conv, state_lru,
              cache_mem_k, cache_mem_v, norm_mix_g, w_in, conv_w, conv_b, lru_wa, lru_ba, lru_wi, lru_bi,
              lru_lambda, w_out, norm_mem_g, norm_memkv_g, mem_wq, mem_wk, mem_wv, mem_wo, norm_ffn_g,
              peer_wq, peer_subkeys, peer_u, peer_v, norm_final_g):
    Bp, Tp = x_prompt.shape[0], x_prompt.shape[1]
    Ts = x_sample.shape[1]
    P = cache_k.shape[2]
    dt = x_prompt.dtype
    pos_p = jnp.arange(Tp, dtype=jnp.int32)
    pos_s = P + jnp.arange(Ts, dtype=jnp.int32)
    empty_kv = jnp.zeros((Bp, 0, N_KV_HEADS, HEAD_DIM), dt)
    empty_ki = jnp.zeros((Bp, 0, IDX_DIM), dt)
    conv0 = jnp.zeros((Bp, CONV_W - 1, LRU_WIDTH), dt)
    h0 = jnp.zeros((Bp, LRU_WIDTH), state_lru.dtype)

    hp, hs = x_prompt, x_sample
    kp_l, vp_l, kip_l, cp_l, lp_l, mkp_l, mvp_l = [], [], [], [], [], [], []
    ks_l, vs_l, kis_l, cs_l, ls_l = [], [], [], [], []
    for l in range(DEPTH):
        lw = (norm_mix_g[l], w_in[l], conv_w[l], conv_b[l], lru_wa[l], lru_ba[l], lru_wi[l], lru_bi[l],
              lru_lambda[l], w_out[l], norm_mem_g[l], mem_wq[l], mem_wo[l], norm_ffn_g[l],
              peer_wq[l], peer_subkeys[l], peer_u[l], peer_v[l])
        mk_p, mv_p = mem_kv(mem_prompt, norm_memkv_g[l], mem_wk[l], mem_wv[l])
        hp, k_p, v_p, ki_p, c_p, h_p = layer_forward(hp, pos_p, empty_kv, empty_kv, empty_ki, conv0, h0,
                                                     mk_p, mv_p, *lw)
        hs, k_s, v_s, ki_s, c_s, h_s = layer_forward(hs, pos_s, cache_k[l], cache_v[l], cache_kidx[l],
                                                     state_conv[l], state_lru[l],
                                                     cache_mem_k[l], cache_mem_v[l], *lw)
        kp_l.append(k_p); vp_l.append(v_p); kip_l.append(ki_p); cp_l.append(c_p); lp_l.append(h_p)
        mkp_l.append(mk_p); mvp_l.append(mv_p)
        ks_l.append(k_s); vs_l.append(v_s); kis_l.append(ki_s); cs_l.append(c_s); ls_l.append(h_s)

    y_prompt = rms_norm(hp, norm_final_g)
    y_sample = rms_norm(hs, norm_final_g)
    k_prompt = jnp.stack(kp_l)
    v_prompt = jnp.stack(vp_l)
    kidx_prompt = jnp.stack(kip_l)
    conv_prompt = jnp.stack(cp_l)
    lru_prompt = jnp.stack(lp_l)
    memk_prompt = jnp.stack(mkp_l)
    memv_prompt = jnp.stack(mvp_l)
    k_sample = jnp.stack(ks_l)
    v_sample = jnp.stack(vs_l)
    kidx_sample = jnp.stack(kis_l)
    conv_sample = jnp.stack(cs_l)
    lru_sample = jnp.stack(ls_l)
    return (y_prompt, y_sample, k_prompt, v_prompt, kidx_prompt, conv_prompt, lru_prompt,
            memk_prompt, memv_prompt, k_sample, v_sample, kidx_sample, conv_sample, lru_sample)
```

```python
import functools

import jax
import jax.numpy as jnp
from jax import lax
from jax.experimental import pallas as pl
from jax.experimental.pallas import tpu as pltpu

F32 = jnp.float32
BF16 = jnp.bfloat16

CHUNK = 64
HEAD_DIM = 128
N_KV_HEADS = 8
IDX_HEADS = 16
IDX_DIM = 64
TOPK_MAX = 256
ROPE_THETA = 500000.0
LRU_BLOCKS = 16
CONV_W = 4
LRU_C = 8.0
MEM_HEADS = 4
PEER_HEADS = 8
PEER_NKEYS = 128
PEER_HALF = 128
PEER_TOPK = 16
EPS = 1e-6

LANES = 128
SUBLANES = 8
TOKEN_TILE = 512
VMEM_LIMIT = 56 * 1024 * 1024
NEG = -1e30
INT_MIN = -(2 ** 31)


def _cparams(sem, vmem=VMEM_LIMIT):
    return pltpu.CompilerParams(dimension_semantics=sem, vmem_limit_bytes=vmem)


def _tile(n, pref, mult):
    if n <= pref:
        return n
    t = (pref // mult) * mult
    while t > mult and n % t:
        t -= mult
    assert n % t == 0, (n, pref, mult)
    return t


def _rmsnorm_kernel(x_ref, g_ref, o_ref):
    x = x_ref[...]
    r = lax.rsqrt(jnp.mean(x * x, axis=-1, keepdims=True) + EPS)
    o_ref[...] = ((x * r) * g_ref[...]).astype(o_ref.dtype)


def _add_rmsnorm_kernel(x_ref, y_ref, g_ref, o_ref):
    x = x_ref[...] + y_ref[...]
    r = lax.rsqrt(jnp.mean(x * x, axis=-1, keepdims=True) + EPS)
    o_ref[...] = ((x * r) * g_ref[...]).astype(o_ref.dtype)


def _rmsnorm(x, g, out_dtype, add=None):
    m, d = x.shape
    tm = _tile(m, 256, SUBLANES)
    row = pl.BlockSpec((tm, d), lambda i: (i, 0))
    gsp = pl.BlockSpec((1, d), lambda i: (0, 0))
    if add is None:
        kern, specs, args = _rmsnorm_kernel, [row, gsp], (x, g.reshape(1, d))
    else:
        kern, specs, args = _add_rmsnorm_kernel, [row, row, gsp], (x, add, g.reshape(1, d))
    return pl.pallas_call(
        kern, grid=(m // tm,), in_specs=specs, out_specs=row,
        out_shape=jax.ShapeDtypeStruct((m, d), out_dtype),
        compiler_params=_cparams(("parallel",)), name="rmsnorm")(*args)


def _mm_kernel(a_ref, b_ref, o_ref):
    o_ref[...] = jnp.dot(a_ref[...], b_ref[...], preferred_element_type=F32).astype(o_ref.dtype)


def _mm_res_kernel(a_ref, b_ref, r_ref, o_ref):
    o_ref[...] = (r_ref[...] + jnp.dot(a_ref[...], b_ref[...], preferred_element_type=F32)).astype(o_ref.dtype)


def _matmul(a, b, res=None, tm_pref=512, tn_pref=640, out_dtype=F32, name="matmul"):
    m, k = a.shape
    n = b.shape[1]
    tm = _tile(m, tm_pref, 16)
    tn = _tile(n, tn_pref, LANES)
    a_spec = pl.BlockSpec((tm, k), lambda j, i: (i, 0))
    b_spec = pl.BlockSpec((k, tn), lambda j, i: (0, j))
    o_spec = pl.BlockSpec((tm, tn), lambda j, i: (i, j))
    if res is None:
        kern, specs, args = _mm_kernel, [a_spec, b_spec], (a, b)
    else:
        kern, specs, args = _mm_res_kernel, [a_spec, b_spec, o_spec], (a, b, res)
    return pl.pallas_call(
        kern, grid=(n // tn, m // tm), in_specs=specs, out_specs=o_spec,
        out_shape=jax.ShapeDtypeStruct((m, n), out_dtype),
        compiler_params=_cparams(("parallel", "parallel")), name=name)(*args)


def _rope_tables(pos, head_dim):
    rot = head_dim // 4
    half = rot // 2
    inv = ROPE_THETA ** (-jnp.arange(half, dtype=F32) / half)
    ang = pos.astype(F32)[:, None] * inv[None, :]
    cos, sin = jnp.cos(ang), jnp.sin(ang)
    n = pos.shape[0]
    ones = jnp.ones((n, head_dim - rot), F32)
    zr = jnp.zeros((n, head_dim - rot), F32)
    zh = jnp.zeros((n, half), F32)
    c = jnp.concatenate([cos, cos, ones], axis=1)
    sa = jnp.concatenate([-sin, zh, zr], axis=1)
    sb = jnp.concatenate([zh, sin, zr], axis=1)
    reps = LANES // head_dim
    return tuple(jnp.tile(t, (1, reps)) for t in (c, sa, sb))


def _rope(x, c, sa, sb, half):
    return x * c + pltpu.roll(x, LANES - half, 1) * sa + pltpu.roll(x, half, 1) * sb


def _prep_kernel(q_ref, k_ref, v_ref, qi_ref, kw_ref, c1_ref, sa1_ref, sb1_ref, c2_ref, sa2_ref, sb2_ref,
                 qo_ref, kf_ref, kb_ref, vb_ref, qio_ref, kif_ref, wi_ref):
    c1, sa1, sb1 = c1_ref[...], sa1_ref[...], sb1_ref[...]
    c2, sa2, sb2 = c2_ref[...], sa2_ref[...], sb2_ref[...]
    h1 = HEAD_DIM // 8
    h2 = IDX_DIM // 8
    for h in range(q_ref.shape[1] // LANES):
        sl = slice(h * LANES, (h + 1) * LANES)
        qo_ref[:, sl] = _rope(q_ref[:, sl], c1, sa1, sb1, h1).astype(BF16)
    for h in range(k_ref.shape[1] // LANES):
        sl = slice(h * LANES, (h + 1) * LANES)
        kr = _rope(k_ref[:, sl], c1, sa1, sb1, h1)
        kf_ref[:, sl] = kr
        kb_ref[:, sl] = kr.astype(BF16)
    vb_ref[...] = v_ref[...].astype(BF16)
    for h in range(qi_ref.shape[1] // LANES):
        sl = slice(h * LANES, (h + 1) * LANES)
        qio_ref[:, sl] = _rope(qi_ref[:, sl], c2, sa2, sb2, h2).astype(BF16)
    t = kw_ref[...]
    kif_ref[...] = _rope(t, c2, sa2, sb2, h2)[:, :IDX_DIM]
    wi_ref[...] = t[:, IDX_DIM:IDX_DIM + IDX_HEADS] * (IDX_HEADS ** -0.5)


def _prep(y, tabs1, tabs2, d, kvw, iqw, off_k, off_v, off_qi, off_kw):
    n = y.shape[0]
    tm = _tile(n, 256, 16)
    tab = pl.BlockSpec((tm, LANES), lambda i: (i, 0))
    in_specs = [
        pl.BlockSpec((tm, d), lambda i: (i, 0)),
        pl.BlockSpec((tm, kvw), lambda i: (i, off_k // kvw)),
        pl.BlockSpec((tm, kvw), lambda i: (i, off_v // kvw)),
        pl.BlockSpec((tm, iqw), lambda i: (i, off_qi // iqw)),
        pl.BlockSpec((tm, LANES), lambda i: (i, off_kw // LANES)),
        tab, tab, tab, tab, tab, tab,
    ]
    outs = [
        (d, BF16), (kvw, F32), (kvw, BF16), (kvw, BF16), (iqw, BF16), (IDX_DIM, F32), (IDX_HEADS, F32),
    ]
    return pl.pallas_call(
        _prep_kernel, grid=(n // tm,), in_specs=in_specs,
        out_specs=[pl.BlockSpec((tm, w), lambda i: (i, 0)) for w, _ in outs],
        out_shape=[jax.ShapeDtypeStruct((n, w), dt) for w, dt in outs],
        compiler_params=_cparams(("parallel",)), name="prep")(y, y, y, y, y, *tabs1, *tabs2)


def _num_k_tiles(i, tq, tk, q_off, s_valid):
    last_chunk_end = ((q_off + i * tq + tq - 1) // CHUNK + 1) * CHUNK
    return (jnp.minimum(last_chunk_end, s_valid) + tk - 1) // tk


def _idx_kernel(qi_ref, wi_ref, klo_ref, khi_ref, o_ref, key_ref, *, tq, tk, nk, q_off, s_valid, n_sel):
    i = pl.program_id(1)
    q0 = q_off + i * tq
    nkt = _num_k_tiles(i, tq, tk, q_off, s_valid)
    wi = wi_ref[0] * (IDX_DIM ** -0.5)
    wcols = [wi[:, h:h + 1] for h in range(IDX_HEADS)]
    q_chunk = (q0 + lax.broadcasted_iota(jnp.int32, (tq, tk), 0)) // CHUNK
    nt = (((1,), (1,)), ((), ()))

    def score_tile(j, carry):
        k0 = pl.multiple_of(j * tk, tk)
        klo = klo_ref[0, pl.ds(k0, tk), :]
        khi = khi_ref[0, pl.ds(k0, tk), :]
        acc = jnp.zeros((tq, tk), F32)
        for p in range(IDX_HEADS // 2):
            qp = qi_ref[0, :, p * LANES:(p + 1) * LANES]
            d0 = lax.dot_general(qp, klo, nt, preferred_element_type=F32)
            d1 = lax.dot_general(qp, khi, nt, preferred_element_type=F32)
            acc = acc + jnp.maximum(d0, 0.0) * wcols[2 * p]
            acc = acc + jnp.maximum(d1, 0.0) * wcols[2 * p + 1]
        k_pos = k0 + lax.broadcasted_iota(jnp.int32, (tq, tk), 1)
        vis = jnp.logical_and(k_pos // CHUNK <= q_chunk, k_pos < s_valid)
        bits = pltpu.bitcast(acc, jnp.int32)
        key = bits ^ ((bits >> 31) & 0x7FFFFFFF)
        key_ref[j] = jnp.where(vis, key, INT_MIN)
        return carry

    lax.fori_loop(0, nkt, score_tile, 0)

    def count(pred):
        def count_tile(j, acc):
            m = jnp.where(pred(j, key_ref[j]), 1.0, 0.0)
            part = m[:, 0:LANES]
            for c in range(1, tk // LANES):
                part = part + m[:, c * LANES:(c + 1) * LANES]
            return acc + part

        acc = lax.fori_loop(0, nkt, count_tile, jnp.zeros((tq, LANES), F32))
        return jnp.sum(acc, axis=1, keepdims=True)

    def bit_step(it, carry):
        r, cnt_r = carry
        cand = r | jnp.left_shift(jnp.int32(1), 31 - it)
        cs = cand ^ INT_MIN
        cnt = count(lambda j, key: key >= cs)
        take = cnt >= n_sel
        return jnp.where(take, cand, r), jnp.where(take, cnt, cnt_r)

    r, cnt_ge = lax.fori_loop(0, 32, bit_step,
                              (jnp.zeros((tq, 1), jnp.int32), jnp.full((tq, 1), float(n_sel), F32)))
    thr = jnp.maximum(r ^ INT_MIN, INT_MIN + 1)
    has_ties = jnp.max(cnt_ge) > float(n_sel)

    @pl.when(jnp.logical_not(has_ties))
    def _write_plain():
        def write_tile(j, carry):
            o_ref[0, j] = jnp.where(key_ref[j] >= thr, 0.0, NEG).astype(BF16)
            return carry

        lax.fori_loop(0, nkt, write_tile, 0)

    @pl.when(has_ties)
    def _write_tie_broken():
        need = float(n_sel) - count(lambda j, key: key > thr)
        lane = lax.broadcasted_iota(jnp.int32, (tq, tk), 1)

        def pos_step(it, p):
            cand = p | jnp.left_shift(jnp.int32(1), pos_bits - 1 - it)
            cnt = count(lambda j, key: jnp.logical_and(key == thr, lane + j * tk < cand))
            return jnp.where(cnt < need, cand, p)

        pos_bits = max(1, (nk * tk - 1).bit_length())
        p = lax.fori_loop(0, pos_bits, pos_step, jnp.zeros((tq, 1), jnp.int32))

        def write_tile(j, carry):
            key = key_ref[j]
            sel = jnp.logical_or(key > thr, jnp.logical_and(key == thr, lane + j * tk <= p))
            o_ref[0, j] = jnp.where(sel, 0.0, NEG).astype(BF16)
            return carry

        lax.fori_loop(0, nkt, write_tile, 0)

    def fill_tile(j, carry):
        o_ref[0, j] = jnp.full((tq, tk), NEG, BF16)
        return carry

    lax.fori_loop(nkt, nk, fill_tile, 0)


def _idx_mask(qi, wi, klo, khi, *, tq, tk, q_off, s_valid, n_sel):
    b, t, iqw = qi.shape
    s_pad = klo.shape[1]
    nk = s_pad // tk
    kern = functools.partial(_idx_kernel, tq=tq, tk=tk, nk=nk, q_off=q_off, s_valid=s_valid, n_sel=n_sel)
    return pl.pallas_call(
        kern, grid=(b, t // tq),
        in_specs=[
            pl.BlockSpec((1, tq, iqw), lambda bb, i: (bb, i, 0)),
            pl.BlockSpec((1, tq, IDX_HEADS), lambda bb, i: (bb, i, 0)),
            pl.BlockSpec((1, s_pad, LANES), lambda bb, i: (bb, 0, 0)),
            pl.BlockSpec((1, s_pad, LANES), lambda bb, i: (bb, 0, 0)),
        ],
        out_specs=pl.BlockSpec((1, nk, tq, tk), lambda bb, i: (bb, 0, i, 0)),
        out_shape=jax.ShapeDtypeStruct((b, nk, t, tk), BF16),
        scratch_shapes=[pltpu.VMEM((nk, tq, tk), jnp.int32)],
        compiler_params=_cparams(("parallel", "parallel")), name="idx_mask")(qi, wi, klo, khi)


def _attn_kernel(q_ref, k_ref, v_ref, b_ref, o_ref, qs_ref, m_ref, l_ref, acc_ref,
                 *, tq, tk, nk, q_off, s_valid, nkv, grp):
    i = pl.program_id(1)
    j = pl.program_id(2)
    nkt = _num_k_tiles(i, tq, tk, q_off, s_valid)
    scale = HEAD_DIM ** -0.5
    nt = (((1,), (1,)), ((), ()))

    @pl.when(j == 0)
    def _init():
        for g in range(nkv):
            for hh in range(grp):
                h = g * grp + hh
                qs_ref[g, hh * tq:(hh + 1) * tq, :] = q_ref[0, :, h * HEAD_DIM:(h + 1) * HEAD_DIM]
        m_ref[...] = jnp.full(m_ref.shape, NEG, F32)
        l_ref[...] = jnp.zeros(l_ref.shape, F32)
        acc_ref[...] = jnp.zeros(acc_ref.shape, F32)

    @pl.when(j < nkt)
    def _compute():
        bias = b_ref[0, 0].astype(F32)
        bias = jnp.concatenate([bias] * grp, axis=0)
        for g in range(nkv):
            kg = k_ref[0, :, g * HEAD_DIM:(g + 1) * HEAD_DIM]
            vg = v_ref[0, :, g * HEAD_DIM:(g + 1) * HEAD_DIM]
            s = lax.dot_general(qs_ref[g], kg, nt, preferred_element_type=F32) * scale + bias
            m_prev = m_ref[g]
            m_new = jnp.maximum(m_prev, jnp.max(s, axis=1, keepdims=True))
            p = jnp.exp(s - m_new)
            alpha = jnp.exp(m_prev - m_new)
            l_ref[g] = alpha * l_ref[g] + jnp.sum(p, axis=1, keepdims=True)
            acc_ref[g] = alpha * acc_ref[g] + jnp.dot(p.astype(BF16), vg, preferred_element_type=F32)
            m_ref[g] = m_new

    @pl.when(j == nk - 1)
    def _finish():
        for g in range(nkv):
            for hh in range(grp):
                h = g * grp + hh
                rows = slice(hh * tq, (hh + 1) * tq)
                o_ref[0, :, h * HEAD_DIM:(h + 1) * HEAD_DIM] = acc_ref[g, rows, :] / l_ref[g, rows, :]


def _attention(q, k, v, bias, *, tq, tk, q_off, s_valid):
    b, t, aw = q.shape
    s_pad, kvw = k.shape[1], k.shape[2]
    nk = s_pad // tk
    nkv = kvw // HEAD_DIM
    grp = aw // kvw
    kern = functools.partial(_attn_kernel, tq=tq, tk=tk, nk=nk, q_off=q_off, s_valid=s_valid, nkv=nkv, grp=grp)

    def kj(i, j):
        return jnp.minimum(j, _num_k_tiles(i, tq, tk, q_off, s_valid) - 1)

    return pl.pallas_call(
        kern, grid=(b, t // tq, nk),
        in_specs=[
            pl.BlockSpec((1, tq, aw), lambda bb, i, j: (bb, i, 0)),
            pl.BlockSpec((1, tk, kvw), lambda bb, i, j: (bb, kj(i, j), 0)),
            pl.BlockSpec((1, tk, kvw), lambda bb, i, j: (bb, kj(i, j), 0)),
            pl.BlockSpec((1, 1, tq, tk), lambda bb, i, j: (bb, kj(i, j), i, 0)),
        ],
        out_specs=pl.BlockSpec((1, tq, aw), lambda bb, i, j: (bb, i, 0)),
        out_shape=jax.ShapeDtypeStruct((b, t, aw), F32),
        scratch_shapes=[
            pltpu.VMEM((nkv, grp * tq, HEAD_DIM), BF16),
            pltpu.VMEM((nkv, grp * tq, 1), F32),
            pltpu.VMEM((nkv, grp * tq, 1), F32),
            pltpu.VMEM((nkv, grp * tq, HEAD_DIM), F32),
        ],
        compiler_params=_cparams(("parallel", "parallel", "arbitrary")), name="attention")(q, k, v, bias)


def _gelu(x):
    return 0.5 * x * (1.0 + lax.erf(x * (0.5 ** 0.5)))


def _lru_kernel(x_ref, gl_ref, gb_ref, c0_ref, h0_ref, cw_ref, cb_ref, wa_ref, ba_ref, wi_ref, bi_ref, lam_ref,
                o_ref, cn_ref, hn_ref, xp_ref, a_ref, b_ref, hs_ref, h_ref, *, tt, nblk, bw):
    t = pl.program_id(1)
    pad = SUBLANES

    @pl.when(t == 0)
    def _init():
        xp_ref[0:pad, :] = c0_ref[0]
        h_ref[...] = h0_ref[0]

    xp_ref[pad:pad + tt, :] = x_ref[...]
    cw = cw_ref[...]
    base = pad - (CONV_W - 1)
    xc = cb_ref[...] + xp_ref[base:base + tt, :] * cw[0:1]
    for jj in range(1, CONV_W):
        xc = xc + xp_ref[base + jj:base + jj + tt, :] * cw[jj:jj + 1]
    tail = xp_ref[tt:tt + pad, :]
    cn_ref[0] = tail
    xp_ref[0:pad, :] = tail

    xcb = xc.astype(BF16)
    ra, ri = [], []
    for n in range(nblk):
        xs = xcb[:, n * bw:(n + 1) * bw]
        ra.append(jnp.dot(xs, wa_ref[n], preferred_element_type=F32))
        ri.append(jnp.dot(xs, wi_ref[n], preferred_element_type=F32))
    r = jax.nn.sigmoid(jnp.concatenate(ra, axis=1) + ba_ref[...])
    ig = jax.nn.sigmoid(jnp.concatenate(ri, axis=1) + bi_ref[...])
    z = -lam_ref[...]
    softplus = jnp.maximum(z, 0.0) + jnp.log1p(jnp.exp(-jnp.abs(z)))
    log_a = (-LRU_C * r) * softplus
    x2 = 2.0 * log_a
    neg_expm1 = -jnp.tanh(0.5 * x2) * (jnp.exp(x2) + 1.0)
    a_ref[...] = jnp.exp(log_a)
    b_ref[...] = jnp.sqrt(neg_expm1) * (ig * xc)

    def step(s, h):
        h = a_ref[pl.ds(s, 1), :] * h + b_ref[pl.ds(s, 1), :]
        hs_ref[pl.ds(s, 1), :] = h
        return h

    h = lax.fori_loop(0, tt, step, h_ref[...], unroll=8)
    h_ref[...] = h
    hn_ref[0] = h
    o_ref[0] = jax.nn.sigmoid(gb_ref[...]) * (hs_ref[...] * _gelu(gl_ref[...]))


def _lru(y, row_off, b, t, off_x, off_gl, off_gb, conv0, h0, conv_w, conv_b, wa, ba, wi, bi, lam, *, tt):
    c = conv_w.shape[1]
    nblk, bw = wa.shape[0], wa.shape[1]
    assert row_off % tt == 0 and t % tt == 0 and off_x % c == 0 and off_gl % c == 0 and off_gb % c == 0
    kern = functools.partial(_lru_kernel, tt=tt, nblk=nblk, bw=bw)
    col = lambda off: pl.BlockSpec((tt, c), lambda bb, i: (row_off // tt + bb * (t // tt) + i, off // c))
    vec = pl.BlockSpec((1, c), lambda bb, i: (0, 0))
    wsp = pl.BlockSpec((nblk, bw, bw), lambda bb, i: (0, 0, 0))
    return pl.pallas_call(
        kern, grid=(b, t // tt),
        in_specs=[
            col(off_x), col(off_gl), col(off_gb),
            pl.BlockSpec((1, SUBLANES, c), lambda bb, i: (bb, 0, 0)),
            pl.BlockSpec((1, 1, c), lambda bb, i: (bb, 0, 0)),
            pl.BlockSpec((CONV_W, c), lambda bb, i: (0, 0)),
            vec, wsp, vec, wsp, vec, vec,
        ],
        out_specs=[
            pl.BlockSpec((1, tt, c), lambda bb, i: (bb, i, 0)),
            pl.BlockSpec((1, SUBLANES, c), lambda bb, i: (bb, 0, 0)),
            pl.BlockSpec((1, 1, c), lambda bb, i: (bb, 0, 0)),
        ],
        out_shape=[
            jax.ShapeDtypeStruct((b, t, c), F32),
            jax.ShapeDtypeStruct((b, SUBLANES, c), F32),
            jax.ShapeDtypeStruct((b, 1, c), F32),
        ],
        scratch_shapes=[
            pltpu.VMEM((tt + SUBLANES, c), F32),
            pltpu.VMEM((tt, c), F32),
            pltpu.VMEM((tt, c), F32),
            pltpu.VMEM((tt, c), F32),
            pltpu.VMEM((1, c), F32),
        ],
        compiler_params=_cparams(("arbitrary", "arbitrary")), name="rg_lru")(
            y, y, y, conv0, h0, conv_w, conv_b.reshape(1, c), wa, ba.reshape(1, c), wi, bi.reshape(1, c),
            lam.reshape(1, c))


def _merge_kernel(ga_ref, att_ref, lru_ref, o_ref):
    o_ref[...] = (jax.nn.sigmoid(ga_ref[...]) * att_ref[...] + lru_ref[...]).astype(o_ref.dtype)


def _merge(y, off_ga, att, lru):
    n, d = att.shape
    tm = _tile(n, 256, 16)
    row = pl.BlockSpec((tm, d), lambda i: (i, 0))
    return pl.pallas_call(
        _merge_kernel, grid=(n // tm,),
        in_specs=[pl.BlockSpec((tm, d), lambda i: (i, off_ga // d)), row, row],
        out_specs=row, out_shape=jax.ShapeDtypeStruct((n, d), BF16),
        compiler_params=_cparams(("parallel",)), name="merge")(y, att, lru)


def _mem_attn_kernel(q_ref, mk_ref, mv_ref, o_ref, *, heads, hd):
    scale = hd ** -0.5
    nt = (((1,), (1,)), ((), ()))
    for h in range(heads):
        sl = slice(h * hd, (h + 1) * hd)
        s = lax.dot_general(q_ref[0, :, sl].astype(BF16), mk_ref[0, :, sl], nt, preferred_element_type=F32) * scale
        s = s - jnp.max(s, axis=1, keepdims=True)
        e = jnp.exp(s)
        p = e / jnp.sum(e, axis=1, keepdims=True)
        o_ref[0, :, sl] = jnp.dot(p.astype(BF16), mv_ref[0, :, sl], preferred_element_type=F32).astype(o_ref.dtype)


def _mem_attention(q, mk, mv, *, tm):
    b, t, w = q.shape
    nm = mk.shape[1]
    kern = functools.partial(_mem_attn_kernel, heads=MEM_HEADS, hd=w // MEM_HEADS)
    return pl.pallas_call(
        kern, grid=(b, t // tm),
        in_specs=[
            pl.BlockSpec((1, tm, w), lambda bb, i: (bb, i, 0)),
            pl.BlockSpec((1, nm, w), lambda bb, i: (bb, 0, 0)),
            pl.BlockSpec((1, nm, w), lambda bb, i: (bb, 0, 0)),
        ],
        out_specs=pl.BlockSpec((1, tm, w), lambda bb, i: (bb, i, 0)),
        out_shape=jax.ShapeDtypeStruct((b, t, w), BF16),
        compiler_params=_cparams(("parallel", "parallel")), name="mem_attention")(q, mk, mv)


def _top_values(x, k):
    vals = []
    for _ in range(k):
        m = jnp.max(x, axis=0, keepdims=True)
        vals.append(m)
        x = jnp.where(x == m, -jnp.inf, x)
    return vals


def _peer_route_kernel(q_ref, sub_ref, sa_ref, sb_ref, ca_ref, eb_ref, thr_ref, *, heads):
    nt = (((1,), (1,)), ((), ()))
    for h in range(heads):
        st = []
        sv = []
        for c in range(2):
            col = (h * 2 + c) * PEER_HALF
            qh = q_ref[:, col:col + PEER_HALF].astype(BF16)
            s = lax.dot_general(sub_ref[h * 2 + c], qh, nt, preferred_element_type=F32)
            st.append(s)
            sv.append(_top_values(s, PEER_TOPK))
        cand = jnp.concatenate([sv[0][a] + jnp.concatenate(sv[1], axis=0) for a in range(PEER_TOPK)], axis=0)
        tv = _top_values(cand, PEER_TOPK)
        z = jnp.zeros_like(tv[0])
        for v in tv:
            z = z + jnp.exp(v - tv[0])
        sa_ref[h] = st[0]
        sb_ref[h] = st[1]
        ca_ref[h] = jnp.exp(st[0] - sv[0][0]) / z
        eb_ref[h] = jnp.exp(st[1] - sv[1][0])
        thr_ref[h] = tv[-1]


def _peer_route(qp, sub, *, tn):
    n = qp.shape[0]
    heads = PEER_HEADS
    kern = functools.partial(_peer_route_kernel, heads=heads)
    big = pl.BlockSpec((heads, PEER_NKEYS, tn), lambda i: (0, 0, i))
    big_shape = jax.ShapeDtypeStruct((heads, PEER_NKEYS, n), F32)
    return pl.pallas_call(
        kern, grid=(n // tn,),
        in_specs=[
            pl.BlockSpec((tn, qp.shape[1]), lambda i: (i, 0)),
            pl.BlockSpec(sub.shape, lambda i: (0, 0, 0)),
        ],
        out_specs=[big, big, big, big, pl.BlockSpec((heads, 1, tn), lambda i: (0, 0, i))],
        out_shape=[big_shape, big_shape, big_shape, big_shape, jax.ShapeDtypeStruct((heads, 1, n), F32)],
        compiler_params=_cparams(("parallel",)), name="peer_route")(qp, sub)


def _peer_dense_kernel(x_ref, u_ref, v_ref, sa_ref, sb_ref, ca_ref, eb_ref, thr_ref, o_ref, *, heads, rows):
    j = pl.program_id(1)

    @pl.when(j == 0)
    def _init():
        o_ref[...] = jnp.zeros(o_ref.shape, F32)

    nt = (((1,), (1,)), ((), ()))
    act = _gelu(lax.dot_general(u_ref[...], x_ref[...], nt, preferred_element_type=F32))
    tiles = []
    for r in range(rows):
        i1 = j * rows + r
        w = None
        for h in range(heads):
            s1 = sa_ref[h, pl.ds(i1, 1), :]
            c1 = ca_ref[h, pl.ds(i1, 1), :]
            sel = (s1 + sb_ref[h]) >= thr_ref[h]
            term = jnp.where(sel, eb_ref[h], 0.0) * c1
            w = term if w is None else w + term
        tiles.append(w)
    wt = jnp.concatenate(tiles, axis=0) if rows > 1 else tiles[0]
    coef = (wt * act).astype(BF16)
    tn_dims = (((0,), (0,)), ((), ()))
    o_ref[...] += lax.dot_general(coef, v_ref[...], tn_dims, preferred_element_type=F32)


def _peer_dense(xn, u, v, sa, sb, ca, eb, thr, *, tn, te):
    n, d = xn.shape
    e = u.shape[0]
    heads = sa.shape[0]
    rows = te // PEER_NKEYS
    kern = functools.partial(_peer_dense_kernel, heads=heads, rows=rows)
    big = pl.BlockSpec((heads, PEER_NKEYS, tn), lambda i, j: (0, 0, i))
    return pl.pallas_call(
        kern, grid=(n // tn, e // te),
        in_specs=[
            pl.BlockSpec((tn, d), lambda i, j: (i, 0)),
            pl.BlockSpec((te, d), lambda i, j: (j, 0)),
            pl.BlockSpec((te, d), lambda i, j: (j, 0)),
            big, big, big, big,
            pl.BlockSpec((heads, 1, tn), lambda i, j: (0, 0, i)),
        ],
        out_specs=pl.BlockSpec((tn, d), lambda i, j: (i, 0)),
        out_shape=jax.ShapeDtypeStruct((n, d), F32),
        compiler_params=_cparams(("parallel", "arbitrary")), name="peer_dense")(xn, u, v, sa, sb, ca, eb, thr)


def _pad_rows(x, n):
    return jnp.pad(x, ((0, n - x.shape[0]),) + ((0, 0),) * (x.ndim - 1))


def _index_keys(ki, s_pad):
    kb = ki.astype(BF16)
    extra = s_pad - ki.shape[1]
    lo = jnp.pad(kb, ((0, 0), (0, extra), (0, LANES - IDX_DIM)))
    hi = jnp.pad(kb, ((0, 0), (0, extra), (LANES - IDX_DIM, 0)))
    return lo, hi


def kernel(x_prompt, x_sample, mem_prompt, cache_k, cache_v, cache_kidx, state_conv, state_lru, cache_mem_k, cache_mem_v, norm_mix_g, w_in, conv_w, conv_b, lru_wa, lru_ba, lru_wi, lru_bi, lru_lambda, w_out, norm_mem_g, norm_memkv_g, mem_wq, mem_wk, mem_wv, mem_wo, norm_ffn_g, peer_wq, peer_subkeys, peer_u, peer_v, norm_final_g):
    assert w_in.shape[0] == 1, "single-layer step"
    assert IDX_DIM * 2 == LANES and PEER_HALF == LANES and PEER_NKEYS == LANES
    bp, tp, d = x_prompt.shape
    bs, ts, _ = x_sample.shape
    past = cache_k.shape[2]
    n_p, n_s = bp * tp, bs * ts
    n_tok = n_p + n_s
    n_pad = -(-n_tok // TOKEN_TILE) * TOKEN_TILE
    kvw = N_KV_HEADS * HEAD_DIM
    iqw = IDX_HEADS * IDX_DIM
    n_mem = mem_prompt.shape[1]
    mem_w = mem_wq.shape[2]

    x0 = _pad_rows(jnp.concatenate([x_prompt.reshape(n_p, d), x_sample.reshape(n_s, d)], axis=0), n_pad)
    pos = _pad_rows(jnp.concatenate([jnp.tile(jnp.arange(tp, dtype=jnp.int32), bp),
                                     jnp.tile(past + jnp.arange(ts, dtype=jnp.int32), bs)]), n_pad)

    w = w_in[0]
    o_k, o_v, o_qi, o_ki = d, d + kvw, d + 2 * kvw, d + 2 * kvw + iqw
    o_wi = o_ki + IDX_DIM
    o_xl = o_wi + IDX_HEADS
    o_gl, o_ga, o_gb = o_xl + d, o_xl + 2 * d, o_xl + 3 * d
    assert o_gb + d == w.shape[1]
    w_all = jnp.concatenate([
        w[:, :d], w[:, o_xl:o_xl + 4 * d], w[:, o_k:o_ki],
        jnp.pad(w[:, o_ki:o_xl], ((0, 0), (0, LANES - IDX_DIM - IDX_HEADS)))], axis=1).astype(BF16)
    y_xl, y_gl, y_ga, y_gb = d, 2 * d, 3 * d, 4 * d
    y_k, y_v, y_qi = 5 * d, 5 * d + kvw, 5 * d + 2 * kvw
    y_kw = y_qi + iqw

    xn = _rmsnorm(x0, norm_mix_g[0], BF16)
    y = _matmul(xn, w_all, name="in_proj")

    tabs1 = _rope_tables(pos, HEAD_DIM)
    tabs2 = _rope_tables(pos, IDX_DIM)
    q_bf, k_f, k_bf, v_bf, qi_bf, ki_f, wi = _prep(y, tabs1, tabs2, d, kvw, iqw, y_k, y_v, y_qi, y_kw)
    v_f = y[:n_tok, y_v:y_v + kvw]

    n_sel_p = min(TOPK_MAX, tp // 4)
    tq_p = _tile(tp, 128, 16)
    tk_p = _tile(tp, 512, LANES)
    klo, khi = _index_keys(ki_f[:n_p].reshape(bp, tp, IDX_DIM), tp)
    bias_p = _idx_mask(qi_bf[:n_p].reshape(bp, tp, iqw), wi[:n_p].reshape(bp, tp, IDX_HEADS), klo, khi,
                       tq=tq_p, tk=tk_p, q_off=0, s_valid=tp, n_sel=n_sel_p)
    att_p = _attention(q_bf[:n_p].reshape(bp, tp, d), k_bf[:n_p].reshape(bp, tp, kvw),
                       v_bf[:n_p].reshape(bp, tp, kvw), bias_p,
                       tq=_tile(tp, 256, 16), tk=tk_p, q_off=0, s_valid=tp)

    s_s = past + ts
    s_pad = -(-s_s // LANES) * LANES
    n_sel_s = min(TOPK_MAX, s_s // 4)
    ki_s = jnp.concatenate([cache_kidx[0], ki_f[n_p:n_tok].reshape(bs, ts, IDX_DIM)], axis=1)
    klo_s, khi_s = _index_keys(ki_s, s_pad)
    bias_s = _idx_mask(qi_bf[n_p:n_tok].reshape(bs, ts, iqw), wi[n_p:n_tok].reshape(bs, ts, IDX_HEADS), klo_s, khi_s,
                       tq=ts, tk=s_pad, q_off=past, s_valid=s_s, n_sel=n_sel_s)

    def with_cache(cache, new):
        full = jnp.concatenate([cache.reshape(bs, past, kvw).astype(BF16), new.reshape(bs, ts, kvw)], axis=1)
        return jnp.pad(full, ((0, 0), (0, s_pad - s_s), (0, 0)))

    att_s = _attention(q_bf[n_p:n_tok].reshape(bs, ts, d), with_cache(cache_k[0], k_bf[n_p:n_tok]),
                       with_cache(cache_v[0], v_bf[n_p:n_tok]), bias_s,
                       tq=ts, tk=s_pad, q_off=past, s_valid=s_s)
    att = _pad_rows(jnp.concatenate([att_p.reshape(n_p, d), att_s.reshape(n_s, d)], axis=0), n_pad)

    wa_bf, wi_bf = lru_wa[0].astype(BF16), lru_wi[0].astype(BF16)
    lru_args = (conv_w[0], conv_b[0], wa_bf, lru_ba[0], wi_bf, lru_bi[0], lru_lambda[0])
    state_pad = SUBLANES - (CONV_W - 1)
    conv0_p = jnp.zeros((bp, SUBLANES, d), F32)
    h0_p = jnp.zeros((bp, 1, d), F32)
    lru_p, cn_p, hn_p = _lru(y, 0, bp, tp, y_xl, y_gl, y_gb, conv0_p, h0_p, *lru_args,
                             tt=_tile(tp, 128, SUBLANES))
    conv0_s = jnp.pad(state_conv[0], ((0, 0), (state_pad, 0), (0, 0)))
    lru_s, cn_s, hn_s = _lru(y, n_p, bs, ts, y_xl, y_gl, y_gb, conv0_s,
                             state_lru[0].reshape(bs, 1, d), *lru_args, tt=ts)
    lru = _pad_rows(jnp.concatenate([lru_p.reshape(n_p, d), lru_s.reshape(n_s, d)], axis=0), n_pad)

    merged = _merge(y, y_ga, att, lru)
    x1 = _matmul(merged, w_out[0].astype(BF16), res=x0, tn_pref=1024, name="out_proj")

    mem_n = _rmsnorm(mem_prompt.reshape(bp * n_mem, d), norm_memkv_g[0], BF16)
    w_kv = jnp.concatenate([mem_wk[0], mem_wv[0]], axis=1).astype(BF16)
    mkv = _matmul(mem_n, w_kv, tn_pref=1024, name="mem_kv")
    mk_p = mkv[:, :mem_w].reshape(bp, n_mem, mem_w)
    mv_p = mkv[:, mem_w:].reshape(bp, n_mem, mem_w)
    xn2 = _rmsnorm(x1, norm_mem_g[0], BF16)
    qm = _matmul(xn2, mem_wq[0].astype(BF16), tn_pref=1024, name="mem_q")
    om_p = _mem_attention(qm[:n_p].reshape(bp, tp, mem_w), mk_p.astype(BF16), mv_p.astype(BF16),
                          tm=_tile(tp, 512, 16))
    om_s = _mem_attention(qm[n_p:n_tok].reshape(bs, ts, mem_w),
                          cache_mem_k[0].reshape(bs, n_mem, mem_w).astype(BF16),
                          cache_mem_v[0].reshape(bs, n_mem, mem_w).astype(BF16), tm=ts)
    om = _pad_rows(jnp.concatenate([om_p.reshape(n_p, mem_w), om_s.reshape(n_s, mem_w)], axis=0), n_pad)
    x2 = _matmul(om, mem_wo[0].astype(BF16), res=x1, tn_pref=1024, name="mem_out")

    xn3 = _rmsnorm(x2, norm_ffn_g[0], BF16)
    qp = _matmul(xn3, peer_wq[0].astype(BF16), tn_pref=1024, name="peer_q")
    sub = peer_subkeys[0].reshape(PEER_HEADS * 2, PEER_NKEYS, PEER_HALF).astype(BF16)
    sa, sb, ca, eb, thr = _peer_route(qp, sub, tn=_tile(n_pad, 256, LANES))
    peer = _peer_dense(xn3, peer_u[0].astype(BF16), peer_v[0].astype(BF16), sa, sb, ca, eb, thr,
                       tn=_tile(n_pad, 256, LANES), te=256)

    y_all = _rmsnorm(x2, norm_final_g, F32, add=peer)

    y_prompt = y_all[:n_p].reshape(bp, tp, d)
    y_sample = y_all[n_p:n_tok].reshape(bs, ts, d)
    kv_shape_p = (1, bp, tp, N_KV_HEADS, HEAD_DIM)
    kv_shape_s = (1, bs, ts, N_KV_HEADS, HEAD_DIM)
    tail = slice(SUBLANES - (CONV_W - 1), SUBLANES)
    return (
        y_prompt, y_sample,
        k_f[:n_p].reshape(kv_shape_p), v_f[:n_p].reshape(kv_shape_p), ki_f[:n_p].reshape(1, bp, tp, IDX_DIM),
        cn_p[:, tail][None], hn_p.reshape(1, bp, d),
        mk_p.reshape(1, bp, n_mem, MEM_HEADS, mem_w // MEM_HEADS),
        mv_p.reshape(1, bp, n_mem, MEM_HEADS, mem_w // MEM_HEADS),
        k_f[n_p:n_tok].reshape(kv_shape_s), v_f[n_p:].reshape(kv_shape_s),
        ki_f[n_p:n_tok].reshape(1, bs, ts, IDX_DIM),
        cn_s[:, tail][None], hn_s.reshape(1, bs, d),
    )
```

```python
import functools

import jax
import jax.numpy as jnp
from jax import lax
from jax.experimental import pallas as pl
from jax.experimental.pallas import tpu as pltpu

F32 = jnp.float32
BF16 = jnp.bfloat16

CHUNK = 64
HEAD_DIM = 128
N_KV_HEADS = 8
IDX_HEADS = 16
IDX_DIM = 64
TOPK_MAX = 256
ROPE_THETA = 500000.0
LRU_BLOCKS = 16
CONV_W = 4
LRU_C = 8.0
MEM_HEADS = 4
PEER_HEADS = 8
PEER_NKEYS = 128
PEER_HALF = 128
PEER_TOPK = 16
EPS = 1e-6

LANES = 128
SUBLANES = 8
TOKEN_TILE = 512
VMEM_LIMIT = 56 * 1024 * 1024
NEG = -1e30
INT_MIN = -(2 ** 31)


def _cparams(sem, vmem=VMEM_LIMIT):
    return pltpu.CompilerParams(dimension_semantics=sem, vmem_limit_bytes=vmem)


def _tile(n, pref, mult):
    if n <= pref:
        return n
    t = (pref // mult) * mult
    while t > mult and n % t:
        t -= mult
    assert n % t == 0, (n, pref, mult)
    return t


def _rmsnorm_kernel(x_ref, g_ref, o_ref):
    x = x_ref[...]
    r = lax.rsqrt(jnp.mean(x * x, axis=-1, keepdims=True) + EPS)
    o_ref[...] = ((x * r) * g_ref[...]).astype(o_ref.dtype)


def _add_rmsnorm_kernel(x_ref, y_ref, g_ref, o_ref):
    x = x_ref[...] + y_ref[...]
    r = lax.rsqrt(jnp.mean(x * x, axis=-1, keepdims=True) + EPS)
    o_ref[...] = ((x * r) * g_ref[...]).astype(o_ref.dtype)


def _rmsnorm(x, g, out_dtype, add=None):
    m, d = x.shape
    tm = _tile(m, 256, SUBLANES)
    row = pl.BlockSpec((tm, d), lambda i: (i, 0))
    gsp = pl.BlockSpec((1, d), lambda i: (0, 0))
    if add is None:
        kern, specs, args = _rmsnorm_kernel, [row, gsp], (x, g.reshape(1, d))
    else:
        kern, specs, args = _add_rmsnorm_kernel, [row, row, gsp], (x, add, g.reshape(1, d))
    return pl.pallas_call(
        kern, grid=(m // tm,), in_specs=specs, out_specs=row,
        out_shape=jax.ShapeDtypeStruct((m, d), out_dtype),
        compiler_params=_cparams(("parallel",)), name="rmsnorm")(*args)


def _mm_kernel(a_ref, b_ref, o_ref):
    o_ref[...] = jnp.dot(a_ref[...], b_ref[...], preferred_element_type=F32).astype(o_ref.dtype)


def _mm_res_kernel(a_ref, b_ref, r_ref, o_ref):
    o_ref[...] = (r_ref[...] + jnp.dot(a_ref[...], b_ref[...], preferred_element_type=F32)).astype(o_ref.dtype)


def _matmul(a, b, res=None, tm_pref=512, tn_pref=640, out_dtype=F32, name="matmul"):
    m, k = a.shape
    n = b.shape[1]
    tm = _tile(m, tm_pref, 16)
    tn = _tile(n, tn_pref, LANES)
    a_spec = pl.BlockSpec((tm, k), lambda j, i: (i, 0))
    b_spec = pl.BlockSpec((k, tn), lambda j, i: (0, j))
    o_spec = pl.BlockSpec((tm, tn), lambda j, i: (i, j))
    if res is None:
        kern, specs, args = _mm_kernel, [a_spec, b_spec], (a, b)
    else:
        kern, specs, args = _mm_res_kernel, [a_spec, b_spec, o_spec], (a, b, res)
    return pl.pallas_call(
        kern, grid=(n // tn, m // tm), in_specs=specs, out_specs=o_spec,
        out_shape=jax.ShapeDtypeStruct((m, n), out_dtype),
        compiler_params=_cparams(("parallel", "parallel")), name=name)(*args)


def _rope_tables(pos, head_dim):
    rot = head_dim // 4
    half = rot // 2
    inv = ROPE_THETA ** (-jnp.arange(half, dtype=F32) / half)
    ang = pos.astype(F32)[:, None] * inv[None, :]
    cos, sin = jnp.cos(ang), jnp.sin(ang)
    n = pos.shape[0]
    ones = jnp.ones((n, head_dim - rot), F32)
    zr = jnp.zeros((n, head_dim - rot), F32)
    zh = jnp.zeros((n, half), F32)
    c = jnp.concatenate([cos, cos, ones], axis=1)
    sa = jnp.concatenate([-sin, zh, zr], axis=1)
    sb = jnp.concatenate([zh, sin, zr], axis=1)
    reps = LANES // head_dim
    return tuple(jnp.tile(t, (1, reps)) for t in (c, sa, sb))


def _rope(x, c, sa, sb, half):
    return x * c + pltpu.roll(x, LANES - half, 1) * sa + pltpu.roll(x, half, 1) * sb


def _prep_kernel(q_ref, k_ref, v_ref, qi_ref, kw_ref, c1_ref, sa1_ref, sb1_ref, c2_ref, sa2_ref, sb2_ref,
                 qo_ref, kf_ref, kb_ref, vb_ref, qio_ref, kif_ref, wi_ref):
    c1, sa1, sb1 = c1_ref[...], sa1_ref[...], sb1_ref[...]
    c2, sa2, sb2 = c2_ref[...], sa2_ref[...], sb2_ref[...]
    h1 = HEAD_DIM // 8
    h2 = IDX_DIM // 8
    for h in range(q_ref.shape[1] // LANES):
        sl = slice(h * LANES, (h + 1) * LANES)
        qo_ref[:, sl] = _rope(q_ref[:, sl], c1, sa1, sb1, h1).astype(BF16)
    for h in range(k_ref.shape[1] // LANES):
        sl = slice(h * LANES, (h + 1) * LANES)
        kr = _rope(k_ref[:, sl], c1, sa1, sb1, h1)
        kf_ref[:, sl] = kr
        kb_ref[:, sl] = kr.astype(BF16)
    vb_ref[...] = v_ref[...].astype(BF16)
    for h in range(qi_ref.shape[1] // LANES):
        sl = slice(h * LANES, (h + 1) * LANES)
        qio_ref[:, sl] = _rope(qi_ref[:, sl], c2, sa2, sb2, h2).astype(BF16)
    t = kw_ref[...]
    kif_ref[...] = _rope(t, c2, sa2, sb2, h2)[:, :IDX_DIM]
    wi_ref[...] = t[:, IDX_DIM:IDX_DIM + IDX_HEADS] * (IDX_HEADS ** -0.5)


def _prep(y, tabs1, tabs2, d, kvw, iqw, off_k, off_v, off_qi, off_kw):
    n = y.shape[0]
    tm = _tile(n, 256, 16)
    tab = pl.BlockSpec((tm, LANES), lambda i: (i, 0))
    in_specs = [
        pl.BlockSpec((tm, d), lambda i: (i, 0)),
        pl.BlockSpec((tm, kvw), lambda i: (i, off_k // kvw)),
        pl.BlockSpec((tm, kvw), lambda i: (i, off_v // kvw)),
        pl.BlockSpec((tm, iqw), lambda i: (i, off_qi // iqw)),
        pl.BlockSpec((tm, LANES), lambda i: (i, off_kw // LANES)),
        tab, tab, tab, tab, tab, tab,
    ]
    outs = [
        (d, BF16), (kvw, F32), (kvw, BF16), (kvw, BF16), (iqw, BF16), (IDX_DIM, F32), (IDX_HEADS, F32),
    ]
    return pl.pallas_call(
        _prep_kernel, grid=(n // tm,), in_specs=in_specs,
        out_specs=[pl.BlockSpec((tm, w), lambda i: (i, 0)) for w, _ in outs],
        out_shape=[jax.ShapeDtypeStruct((n, w), dt) for w, dt in outs],
        compiler_params=_cparams(("parallel",)), name="prep")(y, y, y, y, y, *tabs1, *tabs2)


def _num_k_tiles(i, tq, tk, q_off, s_valid):
    last_chunk_end = ((q_off + i * tq + tq - 1) // CHUNK + 1) * CHUNK
    return (jnp.minimum(last_chunk_end, s_valid) + tk - 1) // tk


def _idx_kernel(qi_ref, wi_ref, klo_ref, khi_ref, o_ref, key_ref, *, tq, tk, nk, q_off, s_valid, n_sel):
    i = pl.program_id(1)
    q0 = q_off + i * tq
    nkt = _num_k_tiles(i, tq, tk, q_off, s_valid)
    wi = wi_ref[0] * (IDX_DIM ** -0.5)
    wcols = [wi[:, h:h + 1] for h in range(IDX_HEADS)]
    q_chunk = (q0 + lax.broadcasted_iota(jnp.int32, (tq, tk), 0)) // CHUNK
    nt = (((1,), (1,)), ((), ()))

    def score_tile(j, carry):
        k0 = pl.multiple_of(j * tk, tk)
        klo = klo_ref[0, pl.ds(k0, tk), :]
        khi = khi_ref[0, pl.ds(k0, tk), :]
        acc = jnp.zeros((tq, tk), F32)
        for p in range(IDX_HEADS // 2):
            qp = qi_ref[0, :, p * LANES:(p + 1) * LANES]
            d0 = lax.dot_general(qp, klo, nt, preferred_element_type=F32)
            d1 = lax.dot_general(qp, khi, nt, preferred_element_type=F32)
            acc = acc + jnp.maximum(d0, 0.0) * wcols[2 * p]
            acc = acc + jnp.maximum(d1, 0.0) * wcols[2 * p + 1]
        k_pos = k0 + lax.broadcasted_iota(jnp.int32, (tq, tk), 1)
        vis = jnp.logical_and(k_pos // CHUNK <= q_chunk, k_pos < s_valid)
        bits = pltpu.bitcast(acc, jnp.int32)
        key = bits ^ ((bits >> 31) & 0x7FFFFFFF)
        key_ref[j] = jnp.where(vis, key, INT_MIN)
        return carry

    lax.fori_loop(0, nkt, score_tile, 0)

    def count(pred):
        def count_tile(j, acc):
            m = jnp.where(pred(j, key_ref[j]), 1.0, 0.0)
            part = m[:, 0:LANES]
            for c in range(1, tk // LANES):
                part = part + m[:, c * LANES:(c + 1) * LANES]
            return acc + part

        acc = lax.fori_loop(0, nkt, count_tile, jnp.zeros((tq, LANES), F32))
        return jnp.sum(acc, axis=1, keepdims=True)

    def bit_step(it, carry):
        r, cnt_r = carry
        cand = r | jnp.left_shift(jnp.int32(1), 31 - it)
        cs = cand ^ INT_MIN
        cnt = count(lambda j, key: key >= cs)
        take = cnt >= n_sel
        return jnp.where(take, cand, r), jnp.where(take, cnt, cnt_r)

    r, cnt_ge = lax.fori_loop(0, 32, bit_step,
                              (jnp.zeros((tq, 1), jnp.int32), jnp.full((tq, 1), float(n_sel), F32)))
    thr = jnp.maximum(r ^ INT_MIN, INT_MIN + 1)
    has_ties = jnp.max(cnt_ge) > float(n_sel)

    @pl.when(jnp.logical_not(has_ties))
    def _write_plain():
        def write_tile(j, carry):
            o_ref[0, j] = jnp.where(key_ref[j] >= thr, 0.0, NEG).astype(BF16)
            return carry

        lax.fori_loop(0, nkt, write_tile, 0)

    @pl.when(has_ties)
    def _write_tie_broken():
        need = float(n_sel) - count(lambda j, key: key > thr)
        lane = lax.broadcasted_iota(jnp.int32, (tq, tk), 1)

        def pos_step(it, p):
            cand = p | jnp.left_shift(jnp.int32(1), pos_bits - 1 - it)
            cnt = count(lambda j, key: jnp.logical_and(key == thr, lane + j * tk < cand))
            return jnp.where(cnt < need, cand, p)

        pos_bits = max(1, (nk * tk - 1).bit_length())
        p = lax.fori_loop(0, pos_bits, pos_step, jnp.zeros((tq, 1), jnp.int32))

        def write_tile(j, carry):
            key = key_ref[j]
            sel = jnp.logical_or(key > thr, jnp.logical_and(key == thr, lane + j * tk <= p))
            o_ref[0, j] = jnp.where(sel, 0.0, NEG).astype(BF16)
            return carry

        lax.fori_loop(0, nkt, write_tile, 0)

    def fill_tile(j, carry):
        o_ref[0, j] = jnp.full((tq, tk), NEG, BF16)
        return carry

    lax.fori_loop(nkt, nk, fill_tile, 0)


def _idx_mask(qi, wi, klo, khi, *, tq, tk, q_off, s_valid, n_sel):
    b, t, iqw = qi.shape
    s_pad = klo.shape[1]
    nk = s_pad // tk
    kern = functools.partial(_idx_kernel, tq=tq, tk=tk, nk=nk, q_off=q_off, s_valid=s_valid, n_sel=n_sel)
    return pl.pallas_call(
        kern, grid=(b, t // tq),
        in_specs=[
            pl.BlockSpec((1, tq, iqw), lambda bb, i: (bb, i, 0)),
            pl.BlockSpec((1, tq, IDX_HEADS), lambda bb, i: (bb, i, 0)),
            pl.BlockSpec((1, s_pad, LANES), lambda bb, i: (bb, 0, 0)),
            pl.BlockSpec((1, s_pad, LANES), lambda bb, i: (bb, 0, 0)),
        ],
        out_specs=pl.BlockSpec((1, nk, tq, tk), lambda bb, i: (bb, 0, i, 0)),
        out_shape=jax.ShapeDtypeStruct((b, nk, t, tk), BF16),
        scratch_shapes=[pltpu.VMEM((nk, tq, tk), jnp.int32)],
        compiler_params=_cparams(("parallel", "parallel")), name="idx_mask")(qi, wi, klo, khi)


def _attn_kernel(q_ref, k_ref, v_ref, b_ref, o_ref, qs_ref, m_ref, acc_ref,
                 *, tq, tk, nk, q_off, s_valid, nkv, grp):
    i = pl.program_id(1)
    j = pl.program_id(2)
    nkt = _num_k_tiles(i, tq, tk, q_off, s_valid)
    scale_log2e = (HEAD_DIM ** -0.5) * 1.4426950408889634
    nt = (((1,), (1,)), ((), ()))

    @pl.when(j == 0)
    def _init():
        for g in range(nkv):
            for hh in range(grp):
                h = g * grp + hh
                qs_ref[g, hh * tq:(hh + 1) * tq, :] = q_ref[0, :, h * HEAD_DIM:(h + 1) * HEAD_DIM]
        m_ref[...] = jnp.full(m_ref.shape, NEG, F32)
        acc_ref[...] = jnp.zeros(acc_ref.shape, F32)

    @pl.when(j < nkt)
    def _compute():
        bias = b_ref[0, 0].astype(F32)
        bias = jnp.concatenate([bias] * grp, axis=0)
        nc = tk // LANES
        ones = jnp.ones((tk, LANES), BF16)
        chunks = lambda a: [a[:, c * LANES:(c + 1) * LANES] for c in range(nc)]
        for g in range(nkv):
            kg = k_ref[0, :, g * HEAD_DIM:(g + 1) * HEAD_DIM]
            vg = jnp.concatenate([v_ref[0, :, g * HEAD_DIM:(g + 1) * HEAD_DIM], ones], axis=1)
            s = lax.dot_general(qs_ref[g], kg, nt, preferred_element_type=F32) * scale_log2e + bias
            m_prev = m_ref[g]
            m_new = jnp.maximum(m_prev, jnp.max(functools.reduce(jnp.maximum, chunks(s)), axis=1, keepdims=True))
            p = jnp.exp2(s - jnp.tile(m_new, (1, nc)))
            alpha = jnp.exp2(m_prev - m_new)
            acc_ref[g] = jnp.tile(alpha, (1, 2)) * acc_ref[g] + jnp.dot(p.astype(BF16), vg, preferred_element_type=F32)
            m_ref[g] = m_new

    @pl.when(j == nk - 1)
    def _finish():
        for g in range(nkv):
            for hh in range(grp):
                h = g * grp + hh
                rows = slice(hh * tq, (hh + 1) * tq)
                o_ref[0, :, h * HEAD_DIM:(h + 1) * HEAD_DIM] = acc_ref[g, rows, :HEAD_DIM] / acc_ref[g, rows, HEAD_DIM:]


def _attention(q, k, v, bias, *, tq, tk, q_off, s_valid):
    b, t, aw = q.shape
    s_pad, kvw = k.shape[1], k.shape[2]
    nk = s_pad // tk
    nkv = kvw // HEAD_DIM
    grp = aw // kvw
    kern = functools.partial(_attn_kernel, tq=tq, tk=tk, nk=nk, q_off=q_off, s_valid=s_valid, nkv=nkv, grp=grp)

    def kj(i, j):
        return jnp.minimum(j, _num_k_tiles(i, tq, tk, q_off, s_valid) - 1)

    return pl.pallas_call(
        kern, grid=(b, t // tq, nk),
        in_specs=[
            pl.BlockSpec((1, tq, aw), lambda bb, i, j: (bb, i, 0)),
            pl.BlockSpec((1, tk, kvw), lambda bb, i, j: (bb, kj(i, j), 0)),
            pl.BlockSpec((1, tk, kvw), lambda bb, i, j: (bb, kj(i, j), 0)),
            pl.BlockSpec((1, 1, tq, tk), lambda bb, i, j: (bb, kj(i, j), i, 0)),
        ],
        out_specs=pl.BlockSpec((1, tq, aw), lambda bb, i, j: (bb, i, 0)),
        out_shape=jax.ShapeDtypeStruct((b, t, aw), F32),
        scratch_shapes=[
            pltpu.VMEM((nkv, grp * tq, HEAD_DIM), BF16),
            pltpu.VMEM((nkv, grp * tq, LANES), F32),
            pltpu.VMEM((nkv, grp * tq, 2 * HEAD_DIM), F32),
        ],
        compiler_params=_cparams(("parallel", "parallel", "arbitrary")), name="attention")(q, k, v, bias)


def _gelu(x):
    return 0.5 * x * (1.0 + lax.erf(x * (0.5 ** 0.5)))


def _lru_kernel(x_ref, gl_ref, gb_ref, c0_ref, h0_ref, cw_ref, cb_ref, wa_ref, ba_ref, wi_ref, bi_ref, lam_ref,
                o_ref, cn_ref, hn_ref, xp_ref, a_ref, b_ref, hs_ref, h_ref, *, tt, nblk, bw):
    t = pl.program_id(1)
    pad = SUBLANES

    @pl.when(t == 0)
    def _init():
        xp_ref[0:pad, :] = c0_ref[0]
        h_ref[...] = h0_ref[0]

    xp_ref[pad:pad + tt, :] = x_ref[...]
    cw = cw_ref[...]
    base = pad - (CONV_W - 1)
    xc = cb_ref[...] + xp_ref[base:base + tt, :] * cw[0:1]
    for jj in range(1, CONV_W):
        xc = xc + xp_ref[base + jj:base + jj + tt, :] * cw[jj:jj + 1]
    tail = xp_ref[tt:tt + pad, :]
    cn_ref[0] = tail
    xp_ref[0:pad, :] = tail

    xcb = xc.astype(BF16)
    ra, ri = [], []
    for n in range(nblk):
        xs = xcb[:, n * bw:(n + 1) * bw]
        ra.append(jnp.dot(xs, wa_ref[n], preferred_element_type=F32))
        ri.append(jnp.dot(xs, wi_ref[n], preferred_element_type=F32))
    r = jax.nn.sigmoid(jnp.concatenate(ra, axis=1) + ba_ref[...])
    ig = jax.nn.sigmoid(jnp.concatenate(ri, axis=1) + bi_ref[...])
    z = -lam_ref[...]
    softplus = jnp.maximum(z, 0.0) + jnp.log1p(jnp.exp(-jnp.abs(z)))
    log_a = (-LRU_C * r) * softplus
    x2 = 2.0 * log_a
    neg_expm1 = -jnp.tanh(0.5 * x2) * (jnp.exp(x2) + 1.0)
    a_ref[...] = jnp.exp(log_a)
    b_ref[...] = jnp.sqrt(neg_expm1) * (ig * xc)

    def step(s, h):
        h = a_ref[pl.ds(s, 1), :] * h + b_ref[pl.ds(s, 1), :]
        hs_ref[pl.ds(s, 1), :] = h
        return h

    h = lax.fori_loop(0, tt, step, h_ref[...], unroll=8)
    h_ref[...] = h
    hn_ref[0] = h
    o_ref[0] = jax.nn.sigmoid(gb_ref[...]) * (hs_ref[...] * _gelu(gl_ref[...]))


def _lru(y, row_off, b, t, off_x, off_gl, off_gb, conv0, h0, conv_w, conv_b, wa, ba, wi, bi, lam, *, tt):
    c = conv_w.shape[1]
    nblk, bw = wa.shape[0], wa.shape[1]
    assert row_off % tt == 0 and t % tt == 0 and off_x % c == 0 and off_gl % c == 0 and off_gb % c == 0
    kern = functools.partial(_lru_kernel, tt=tt, nblk=nblk, bw=bw)
    col = lambda off: pl.BlockSpec((tt, c), lambda bb, i: (row_off // tt + bb * (t // tt) + i, off // c))
    vec = pl.BlockSpec((1, c), lambda bb, i: (0, 0))
    wsp = pl.BlockSpec((nblk, bw, bw), lambda bb, i: (0, 0, 0))
    return pl.pallas_call(
        kern, grid=(b, t // tt),
        in_specs=[
            col(off_x), col(off_gl), col(off_gb),
            pl.BlockSpec((1, SUBLANES, c), lambda bb, i: (bb, 0, 0)),
            pl.BlockSpec((1, 1, c), lambda bb, i: (bb, 0, 0)),
            pl.BlockSpec((CONV_W, c), lambda bb, i: (0, 0)),
            vec, wsp, vec, wsp, vec, vec,
        ],
        out_specs=[
            pl.BlockSpec((1, tt, c), lambda bb, i: (bb, i, 0)),
            pl.BlockSpec((1, SUBLANES, c), lambda bb, i: (bb, 0, 0)),
            pl.BlockSpec((1, 1, c), lambda bb, i: (bb, 0, 0)),
        ],
        out_shape=[
            jax.ShapeDtypeStruct((b, t, c), F32),
            jax.ShapeDtypeStruct((b, SUBLANES, c), F32),
            jax.ShapeDtypeStruct((b, 1, c), F32),
        ],
        scratch_shapes=[
            pltpu.VMEM((tt + SUBLANES, c), F32),
            pltpu.VMEM((tt, c), F32),
            pltpu.VMEM((tt, c), F32),
            pltpu.VMEM((tt, c), F32),
            pltpu.VMEM((1, c), F32),
        ],
        compiler_params=_cparams(("arbitrary", "arbitrary")), name="rg_lru")(
            y, y, y, conv0, h0, conv_w, conv_b.reshape(1, c), wa, ba.reshape(1, c), wi, bi.reshape(1, c),
            lam.reshape(1, c))


def _merge_kernel(ga_ref, att_ref, lru_ref, o_ref):
    o_ref[...] = (jax.nn.sigmoid(ga_ref[...]) * att_ref[...] + lru_ref[...]).astype(o_ref.dtype)


def _merge(y, off_ga, att, lru):
    n, d = att.shape
    tm = _tile(n, 256, 16)
    row = pl.BlockSpec((tm, d), lambda i: (i, 0))
    return pl.pallas_call(
        _merge_kernel, grid=(n // tm,),
        in_specs=[pl.BlockSpec((tm, d), lambda i: (i, off_ga // d)), row, row],
        out_specs=row, out_shape=jax.ShapeDtypeStruct((n, d), BF16),
        compiler_params=_cparams(("parallel",)), name="merge")(y, att, lru)


def _mem_attn_kernel(q_ref, mk_ref, mv_ref, o_ref, *, heads, hd):
    scale = hd ** -0.5
    nt = (((1,), (1,)), ((), ()))
    for h in range(heads):
        sl = slice(h * hd, (h + 1) * hd)
        s = lax.dot_general(q_ref[0, :, sl].astype(BF16), mk_ref[0, :, sl], nt, preferred_element_type=F32) * scale
        s = s - jnp.max(s, axis=1, keepdims=True)
        e = jnp.exp(s)
        p = e / jnp.sum(e, axis=1, keepdims=True)
        o_ref[0, :, sl] = jnp.dot(p.astype(BF16), mv_ref[0, :, sl], preferred_element_type=F32).astype(o_ref.dtype)


def _mem_attention(q, mk, mv, *, tm):
    b, t, w = q.shape
    nm = mk.shape[1]
    kern = functools.partial(_mem_attn_kernel, heads=MEM_HEADS, hd=w // MEM_HEADS)
    return pl.pallas_call(
        kern, grid=(b, t // tm),
        in_specs=[
            pl.BlockSpec((1, tm, w), lambda bb, i: (bb, i, 0)),
            pl.BlockSpec((1, nm, w), lambda bb, i: (bb, 0, 0)),
            pl.BlockSpec((1, nm, w), lambda bb, i: (bb, 0, 0)),
        ],
        out_specs=pl.BlockSpec((1, tm, w), lambda bb, i: (bb, i, 0)),
        out_shape=jax.ShapeDtypeStruct((b, t, w), BF16),
        compiler_params=_cparams(("parallel", "parallel")), name="mem_attention")(q, mk, mv)


def _top_values(x, k):
    vals = []
    for _ in range(k):
        m = jnp.max(x, axis=0, keepdims=True)
        vals.append(m)
        x = jnp.where(x == m, -jnp.inf, x)
    return vals


def _peer_route_kernel(q_ref, sub_ref, sa_ref, sb_ref, ca_ref, eb_ref, thr_ref, *, heads):
    nt = (((1,), (1,)), ((), ()))
    for h in range(heads):
        st = []
        sv = []
        for c in range(2):
            col = (h * 2 + c) * PEER_HALF
            qh = q_ref[:, col:col + PEER_HALF].astype(BF16)
            s = lax.dot_general(sub_ref[h * 2 + c], qh, nt, preferred_element_type=F32)
            st.append(s)
            sv.append(_top_values(s, PEER_TOPK))
        cand = jnp.concatenate([sv[0][a] + jnp.concatenate(sv[1], axis=0) for a in range(PEER_TOPK)], axis=0)
        tv = _top_values(cand, PEER_TOPK)
        z = jnp.zeros_like(tv[0])
        for v in tv:
            z = z + jnp.exp(v - tv[0])
        sa_ref[h] = st[0]
        sb_ref[h] = st[1]
        ca_ref[h] = jnp.exp(st[0] - sv[0][0]) / z
        eb_ref[h] = jnp.exp(st[1] - sv[1][0])
        thr_ref[h] = tv[-1]


def _peer_route(qp, sub, *, tn):
    n = qp.shape[0]
    heads = PEER_HEADS
    kern = functools.partial(_peer_route_kernel, heads=heads)
    big = pl.BlockSpec((heads, PEER_NKEYS, tn), lambda i: (0, 0, i))
    big_shape = jax.ShapeDtypeStruct((heads, PEER_NKEYS, n), F32)
    return pl.pallas_call(
        kern, grid=(n // tn,),
        in_specs=[
            pl.BlockSpec((tn, qp.shape[1]), lambda i: (i, 0)),
            pl.BlockSpec(sub.shape, lambda i: (0, 0, 0)),
        ],
        out_specs=[big, big, big, big, pl.BlockSpec((heads, 1, tn), lambda i: (0, 0, i))],
        out_shape=[big_shape, big_shape, big_shape, big_shape, jax.ShapeDtypeStruct((heads, 1, n), F32)],
        compiler_params=_cparams(("parallel",)), name="peer_route")(qp, sub)


def _peer_dense_kernel(x_ref, u_ref, v_ref, sa_ref, ca_ref, sb_ref, eb_ref, thr_ref, o_ref, *, heads, rows):
    j = pl.program_id(1)

    @pl.when(j == 0)
    def _init():
        o_ref[...] = jnp.zeros(o_ref.shape, F32)

    nt = (((1,), (1,)), ((), ()))
    act = _gelu(lax.dot_general(u_ref[...], x_ref[...], nt, preferred_element_type=F32))
    tiles = []
    for r in range(rows):
        w = None
        for h in range(heads):
            s1 = sa_ref[r, h:h + 1, :]
            c1 = ca_ref[r, h:h + 1, :]
            sel = (s1 + sb_ref[h]) >= thr_ref[h:h + 1, :]
            term = jnp.where(sel, eb_ref[h], 0.0) * c1
            w = term if w is None else w + term
        tiles.append((w * act[r * PEER_NKEYS:(r + 1) * PEER_NKEYS, :]).astype(BF16))
    coef = jnp.concatenate(tiles, axis=0) if rows > 1 else tiles[0]
    tn_dims = (((0,), (0,)), ((), ()))
    o_ref[...] += lax.dot_general(coef, v_ref[...], tn_dims, preferred_element_type=F32)


def _peer_dense(xn, u, v, sa, ca, sb, eb, thr, *, tn, te):
    n, d = xn.shape
    e = u.shape[0]
    heads = sb.shape[0]
    rows = te // PEER_NKEYS
    kern = functools.partial(_peer_dense_kernel, heads=heads, rows=rows)
    once = pl.Buffered(1)
    row_blk = pl.BlockSpec((rows, heads, tn), lambda i, j: (j, 0, i))
    big = pl.BlockSpec((heads, PEER_NKEYS, tn), lambda i, j: (0, 0, i), pipeline_mode=once)
    return pl.pallas_call(
        kern, grid=(n // tn, e // te),
        in_specs=[
            pl.BlockSpec((tn, d), lambda i, j: (i, 0), pipeline_mode=once),
            pl.BlockSpec((te, d), lambda i, j: (j, 0)),
            pl.BlockSpec((te, d), lambda i, j: (j, 0)),
            row_blk, row_blk, big, big,
            pl.BlockSpec((heads, tn), lambda i, j: (0, i)),
        ],
        out_specs=pl.BlockSpec((tn, d), lambda i, j: (i, 0)),
        out_shape=jax.ShapeDtypeStruct((n, d), F32),
        compiler_params=_cparams(("parallel", "arbitrary")), name="peer_dense")(xn, u, v, sa, ca, sb, eb, thr)


def _pad_rows(x, n):
    return jnp.pad(x, ((0, n - x.shape[0]),) + ((0, 0),) * (x.ndim - 1))


def _index_keys(ki, s_pad):
    kb = ki.astype(BF16)
    extra = s_pad - ki.shape[1]
    lo = jnp.pad(kb, ((0, 0), (0, extra), (0, LANES - IDX_DIM)))
    hi = jnp.pad(kb, ((0, 0), (0, extra), (LANES - IDX_DIM, 0)))
    return lo, hi


def kernel(x_prompt, x_sample, mem_prompt, cache_k, cache_v, cache_kidx, state_conv, state_lru, cache_mem_k, cache_mem_v, norm_mix_g, w_in, conv_w, conv_b, lru_wa, lru_ba, lru_wi, lru_bi, lru_lambda, w_out, norm_mem_g, norm_memkv_g, mem_wq, mem_wk, mem_wv, mem_wo, norm_ffn_g, peer_wq, peer_subkeys, peer_u, peer_v, norm_final_g):
    assert w_in.shape[0] == 1, "single-layer step"
    assert IDX_DIM * 2 == LANES and PEER_HALF == LANES and PEER_NKEYS == LANES
    bp, tp, d = x_prompt.shape
    bs, ts, _ = x_sample.shape
    past = cache_k.shape[2]
    n_p, n_s = bp * tp, bs * ts
    n_tok = n_p + n_s
    n_pad = -(-n_tok // TOKEN_TILE) * TOKEN_TILE
    kvw = N_KV_HEADS * HEAD_DIM
    iqw = IDX_HEADS * IDX_DIM
    n_mem = mem_prompt.shape[1]
    mem_w = mem_wq.shape[2]

    x0 = _pad_rows(jnp.concatenate([x_prompt.reshape(n_p, d), x_sample.reshape(n_s, d)], axis=0), n_pad)
    pos = _pad_rows(jnp.concatenate([jnp.tile(jnp.arange(tp, dtype=jnp.int32), bp),
                                     jnp.tile(past + jnp.arange(ts, dtype=jnp.int32), bs)]), n_pad)

    w = w_in[0]
    o_k, o_v, o_qi, o_ki = d, d + kvw, d + 2 * kvw, d + 2 * kvw + iqw
    o_wi = o_ki + IDX_DIM
    o_xl = o_wi + IDX_HEADS
    o_gl, o_ga, o_gb = o_xl + d, o_xl + 2 * d, o_xl + 3 * d
    assert o_gb + d == w.shape[1]
    w_all = jnp.concatenate([
        w[:, :d], w[:, o_xl:o_xl + 4 * d], w[:, o_k:o_ki],
        jnp.pad(w[:, o_ki:o_xl], ((0, 0), (0, LANES - IDX_DIM - IDX_HEADS)))], axis=1).astype(BF16)
    y_xl, y_gl, y_ga, y_gb = d, 2 * d, 3 * d, 4 * d
    y_k, y_v, y_qi = 5 * d, 5 * d + kvw, 5 * d + 2 * kvw
    y_kw = y_qi + iqw

    xn = _rmsnorm(x0, norm_mix_g[0], BF16)
    y = _matmul(xn, w_all, name="in_proj")

    tabs1 = _rope_tables(pos, HEAD_DIM)
    tabs2 = _rope_tables(pos, IDX_DIM)
    q_bf, k_f, k_bf, v_bf, qi_bf, ki_f, wi = _prep(y, tabs1, tabs2, d, kvw, iqw, y_k, y_v, y_qi, y_kw)
    v_f = y[:n_tok, y_v:y_v + kvw]

    n_sel_p = min(TOPK_MAX, tp // 4)
    tq_p = _tile(tp, 128, 16)
    tk_p = _tile(tp, 512, LANES)
    klo, khi = _index_keys(ki_f[:n_p].reshape(bp, tp, IDX_DIM), tp)
    bias_p = _idx_mask(qi_bf[:n_p].reshape(bp, tp, iqw), wi[:n_p].reshape(bp, tp, IDX_HEADS), klo, khi,
                       tq=tq_p, tk=tk_p, q_off=0, s_valid=tp, n_sel=n_sel_p)
    att_p = _attention(q_bf[:n_p].reshape(bp, tp, d), k_bf[:n_p].reshape(bp, tp, kvw),
                       v_bf[:n_p].reshape(bp, tp, kvw), bias_p,
                       tq=_tile(tp, 256, 16), tk=tk_p, q_off=0, s_valid=tp)

    s_s = past + ts
    s_pad = -(-s_s // LANES) * LANES
    n_sel_s = min(TOPK_MAX, s_s // 4)
    ki_s = jnp.concatenate([cache_kidx[0], ki_f[n_p:n_tok].reshape(bs, ts, IDX_DIM)], axis=1)
    klo_s, khi_s = _index_keys(ki_s, s_pad)
    bias_s = _idx_mask(qi_bf[n_p:n_tok].reshape(bs, ts, iqw), wi[n_p:n_tok].reshape(bs, ts, IDX_HEADS), klo_s, khi_s,
                       tq=ts, tk=s_pad, q_off=past, s_valid=s_s, n_sel=n_sel_s)

    def with_cache(cache, new):
        full = jnp.concatenate([cache.reshape(bs, past, kvw).astype(BF16), new.reshape(bs, ts, kvw)], axis=1)
        return jnp.pad(full, ((0, 0), (0, s_pad - s_s), (0, 0)))

    att_s = _attention(q_bf[n_p:n_tok].reshape(bs, ts, d), with_cache(cache_k[0], k_bf[n_p:n_tok]),
                       with_cache(cache_v[0], v_bf[n_p:n_tok]), bias_s,
                       tq=ts, tk=s_pad, q_off=past, s_valid=s_s)
    att = _pad_rows(jnp.concatenate([att_p.reshape(n_p, d), att_s.reshape(n_s, d)], axis=0), n_pad)

    wa_bf, wi_bf = lru_wa[0].astype(BF16), lru_wi[0].astype(BF16)
    lru_args = (conv_w[0], conv_b[0], wa_bf, lru_ba[0], wi_bf, lru_bi[0], lru_lambda[0])
    state_pad = SUBLANES - (CONV_W - 1)
    conv0_p = jnp.zeros((bp, SUBLANES, d), F32)
    h0_p = jnp.zeros((bp, 1, d), F32)
    lru_p, cn_p, hn_p = _lru(y, 0, bp, tp, y_xl, y_gl, y_gb, conv0_p, h0_p, *lru_args,
                             tt=_tile(tp, 128, SUBLANES))
    conv0_s = jnp.pad(state_conv[0], ((0, 0), (state_pad, 0), (0, 0)))
    lru_s, cn_s, hn_s = _lru(y, n_p, bs, ts, y_xl, y_gl, y_gb, conv0_s,
                             state_lru[0].reshape(bs, 1, d), *lru_args, tt=ts)
    lru = _pad_rows(jnp.concatenate([lru_p.reshape(n_p, d), lru_s.reshape(n_s, d)], axis=0), n_pad)

    merged = _merge(y, y_ga, att, lru)
    x1 = _matmul(merged, w_out[0].astype(BF16), res=x0, tn_pref=1024, name="out_proj")

    mem_n = _rmsnorm(mem_prompt.reshape(bp * n_mem, d), norm_memkv_g[0], BF16)
    w_kv = jnp.concatenate([mem_wk[0], mem_wv[0]], axis=1).astype(BF16)
    mkv = _matmul(mem_n, w_kv, tn_pref=1024, name="mem_kv")
    mk_p = mkv[:, :mem_w].reshape(bp, n_mem, mem_w)
    mv_p = mkv[:, mem_w:].reshape(bp, n_mem, mem_w)
    xn2 = _rmsnorm(x1, norm_mem_g[0], BF16)
    qm = _matmul(xn2, mem_wq[0].astype(BF16), tn_pref=1024, name="mem_q")
    om_p = _mem_attention(qm[:n_p].reshape(bp, tp, mem_w), mk_p.astype(BF16), mv_p.astype(BF16),
                          tm=_tile(tp, 512, 16))
    om_s = _mem_attention(qm[n_p:n_tok].reshape(bs, ts, mem_w),
                          cache_mem_k[0].reshape(bs, n_mem, mem_w).astype(BF16),
                          cache_mem_v[0].reshape(bs, n_mem, mem_w).astype(BF16), tm=ts)
    om = _pad_rows(jnp.concatenate([om_p.reshape(n_p, mem_w), om_s.reshape(n_s, mem_w)], axis=0), n_pad)
    x2 = _matmul(om, mem_wo[0].astype(BF16), res=x1, tn_pref=1024, name="mem_out")

    xn3 = _rmsnorm(x2, norm_ffn_g[0], BF16)
    qp = _matmul(xn3, peer_wq[0].astype(BF16), tn_pref=1024, name="peer_q")
    sub = peer_subkeys[0].reshape(PEER_HEADS * 2, PEER_NKEYS, PEER_HALF).astype(BF16)
    sa, sb, ca, eb, thr = _peer_route(qp, sub, tn=_tile(n_pad, 256, LANES))
    peer = _peer_dense(xn3, peer_u[0].astype(BF16), peer_v[0].astype(BF16),
                       jnp.transpose(sa, (1, 0, 2)), jnp.transpose(ca, (1, 0, 2)), sb, eb,
                       thr.reshape(PEER_HEADS, n_pad), tn=_tile(n_pad, 512, LANES), te=512)

    y_all = _rmsnorm(x2, norm_final_g, F32, add=peer)

    y_prompt = y_all[:n_p].reshape(bp, tp, d)
    y_sample = y_all[n_p:n_tok].reshape(bs, ts, d)
    kv_shape_p = (1, bp, tp, N_KV_HEADS, HEAD_DIM)
    kv_shape_s = (1, bs, ts, N_KV_HEADS, HEAD_DIM)
    tail = slice(SUBLANES - (CONV_W - 1), SUBLANES)
    return (
        y_prompt, y_sample,
        k_f[:n_p].reshape(kv_shape_p), v_f[:n_p].reshape(kv_shape_p), ki_f[:n_p].reshape(1, bp, tp, IDX_DIM),
        cn_p[:, tail][None], hn_p.reshape(1, bp, d),
        mk_p.reshape(1, bp, n_mem, MEM_HEADS, mem_w // MEM_HEADS),
        mv_p.reshape(1, bp, n_mem, MEM_HEADS, mem_w // MEM_HEADS),
        k_f[n_p:n_tok].reshape(kv_shape_s), v_f[n_p:].reshape(kv_shape_s),
        ki_f[n_p:n_tok].reshape(1, bs, ts, IDX_DIM),
        cn_s[:, tail][None], hn_s.reshape(1, bs, d),
    )
```

```python
import functools

import jax
import jax.numpy as jnp
from jax import lax
from jax.experimental import pallas as pl
from jax.experimental.pallas import tpu as pltpu

F32 = jnp.float32
BF16 = jnp.bfloat16

CHUNK = 64
HEAD_DIM = 128
N_KV_HEADS = 8
IDX_HEADS = 16
IDX_DIM = 64
TOPK_MAX = 256
ROPE_THETA = 500000.0
LRU_BLOCKS = 16
CONV_W = 4
LRU_C = 8.0
MEM_HEADS = 4
PEER_HEADS = 8
PEER_NKEYS = 128
PEER_HALF = 128
PEER_TOPK = 16
EPS = 1e-6

LANES = 128
SUBLANES = 8
TOKEN_TILE = 128
VMEM_LIMIT = 56 * 1024 * 1024
NEG = -1e30
INT_MIN = -(2 ** 31)


def _cparams(sem, vmem=VMEM_LIMIT):
    return pltpu.CompilerParams(dimension_semantics=sem, vmem_limit_bytes=vmem)


def _tile(n, pref, mult):
    if n <= pref:
        return n
    t = (pref // mult) * mult
    while t > mult and n % t:
        t -= mult
    assert n % t == 0, (n, pref, mult)
    return t


def _rmsnorm_kernel(x_ref, g_ref, o_ref):
    x = x_ref[...]
    r = lax.rsqrt(jnp.mean(x * x, axis=-1, keepdims=True) + EPS)
    o_ref[...] = ((x * r) * g_ref[...]).astype(o_ref.dtype)


def _final_norm_kernel(x_ref, y_ref, g_ref, op_ref, os_ref, *, first_blocks):
    x = x_ref[...] + y_ref[...]
    r = lax.rsqrt(jnp.mean(x * x, axis=-1, keepdims=True) + EPS)
    val = (x * r) * g_ref[...]

    @pl.when(pl.program_id(0) < first_blocks)
    def _():
        op_ref[...] = val

    @pl.when(pl.program_id(0) >= first_blocks)
    def _():
        os_ref[...] = val


def _final_norm(x, y, g, n_first, n_second):
    d = x.shape[1]
    tm = _tile(n_second, 128, SUBLANES)
    assert n_first % tm == 0
    fb = n_first // tm
    row = pl.BlockSpec((tm, d), lambda i: (i, 0))
    return pl.pallas_call(
        functools.partial(_final_norm_kernel, first_blocks=fb), grid=(fb + n_second // tm,),
        in_specs=[row, row, pl.BlockSpec((1, d), lambda i: (0, 0))],
        out_specs=[pl.BlockSpec((tm, d), lambda i: (jnp.minimum(i, fb - 1), 0)),
                   pl.BlockSpec((tm, d), lambda i: (jnp.maximum(i - fb, 0), 0))],
        out_shape=[jax.ShapeDtypeStruct((n_first, d), F32), jax.ShapeDtypeStruct((n_second, d), F32)],
        compiler_params=_cparams(("arbitrary",)), name="final_norm")(x, y, g.reshape(1, d))


def _rmsnorm(x, g, out_dtype):
    m, d = x.shape
    tm = _tile(m, 640, SUBLANES)
    row = pl.BlockSpec((tm, d), lambda i: (i, 0))
    return pl.pallas_call(
        _rmsnorm_kernel, grid=(m // tm,), in_specs=[row, pl.BlockSpec((1, d), lambda i: (0, 0))], out_specs=row,
        out_shape=jax.ShapeDtypeStruct((m, d), out_dtype),
        compiler_params=_cparams(("parallel",)), name="rmsnorm")(x, g.reshape(1, d))


def _mm_kernel(a_ref, b_ref, o_ref):
    o_ref[...] = jnp.dot(a_ref[...], b_ref[...], preferred_element_type=F32).astype(o_ref.dtype)


def _mm_res_kernel(a_ref, b_ref, r_ref, o_ref):
    o_ref[...] = (r_ref[...] + jnp.dot(a_ref[...], b_ref[...], preferred_element_type=F32)).astype(o_ref.dtype)


def _matmul(a, b, res=None, tm_pref=640, tn_pref=1024, out_dtype=F32, name="matmul"):
    m, k = a.shape
    n = b.shape[1]
    tm = _tile(m, tm_pref, 16)
    tn = _tile(n, tn_pref, 2 * LANES if n % (2 * LANES) == 0 else LANES)
    a_spec = pl.BlockSpec((tm, k), lambda j, i: (i, 0))
    b_spec = pl.BlockSpec((k, tn), lambda j, i: (0, j))
    o_spec = pl.BlockSpec((tm, tn), lambda j, i: (i, j))
    if res is None:
        kern, specs, args = _mm_kernel, [a_spec, b_spec], (a, b)
    else:
        kern, specs, args = _mm_res_kernel, [a_spec, b_spec, o_spec], (a, b, res)
    return pl.pallas_call(
        kern, grid=(n // tn, m // tm), in_specs=specs, out_specs=o_spec,
        out_shape=jax.ShapeDtypeStruct((m, n), out_dtype),
        compiler_params=_cparams(("parallel", "parallel")), name=name)(*args)


def _rope_tables(pos, head_dim):
    rot = head_dim // 4
    half = rot // 2
    inv = ROPE_THETA ** (-jnp.arange(half, dtype=F32) / half)
    ang = pos.astype(F32)[:, None] * inv[None, :]
    cos, sin = jnp.cos(ang), jnp.sin(ang)
    n = pos.shape[0]
    ones = jnp.ones((n, head_dim - rot), F32)
    zr = jnp.zeros((n, head_dim - rot), F32)
    zh = jnp.zeros((n, half), F32)
    c = jnp.concatenate([cos, cos, ones], axis=1)
    sa = jnp.concatenate([-sin, zh, zr], axis=1)
    sb = jnp.concatenate([zh, sin, zr], axis=1)
    reps = LANES // head_dim
    return tuple(jnp.tile(t, (1, reps)) for t in (c, sa, sb))


def _rope(x, c, sa, sb, half):
    return x * c + pltpu.roll(x, LANES - half, 1) * sa + pltpu.roll(x, half, 1) * sb


def _prep_kernel(q_ref, k_ref, v_ref, qi_ref, kw_ref, c1_ref, sa1_ref, sb1_ref, c2_ref, sa2_ref, sb2_ref,
                 qo_ref, kf_ref, kb_ref, vb_ref, qio_ref, kif_ref, wi_ref):
    c1, sa1, sb1 = c1_ref[...], sa1_ref[...], sb1_ref[...]
    c2, sa2, sb2 = c2_ref[...], sa2_ref[...], sb2_ref[...]
    h1 = HEAD_DIM // 8
    h2 = IDX_DIM // 8
    for h in range(q_ref.shape[1] // LANES):
        sl = slice(h * LANES, (h + 1) * LANES)
        qo_ref[:, sl] = _rope(q_ref[:, sl], c1, sa1, sb1, h1).astype(BF16)
    for h in range(k_ref.shape[1] // LANES):
        sl = slice(h * LANES, (h + 1) * LANES)
        kr = _rope(k_ref[:, sl], c1, sa1, sb1, h1)
        kf_ref[:, sl] = kr
        kb_ref[:, sl] = kr.astype(BF16)
    vb_ref[...] = v_ref[...].astype(BF16)
    for h in range(qi_ref.shape[1] // LANES):
        sl = slice(h * LANES, (h + 1) * LANES)
        qio_ref[:, sl] = _rope(qi_ref[:, sl], c2, sa2, sb2, h2).astype(BF16)
    t = kw_ref[...]
    kif_ref[...] = _rope(t, c2, sa2, sb2, h2)[:, :IDX_DIM]
    wi_ref[...] = t[:, IDX_DIM:IDX_DIM + IDX_HEADS] * (IDX_HEADS ** -0.5)


def _prep(y, tabs1, tabs2, d, kvw, iqw, off_k, off_v, off_qi, off_kw):
    n = y.shape[0]
    tm = _tile(n, 128, 16)
    tab = pl.BlockSpec((tm, LANES), lambda i: (i, 0))
    in_specs = [
        pl.BlockSpec((tm, d), lambda i: (i, 0)),
        pl.BlockSpec((tm, kvw), lambda i: (i, off_k // kvw)),
        pl.BlockSpec((tm, kvw), lambda i: (i, off_v // kvw)),
        pl.BlockSpec((tm, iqw), lambda i: (i, off_qi // iqw)),
        pl.BlockSpec((tm, LANES), lambda i: (i, off_kw // LANES)),
        tab, tab, tab, tab, tab, tab,
    ]
    outs = [
        (d, BF16), (kvw, F32), (kvw, BF16), (kvw, BF16), (iqw, BF16), (IDX_DIM, F32), (IDX_HEADS, F32),
    ]
    return pl.pallas_call(
        _prep_kernel, grid=(n // tm,), in_specs=in_specs,
        out_specs=[pl.BlockSpec((tm, w), lambda i: (i, 0)) for w, _ in outs],
        out_shape=[jax.ShapeDtypeStruct((n, w), dt) for w, dt in outs],
        compiler_params=_cparams(("parallel",)), name="prep")(y, y, y, y, y, *tabs1, *tabs2)


def _num_k_tiles(i, tq, tk, q_off, s_valid):
    last_chunk_end = ((q_off + i * tq + tq - 1) // CHUNK + 1) * CHUNK
    return (jnp.minimum(last_chunk_end, s_valid) + tk - 1) // tk


def _idx_kernel(qi_ref, wi_ref, klo_ref, khi_ref, o_ref, key_ref, *, tq, tk, nk, q_off, s_valid, n_sel):
    i = pl.program_id(1)
    q0 = q_off + i * tq
    nkt = _num_k_tiles(i, tq, tk, q_off, s_valid)
    wi = wi_ref[...] * (IDX_DIM ** -0.5)
    wcols = [wi[:, h:h + 1] for h in range(IDX_HEADS)]
    q_chunk = (q0 + lax.broadcasted_iota(jnp.int32, (tq, tk), 0)) // CHUNK
    nt = (((1,), (1,)), ((), ()))

    def score_tile(j, carry):
        k0 = pl.multiple_of(j * tk, tk)
        klo = klo_ref[0, pl.ds(k0, tk), :]
        khi = khi_ref[0, pl.ds(k0, tk), :]
        acc = jnp.zeros((tq, tk), F32)
        for p in range(IDX_HEADS // 2):
            qp = qi_ref[:, p * LANES:(p + 1) * LANES]
            d0 = lax.dot_general(qp, klo, nt, preferred_element_type=F32)
            d1 = lax.dot_general(qp, khi, nt, preferred_element_type=F32)
            acc = acc + jnp.maximum(d0, 0.0) * wcols[2 * p]
            acc = acc + jnp.maximum(d1, 0.0) * wcols[2 * p + 1]
        k_pos = k0 + lax.broadcasted_iota(jnp.int32, (tq, tk), 1)
        vis = jnp.logical_and(k_pos // CHUNK <= q_chunk, k_pos < s_valid)
        bits = pltpu.bitcast(acc, jnp.int32)
        key = bits ^ ((bits >> 31) & 0x7FFFFFFF)
        key_ref[j] = jnp.where(vis, key, INT_MIN)
        return carry

    lax.fori_loop(0, nkt, score_tile, 0)

    def count(pred):
        def count_tile(j, acc):
            m = jnp.where(pred(j, key_ref[j]), 1.0, 0.0)
            part = m[:, 0:LANES]
            for c in range(1, tk // LANES):
                part = part + m[:, c * LANES:(c + 1) * LANES]
            return acc + part

        acc = lax.fori_loop(0, nkt, count_tile, jnp.zeros((tq, LANES), F32))
        return jnp.sum(acc, axis=1, keepdims=True)

    def bit_step(it, carry):
        r, cnt_r = carry
        cand = r | jnp.left_shift(jnp.int32(1), 31 - it)
        cs = cand ^ INT_MIN
        cnt = count(lambda j, key: key >= cs)
        take = cnt >= n_sel
        return jnp.where(take, cand, r), jnp.where(take, cnt, cnt_r)

    r, cnt_ge = lax.fori_loop(0, 32, bit_step,
                              (jnp.zeros((tq, 1), jnp.int32), jnp.full((tq, 1), float(n_sel), F32)))
    thr = jnp.maximum(r ^ INT_MIN, INT_MIN + 1)
    has_ties = jnp.max(cnt_ge) > float(n_sel)

    @pl.when(jnp.logical_not(has_ties))
    def _write_plain():
        def write_tile(j, carry):
            o_ref[0, j] = jnp.where(key_ref[j] >= thr, 0.0, NEG).astype(BF16)
            return carry

        lax.fori_loop(0, nkt, write_tile, 0)

    @pl.when(has_ties)
    def _write_tie_broken():
        need = float(n_sel) - count(lambda j, key: key > thr)
        lane = lax.broadcasted_iota(jnp.int32, (tq, tk), 1)

        def pos_step(it, p):
            cand = p | jnp.left_shift(jnp.int32(1), pos_bits - 1 - it)
            cnt = count(lambda j, key: jnp.logical_and(key == thr, lane + j * tk < cand))
            return jnp.where(cnt < need, cand, p)

        pos_bits = max(1, (nk * tk - 1).bit_length())
        p = lax.fori_loop(0, pos_bits, pos_step, jnp.zeros((tq, 1), jnp.int32))

        def write_tile(j, carry):
            key = key_ref[j]
            sel = jnp.logical_or(key > thr, jnp.logical_and(key == thr, lane + j * tk <= p))
            o_ref[0, j] = jnp.where(sel, 0.0, NEG).astype(BF16)
            return carry

        lax.fori_loop(0, nkt, write_tile, 0)

    def fill_tile(j, carry):
        o_ref[0, j] = jnp.full((tq, tk), NEG, BF16)
        return carry

    lax.fori_loop(nkt, nk, fill_tile, 0)


def _idx_mask(qi, wi, row_off, t, klo, khi, *, tq, tk, q_off, s_valid, n_sel):
    b, s_pad, _ = klo.shape
    iqw = qi.shape[1]
    nk = s_pad // tk
    assert row_off % tq == 0 and t % tq == 0
    qrow = lambda bb, i: (row_off // tq + bb * (t // tq) + i, 0)
    kern = functools.partial(_idx_kernel, tq=tq, tk=tk, nk=nk, q_off=q_off, s_valid=s_valid, n_sel=n_sel)
    return pl.pallas_call(
        kern, grid=(b, t // tq),
        in_specs=[
            pl.BlockSpec((tq, iqw), qrow),
            pl.BlockSpec((tq, IDX_HEADS), qrow),
            pl.BlockSpec((1, s_pad, LANES), lambda bb, i: (bb, 0, 0)),
            pl.BlockSpec((1, s_pad, LANES), lambda bb, i: (bb, 0, 0)),
        ],
        out_specs=pl.BlockSpec((1, nk, tq, tk), lambda bb, i: (bb, 0, i, 0)),
        out_shape=jax.ShapeDtypeStruct((b, nk, t, tk), BF16),
        scratch_shapes=[pltpu.VMEM((nk, tq, tk), jnp.int32)],
        compiler_params=_cparams(("parallel", "parallel")), name="idx_mask")(qi, wi, klo, khi)


def _attn_kernel(q_ref, k_ref, v_ref, b_ref, o_ref, qs_ref, m_ref, acc_ref,
                 *, tq, tk, nk, q_off, s_valid, nkv, grp):
    i = pl.program_id(1)
    j = pl.program_id(2)
    nkt = _num_k_tiles(i, tq, tk, q_off, s_valid)
    scale_log2e = (HEAD_DIM ** -0.5) * 1.4426950408889634
    nt = (((1,), (1,)), ((), ()))

    @pl.when(j == 0)
    def _init():
        for g in range(nkv):
            for hh in range(grp):
                h = g * grp + hh
                qs_ref[g, hh * tq:(hh + 1) * tq, :] = q_ref[:, h * HEAD_DIM:(h + 1) * HEAD_DIM]
        m_ref[...] = jnp.full(m_ref.shape, NEG, F32)
        acc_ref[...] = jnp.zeros(acc_ref.shape, F32)

    @pl.when(j < nkt)
    def _compute():
        bias = b_ref[0, 0].astype(F32)
        bias = jnp.concatenate([bias] * grp, axis=0)
        nc = tk // LANES
        ones = jnp.ones((tk, LANES), BF16)
        chunks = lambda a: [a[:, c * LANES:(c + 1) * LANES] for c in range(nc)]
        for g in range(nkv):
            kg = k_ref[:, g * HEAD_DIM:(g + 1) * HEAD_DIM]
            vg = jnp.concatenate([v_ref[:, g * HEAD_DIM:(g + 1) * HEAD_DIM], ones], axis=1)
            s = lax.dot_general(qs_ref[g], kg, nt, preferred_element_type=F32) * scale_log2e + bias
            m_prev = m_ref[g]
            m_new = jnp.maximum(m_prev, jnp.max(functools.reduce(jnp.maximum, chunks(s)), axis=1, keepdims=True))
            p = jnp.exp2(s - jnp.tile(m_new, (1, nc)))
            alpha = jnp.exp2(m_prev - m_new)
            acc_ref[g] = jnp.tile(alpha, (1, 2)) * acc_ref[g] + jnp.dot(p.astype(BF16), vg, preferred_element_type=F32)
            m_ref[g] = m_new

    @pl.when(j == nk - 1)
    def _finish():
        for g in range(nkv):
            for hh in range(grp):
                h = g * grp + hh
                rows = slice(hh * tq, (hh + 1) * tq)
                o_ref[:, h * HEAD_DIM:(h + 1) * HEAD_DIM] = acc_ref[g, rows, :HEAD_DIM] / acc_ref[g, rows, HEAD_DIM:]


def _attn_kernel_aliased(q_ref, k_ref, v_ref, b_ref, prev_ref, o_ref, *scratch, **kw):
    del prev_ref
    _attn_kernel(q_ref, k_ref, v_ref, b_ref, o_ref, *scratch, **kw)


def _attention(q, row_off, t, k, v, s_pad, bias, n_out, prev=None, *, tq, tk, q_off, s_valid):
    b = bias.shape[0]
    aw, kvw = q.shape[1], k.shape[1]
    nk = s_pad // tk
    nkv = kvw // HEAD_DIM
    grp = aw // kvw
    assert row_off % tq == 0 and t % tq == 0
    kw = dict(tq=tq, tk=tk, nk=nk, q_off=q_off, s_valid=s_valid, nkv=nkv, grp=grp)

    def kj(i, j):
        return jnp.minimum(j, _num_k_tiles(i, tq, tk, q_off, s_valid) - 1)

    qrow = lambda bb, i, j: (row_off // tq + bb * (t // tq) + i, 0)
    krow = lambda bb, i, j: (bb * nk + kj(i, j), 0)
    in_specs = [
        pl.BlockSpec((tq, aw), qrow),
        pl.BlockSpec((tk, kvw), krow),
        pl.BlockSpec((tk, kvw), krow),
        pl.BlockSpec((1, 1, tq, tk), lambda bb, i, j: (bb, kj(i, j), i, 0)),
    ]
    args = (q, k, v, bias)
    if prev is None:
        kern, aliases = functools.partial(_attn_kernel, **kw), {}
    else:
        kern, aliases = functools.partial(_attn_kernel_aliased, **kw), {4: 0}
        in_specs.append(pl.BlockSpec(memory_space=pl.ANY))
        args += (prev,)
    return pl.pallas_call(
        kern, grid=(b, t // tq, nk), in_specs=in_specs,
        out_specs=pl.BlockSpec((tq, aw), qrow),
        out_shape=jax.ShapeDtypeStruct((n_out, aw), F32),
        scratch_shapes=[
            pltpu.VMEM((nkv, grp * tq, HEAD_DIM), BF16),
            pltpu.VMEM((nkv, grp * tq, LANES), F32),
            pltpu.VMEM((nkv, grp * tq, 2 * HEAD_DIM), F32),
        ],
        input_output_aliases=aliases,
        compiler_params=_cparams(("parallel", "parallel", "arbitrary")), name="attention")(*args)


def _gelu(x):
    return 0.5 * x * (1.0 + lax.erf(x * (0.5 ** 0.5)))


def _lru_kernel(x_ref, gl_ref, gb_ref, c0_ref, h0_ref, cw_ref, cb_ref, wa_ref, ba_ref, wi_ref, bi_ref, lam_ref,
                o_ref, cn_ref, hn_ref, xp_ref, a_ref, b_ref, hs_ref, h_ref, *, tt, nblk, bw):
    t = pl.program_id(1)
    pad = SUBLANES

    @pl.when(t == 0)
    def _init():
        xp_ref[0:pad, :] = c0_ref[0]
        h_ref[...] = h0_ref[0]

    xp_ref[pad:pad + tt, :] = x_ref[...]
    cw = cw_ref[...]
    base = pad - (CONV_W - 1)
    xc = cb_ref[...] + xp_ref[base:base + tt, :] * cw[0:1]
    for jj in range(1, CONV_W):
        xc = xc + xp_ref[base + jj:base + jj + tt, :] * cw[jj:jj + 1]
    tail = xp_ref[tt:tt + pad, :]
    cn_ref[0] = tail
    xp_ref[0:pad, :] = tail

    xcb = xc.astype(BF16)
    ra, ri = [], []
    for n in range(nblk):
        xs = xcb[:, n * bw:(n + 1) * bw]
        ra.append(jnp.dot(xs, wa_ref[n], preferred_element_type=F32))
        ri.append(jnp.dot(xs, wi_ref[n], preferred_element_type=F32))
    r = jax.nn.sigmoid(jnp.concatenate(ra, axis=1) + ba_ref[...])
    ig = jax.nn.sigmoid(jnp.concatenate(ri, axis=1) + bi_ref[...])
    z = -lam_ref[...]
    softplus = jnp.maximum(z, 0.0) + jnp.log1p(jnp.exp(-jnp.abs(z)))
    log_a = (-LRU_C * r) * softplus
    x2 = 2.0 * log_a
    neg_expm1 = -jnp.tanh(0.5 * x2) * (jnp.exp(x2) + 1.0)
    a_ref[...] = jnp.exp(log_a)
    b_ref[...] = jnp.sqrt(neg_expm1) * (ig * xc)

    def step(s, h):
        h = a_ref[pl.ds(s, 1), :] * h + b_ref[pl.ds(s, 1), :]
        hs_ref[pl.ds(s, 1), :] = h
        return h

    h = lax.fori_loop(0, tt, step, h_ref[...], unroll=8)
    h_ref[...] = h
    hn_ref[0] = h
    o_ref[...] = jax.nn.sigmoid(gb_ref[...]) * (hs_ref[...] * _gelu(gl_ref[...]))


def _lru_kernel_aliased(*refs, **kw):
    _lru_kernel(*refs[:12], *refs[13:], **kw)


def _lru(y, row_off, b, t, off_x, off_gl, off_gb, conv0, h0, conv_w, conv_b, wa, ba, wi, bi, lam, n_out, prev=None,
         *, tt):
    c = conv_w.shape[1]
    nblk, bw = wa.shape[0], wa.shape[1]
    assert row_off % tt == 0 and t % tt == 0 and off_x % c == 0 and off_gl % c == 0 and off_gb % c == 0
    rows = lambda bb, i: row_off // tt + bb * (t // tt) + i
    col = lambda off: pl.BlockSpec((tt, c), lambda bb, i: (rows(bb, i), off // c))
    vec = pl.BlockSpec((1, c), lambda bb, i: (0, 0))
    wsp = pl.BlockSpec((nblk, bw, bw), lambda bb, i: (0, 0, 0))
    in_specs = [
        col(off_x), col(off_gl), col(off_gb),
        pl.BlockSpec((1, SUBLANES, c), lambda bb, i: (bb, 0, 0)),
        pl.BlockSpec((1, 1, c), lambda bb, i: (bb, 0, 0)),
        pl.BlockSpec((CONV_W, c), lambda bb, i: (0, 0)),
        vec, wsp, vec, wsp, vec, vec,
    ]
    args = (y, y, y, conv0, h0, conv_w, conv_b.reshape(1, c), wa, ba.reshape(1, c), wi, bi.reshape(1, c),
            lam.reshape(1, c))
    kw = dict(tt=tt, nblk=nblk, bw=bw)
    if prev is None:
        kern, aliases = functools.partial(_lru_kernel, **kw), {}
    else:
        kern, aliases = functools.partial(_lru_kernel_aliased, **kw), {12: 0}
        in_specs.append(pl.BlockSpec(memory_space=pl.ANY))
        args += (prev,)
    return pl.pallas_call(
        kern, grid=(b, t // tt), in_specs=in_specs,
        out_specs=[
            pl.BlockSpec((tt, c), lambda bb, i: (rows(bb, i), 0)),
            pl.BlockSpec((1, SUBLANES, c), lambda bb, i: (bb, 0, 0)),
            pl.BlockSpec((1, 1, c), lambda bb, i: (bb, 0, 0)),
        ],
        out_shape=[
            jax.ShapeDtypeStruct((n_out, c), F32),
            jax.ShapeDtypeStruct((b, SUBLANES, c), F32),
            jax.ShapeDtypeStruct((b, 1, c), F32),
        ],
        scratch_shapes=[
            pltpu.VMEM((tt + SUBLANES, c), F32),
            pltpu.VMEM((tt, c), F32),
            pltpu.VMEM((tt, c), F32),
            pltpu.VMEM((tt, c), F32),
            pltpu.VMEM((1, c), F32),
        ],
        input_output_aliases=aliases,
        compiler_params=_cparams(("arbitrary", "arbitrary")), name="rg_lru")(*args)


def _merge_kernel(ga_ref, att_ref, lru_ref, o_ref):
    o_ref[...] = (jax.nn.sigmoid(ga_ref[...]) * att_ref[...] + lru_ref[...]).astype(o_ref.dtype)


def _merge(y, off_ga, att, lru):
    n, d = att.shape
    tm = _tile(n, 128, 16)
    row = pl.BlockSpec((tm, d), lambda i: (i, 0))
    return pl.pallas_call(
        _merge_kernel, grid=(n // tm,),
        in_specs=[pl.BlockSpec((tm, d), lambda i: (i, off_ga // d)), row, row],
        out_specs=row, out_shape=jax.ShapeDtypeStruct((n, d), BF16),
        compiler_params=_cparams(("parallel",)), name="merge")(y, att, lru)


def _mem_attn_kernel(q_ref, mk_ref, mv_ref, o_ref, *, heads, hd):
    scale = hd ** -0.5
    nt = (((1,), (1,)), ((), ()))
    for h in range(heads):
        sl = slice(h * hd, (h + 1) * hd)
        s = lax.dot_general(q_ref[:, sl].astype(BF16), mk_ref[0, :, sl], nt, preferred_element_type=F32) * scale
        s = s - jnp.max(s, axis=1, keepdims=True)
        e = jnp.exp(s)
        p = e / jnp.sum(e, axis=1, keepdims=True)
        o_ref[:, sl] = jnp.dot(p.astype(BF16), mv_ref[0, :, sl], preferred_element_type=F32).astype(o_ref.dtype)


def _mem_attn_kernel_aliased(q_ref, mk_ref, mv_ref, prev_ref, o_ref, **kw):
    del prev_ref
    _mem_attn_kernel(q_ref, mk_ref, mv_ref, o_ref, **kw)


def _mem_attention(q, row_off, t, mk, mv, prev=None, *, tm):
    b, nm, w = mk.shape
    assert row_off % tm == 0 and t % tm == 0
    kw = dict(heads=MEM_HEADS, hd=w // MEM_HEADS)
    qrow = lambda bb, i: (row_off // tm + bb * (t // tm) + i, 0)
    in_specs = [
        pl.BlockSpec((tm, w), qrow),
        pl.BlockSpec((1, nm, w), lambda bb, i: (bb, 0, 0)),
        pl.BlockSpec((1, nm, w), lambda bb, i: (bb, 0, 0)),
    ]
    args = (q, mk, mv)
    if prev is None:
        kern, aliases = functools.partial(_mem_attn_kernel, **kw), {}
    else:
        kern, aliases = functools.partial(_mem_attn_kernel_aliased, **kw), {3: 0}
        in_specs.append(pl.BlockSpec(memory_space=pl.ANY))
        args += (prev,)
    return pl.pallas_call(
        kern, grid=(b, t // tm), in_specs=in_specs,
        out_specs=pl.BlockSpec((tm, w), qrow),
        out_shape=jax.ShapeDtypeStruct((q.shape[0], w), BF16),
        input_output_aliases=aliases,
        compiler_params=_cparams(("parallel", "parallel")), name="mem_attention")(*args)


def _top_values(x, k):
    vals = []
    for _ in range(k):
        m = jnp.max(x, axis=0, keepdims=True)
        vals.append(m)
        x = jnp.where(x == m, -jnp.inf, x)
    return vals


def _peer_route_kernel(q_ref, sub_ref, sa_ref, sb_ref, ca_ref, eb_ref, thr_ref, *, heads):
    nt = (((1,), (1,)), ((), ()))
    for h in range(heads):
        st = []
        sv = []
        for c in range(2):
            col = (h * 2 + c) * PEER_HALF
            qh = q_ref[:, col:col + PEER_HALF].astype(BF16)
            s = lax.dot_general(sub_ref[h * 2 + c], qh, nt, preferred_element_type=F32)
            st.append(s)
            sv.append(_top_values(s, PEER_TOPK))
        rows = [sv[0][a] + sv[1][b] for a in range(PEER_TOPK) for b in range(PEER_TOPK // (a + 1))]
        rows += [jnp.full_like(rows[0], -jnp.inf)] * (-len(rows) % SUBLANES)
        tv = _top_values(jnp.concatenate(rows, axis=0), PEER_TOPK)
        z = jnp.zeros_like(tv[0])
        for v in tv:
            z = z + jnp.exp(v - tv[0])
        sa_ref[h] = st[0]
        sb_ref[h] = st[1]
        ca_ref[h] = jnp.exp(st[0] - sv[0][0]) / z
        eb_ref[h] = jnp.exp(st[1] - sv[1][0])
        thr_ref[h] = tv[-1]


def _peer_route(qp, sub, *, tn):
    n = qp.shape[0]
    heads = PEER_HEADS
    kern = functools.partial(_peer_route_kernel, heads=heads)
    big = pl.BlockSpec((heads, PEER_NKEYS, tn), lambda i: (0, 0, i))
    big_shape = jax.ShapeDtypeStruct((heads, PEER_NKEYS, n), F32)
    return pl.pallas_call(
        kern, grid=(n // tn,),
        in_specs=[
            pl.BlockSpec((tn, qp.shape[1]), lambda i: (i, 0)),
            pl.BlockSpec(sub.shape, lambda i: (0, 0, 0)),
        ],
        out_specs=[big, big, big, big, pl.BlockSpec((heads, 1, tn), lambda i: (0, 0, i))],
        out_shape=[big_shape, big_shape, big_shape, big_shape, jax.ShapeDtypeStruct((heads, 1, n), F32)],
        compiler_params=_cparams(("parallel",)), name="peer_route")(qp, sub)


def _peer_dense_kernel(x_ref, u_ref, v_ref, sa_ref, ca_ref, sb_ref, eb_ref, thr_ref, o_ref, *, heads, rows):
    j = pl.program_id(1)

    @pl.when(j == 0)
    def _init():
        o_ref[...] = jnp.zeros(o_ref.shape, F32)

    nt = (((1,), (1,)), ((), ()))
    act = _gelu(lax.dot_general(u_ref[...], x_ref[...], nt, preferred_element_type=F32))
    tiles = []
    for r in range(rows):
        w = None
        for h in range(heads):
            s1 = sa_ref[r, h:h + 1, :]
            c1 = ca_ref[r, h:h + 1, :]
            sel = (s1 + sb_ref[h]) >= thr_ref[h:h + 1, :]
            term = jnp.where(sel, eb_ref[h], 0.0) * c1
            w = term if w is None else w + term
        tiles.append((w * act[r * PEER_NKEYS:(r + 1) * PEER_NKEYS, :]).astype(BF16))
    coef = jnp.concatenate(tiles, axis=0) if rows > 1 else tiles[0]
    tn_dims = (((0,), (0,)), ((), ()))
    o_ref[...] += lax.dot_general(coef, v_ref[...], tn_dims, preferred_element_type=F32)


def _peer_dense(xn, u, v, sa, ca, sb, eb, thr, *, tn, te):
    n, d = xn.shape
    e = u.shape[0]
    heads = sb.shape[0]
    rows = te // PEER_NKEYS
    kern = functools.partial(_peer_dense_kernel, heads=heads, rows=rows)
    once = pl.Buffered(1)
    row_blk = pl.BlockSpec((rows, heads, tn), lambda i, j: (j, 0, i))
    big = pl.BlockSpec((heads, PEER_NKEYS, tn), lambda i, j: (0, 0, i), pipeline_mode=once)
    return pl.pallas_call(
        kern, grid=(n // tn, e // te),
        in_specs=[
            pl.BlockSpec((tn, d), lambda i, j: (i, 0), pipeline_mode=once),
            pl.BlockSpec((te, d), lambda i, j: (j, 0)),
            pl.BlockSpec((te, d), lambda i, j: (j, 0)),
            row_blk, row_blk, big, big,
            pl.BlockSpec((heads, tn), lambda i, j: (0, i)),
        ],
        out_specs=pl.BlockSpec((tn, d), lambda i, j: (i, 0)),
        out_shape=jax.ShapeDtypeStruct((n, d), F32),
        compiler_params=_cparams(("parallel", "arbitrary")), name="peer_dense")(xn, u, v, sa, ca, sb, eb, thr)


def _pad_rows(x, n):
    if x.shape[0] == n:
        return x
    return jnp.pad(x, ((0, n - x.shape[0]),) + ((0, 0),) * (x.ndim - 1))


def _index_keys(ki, s_pad):
    kb = ki.astype(BF16)
    extra = s_pad - ki.shape[1]
    lo = jnp.pad(kb, ((0, 0), (0, extra), (0, LANES - IDX_DIM)))
    hi = jnp.pad(kb, ((0, 0), (0, extra), (LANES - IDX_DIM, 0)))
    return lo, hi


def kernel(x_prompt, x_sample, mem_prompt, cache_k, cache_v, cache_kidx, state_conv, state_lru, cache_mem_k, cache_mem_v, norm_mix_g, w_in, conv_w, conv_b, lru_wa, lru_ba, lru_wi, lru_bi, lru_lambda, w_out, norm_mem_g, norm_memkv_g, mem_wq, mem_wk, mem_wv, mem_wo, norm_ffn_g, peer_wq, peer_subkeys, peer_u, peer_v, norm_final_g):
    assert w_in.shape[0] == 1, "single-layer step"
    assert IDX_DIM * 2 == LANES and PEER_HALF == LANES and PEER_NKEYS == LANES and HEAD_DIM == LANES
    bp, tp, d = x_prompt.shape
    bs, ts, _ = x_sample.shape
    past = cache_k.shape[2]
    n_p, n_s = bp * tp, bs * ts
    n_tok = n_p + n_s
    n_pad = -(-n_tok // TOKEN_TILE) * TOKEN_TILE
    kvw = N_KV_HEADS * HEAD_DIM
    iqw = IDX_HEADS * IDX_DIM
    n_mem = mem_prompt.shape[1]
    mem_w = mem_wq.shape[2]

    x0 = _pad_rows(jnp.concatenate([x_prompt.reshape(n_p, d), x_sample.reshape(n_s, d)], axis=0), n_pad)
    pos = _pad_rows(jnp.concatenate([jnp.tile(jnp.arange(tp, dtype=jnp.int32), bp),
                                     jnp.tile(past + jnp.arange(ts, dtype=jnp.int32), bs)]), n_pad)

    w = w_in[0]
    o_k, o_v, o_qi, o_ki = d, d + kvw, d + 2 * kvw, d + 2 * kvw + iqw
    o_wi = o_ki + IDX_DIM
    o_xl = o_wi + IDX_HEADS
    assert o_xl + 4 * d == w.shape[1]
    y_xl, y_gl, y_ga, y_gb = d, 2 * d, 3 * d, 4 * d
    y_k, y_v, y_qi = 5 * d, 5 * d + kvw, 5 * d + 2 * kvw
    y_kw = y_qi + iqw
    y_width = -(-(y_kw + LANES) // (2 * LANES)) * (2 * LANES)
    w_all = jnp.concatenate([
        w[:, :d], w[:, o_xl:o_xl + 4 * d], w[:, o_k:o_ki],
        jnp.pad(w[:, o_ki:o_xl], ((0, 0), (0, y_width - y_kw - IDX_DIM - IDX_HEADS)))], axis=1).astype(BF16)

    xn = _rmsnorm(x0, norm_mix_g[0], BF16)
    y = _matmul(xn, w_all, name="in_proj")

    tabs1 = _rope_tables(pos, HEAD_DIM)
    tabs2 = _rope_tables(pos, IDX_DIM)
    q_bf, k_f, k_bf, v_bf, qi_bf, ki_f, wi = _prep(y, tabs1, tabs2, d, kvw, iqw, y_k, y_v, y_qi, y_kw)
    v_f = y[:n_tok, y_v:y_v + kvw]

    n_sel_p = min(TOPK_MAX, tp // 4)
    tk_p = _tile(tp, 512, LANES)
    klo, khi = _index_keys(ki_f[:n_p].reshape(bp, tp, IDX_DIM), tp)
    bias_p = _idx_mask(qi_bf, wi, 0, tp, klo, khi, tq=_tile(tp, 128, 16), tk=tk_p, q_off=0, s_valid=tp,
                       n_sel=n_sel_p)
    att = _attention(q_bf, 0, tp, k_bf, v_bf, tp, bias_p, n_pad,
                     tq=_tile(tp, 256, 16), tk=tk_p, q_off=0, s_valid=tp)

    s_s = past + ts
    s_pad = -(-s_s // LANES) * LANES
    n_sel_s = min(TOPK_MAX, s_s // 4)
    ki_s = jnp.concatenate([cache_kidx[0], ki_f[n_p:n_tok].reshape(bs, ts, IDX_DIM)], axis=1)
    klo_s, khi_s = _index_keys(ki_s, s_pad)
    bias_s = _idx_mask(qi_bf, wi, n_p, ts, klo_s, khi_s, tq=ts, tk=s_pad, q_off=past, s_valid=s_s, n_sel=n_sel_s)

    def with_cache(cache, new):
        full = jnp.concatenate([cache.reshape(bs, past, kvw).astype(BF16), new.reshape(bs, ts, kvw)], axis=1)
        return jnp.pad(full, ((0, 0), (0, s_pad - s_s), (0, 0))).reshape(bs * s_pad, kvw)

    att = _attention(q_bf, n_p, ts, with_cache(cache_k[0], k_bf[n_p:n_tok]), with_cache(cache_v[0], v_bf[n_p:n_tok]),
                     s_pad, bias_s, n_pad, prev=att, tq=ts, tk=s_pad, q_off=past, s_valid=s_s)

    wa_bf, wi_bf = lru_wa[0].astype(BF16), lru_wi[0].astype(BF16)
    lru_args = (conv_w[0], conv_b[0], wa_bf, lru_ba[0], wi_bf, lru_bi[0], lru_lambda[0])
    state_pad = SUBLANES - (CONV_W - 1)
    lru, cn_p, hn_p = _lru(y, 0, bp, tp, y_xl, y_gl, y_gb, jnp.zeros((bp, SUBLANES, d), F32),
                           jnp.zeros((bp, 1, d), F32), *lru_args, n_pad, tt=_tile(tp, 128, SUBLANES))
    conv0_s = jnp.pad(state_conv[0], ((0, 0), (state_pad, 0), (0, 0)))
    lru, cn_s, hn_s = _lru(y, n_p, bs, ts, y_xl, y_gl, y_gb, conv0_s, state_lru[0].reshape(bs, 1, d), *lru_args,
                           n_pad, prev=lru, tt=ts)
    if n_pad > n_tok:
        att = att.at[n_tok:].set(0.0)
        lru = lru.at[n_tok:].set(0.0)

    merged = _merge(y, y_ga, att, lru)
    x1 = _matmul(merged, w_out[0].astype(BF16), res=x0, name="out_proj")

    mem_n = _rmsnorm(mem_prompt.reshape(bp * n_mem, d), norm_memkv_g[0], BF16)
    w_kv = jnp.concatenate([mem_wk[0], mem_wv[0]], axis=1).astype(BF16)
    mkv = _matmul(mem_n, w_kv, name="mem_kv")
    mk_p = mkv[:, :mem_w].reshape(bp, n_mem, mem_w)
    mv_p = mkv[:, mem_w:].reshape(bp, n_mem, mem_w)
    xn2 = _rmsnorm(x1, norm_mem_g[0], BF16)
    qm = _matmul(xn2, mem_wq[0].astype(BF16), name="mem_q")
    om = _mem_attention(qm, 0, tp, mk_p.astype(BF16), mv_p.astype(BF16), tm=_tile(tp, 512, 16))
    om = _mem_attention(qm, n_p, ts, cache_mem_k[0].reshape(bs, n_mem, mem_w).astype(BF16),
                        cache_mem_v[0].reshape(bs, n_mem, mem_w).astype(BF16), prev=om, tm=ts)
    if n_pad > n_tok:
        om = om.at[n_tok:].set(0.0)
    x2 = _matmul(om, mem_wo[0].astype(BF16), res=x1, name="mem_out")

    xn3 = _rmsnorm(x2, norm_ffn_g[0], BF16)
    qp = _matmul(xn3, peer_wq[0].astype(BF16), name="peer_q")
    sub = peer_subkeys[0].reshape(PEER_HEADS * 2, PEER_NKEYS, PEER_HALF).astype(BF16)
    sa, sb, ca, eb, thr = _peer_route(qp, sub, tn=LANES)
    peer = _peer_dense(xn3, peer_u[0].astype(BF16), peer_v[0].astype(BF16),
                       jnp.transpose(sa, (1, 0, 2)), jnp.transpose(ca, (1, 0, 2)), sb, eb,
                       thr.reshape(PEER_HEADS, n_pad), tn=_tile(n_pad, 640, LANES), te=512)

    y_p, y_s = _final_norm(x2, peer, norm_final_g, n_p, n_s)

    kv_shape_p = (1, bp, tp, N_KV_HEADS, HEAD_DIM)
    kv_shape_s = (1, bs, ts, N_KV_HEADS, HEAD_DIM)
    tail = slice(SUBLANES - (CONV_W - 1), SUBLANES)
    return (
        y_p.reshape(bp, tp, d), y_s.reshape(bs, ts, d),
        k_f[:n_p].reshape(kv_shape_p), v_f[:n_p].reshape(kv_shape_p), ki_f[:n_p].reshape(1, bp, tp, IDX_DIM),
        cn_p[:, tail][None], hn_p.reshape(1, bp, d),
        mk_p.reshape(1, bp, n_mem, MEM_HEADS, mem_w // MEM_HEADS),
        mv_p.reshape(1, bp, n_mem, MEM_HEADS, mem_w // MEM_HEADS),
        k_f[n_p:n_tok].reshape(kv_shape_s), v_f[n_p:].reshape(kv_shape_s),
        ki_f[n_p:n_tok].reshape(1, bs, ts, IDX_DIM),
        cn_s[:, tail][None], hn_s.reshape(1, bs, d),
    )
```

```python
import functools

import jax
import jax.numpy as jnp
from jax import lax
from jax.experimental import pallas as pl
from jax.experimental.pallas import tpu as pltpu

F32 = jnp.float32
BF16 = jnp.bfloat16

CHUNK = 64
HEAD_DIM = 128
N_KV_HEADS = 8
IDX_HEADS = 16
IDX_DIM = 64
TOPK_MAX = 256
ROPE_THETA = 500000.0
LRU_BLOCKS = 16
CONV_W = 4
LRU_C = 8.0
MEM_HEADS = 4
PEER_HEADS = 8
PEER_NKEYS = 128
PEER_HALF = 128
PEER_TOPK = 16
EPS = 1e-6

LANES = 128
SUBLANES = 8
TOKEN_TILE = 128
VMEM_LIMIT = 56 * 1024 * 1024
NEG = -1e30
INT_MIN = -(2 ** 31)


def _cparams(sem, vmem=VMEM_LIMIT):
    return pltpu.CompilerParams(dimension_semantics=sem, vmem_limit_bytes=vmem)


def _tile(n, pref, mult):
    if n <= pref:
        return n
    t = (pref // mult) * mult
    while t > mult and n % t:
        t -= mult
    assert n % t == 0, (n, pref, mult)
    return t


def _rmsnorm_kernel(x_ref, g_ref, o_ref):
    x = x_ref[...]
    r = lax.rsqrt(jnp.mean(x * x, axis=-1, keepdims=True) + EPS)
    o_ref[...] = ((x * r) * g_ref[...]).astype(o_ref.dtype)


def _final_norm_kernel(x_ref, y_ref, g_ref, op_ref, os_ref, *, first_blocks):
    x = x_ref[...] + y_ref[...]
    r = lax.rsqrt(jnp.mean(x * x, axis=-1, keepdims=True) + EPS)
    val = (x * r) * g_ref[...]

    @pl.when(pl.program_id(0) < first_blocks)
    def _():
        op_ref[...] = val

    @pl.when(pl.program_id(0) >= first_blocks)
    def _():
        os_ref[...] = val


def _final_norm(x, y, g, n_first, n_second):
    d = x.shape[1]
    tm = _tile(n_second, 128, SUBLANES)
    assert n_first % tm == 0
    fb = n_first // tm
    row = pl.BlockSpec((tm, d), lambda i: (i, 0))
    return pl.pallas_call(
        functools.partial(_final_norm_kernel, first_blocks=fb), grid=(fb + n_second // tm,),
        in_specs=[row, row, pl.BlockSpec((1, d), lambda i: (0, 0))],
        out_specs=[pl.BlockSpec((tm, d), lambda i: (jnp.minimum(i, fb - 1), 0)),
                   pl.BlockSpec((tm, d), lambda i: (jnp.maximum(i - fb, 0), 0))],
        out_shape=[jax.ShapeDtypeStruct((n_first, d), F32), jax.ShapeDtypeStruct((n_second, d), F32)],
        compiler_params=_cparams(("arbitrary",)), name="final_norm")(x, y, g.reshape(1, d))


def _rmsnorm(x, g, out_dtype):
    m, d = x.shape
    tm = _tile(m, 640, SUBLANES)
    row = pl.BlockSpec((tm, d), lambda i: (i, 0))
    return pl.pallas_call(
        _rmsnorm_kernel, grid=(m // tm,), in_specs=[row, pl.BlockSpec((1, d), lambda i: (0, 0))], out_specs=row,
        out_shape=jax.ShapeDtypeStruct((m, d), out_dtype),
        compiler_params=_cparams(("parallel",)), name="rmsnorm")(x, g.reshape(1, d))


def _mm_kernel(a_ref, b_ref, o_ref):
    o_ref[...] = jnp.dot(a_ref[...], b_ref[...], preferred_element_type=F32).astype(o_ref.dtype)


def _mm_res_kernel(a_ref, b_ref, r_ref, o_ref):
    o_ref[...] = (r_ref[...] + jnp.dot(a_ref[...], b_ref[...], preferred_element_type=F32)).astype(o_ref.dtype)


def _matmul(a, b, res=None, tm_pref=640, tn_pref=1024, out_dtype=F32, name="matmul"):
    m, k = a.shape
    n = b.shape[1]
    tm = _tile(m, tm_pref, 16)
    tn = _tile(n, tn_pref, 2 * LANES if n % (2 * LANES) == 0 else LANES)
    a_spec = pl.BlockSpec((tm, k), lambda j, i: (i, 0))
    b_spec = pl.BlockSpec((k, tn), lambda j, i: (0, j))
    o_spec = pl.BlockSpec((tm, tn), lambda j, i: (i, j))
    if res is None:
        kern, specs, args = _mm_kernel, [a_spec, b_spec], (a, b)
    else:
        kern, specs, args = _mm_res_kernel, [a_spec, b_spec, o_spec], (a, b, res)
    return pl.pallas_call(
        kern, grid=(n // tn, m // tm), in_specs=specs, out_specs=o_spec,
        out_shape=jax.ShapeDtypeStruct((m, n), out_dtype),
        compiler_params=_cparams(("parallel", "parallel")), name=name)(*args)


def _rope_tables(pos, head_dim):
    rot = head_dim // 4
    half = rot // 2
    inv = ROPE_THETA ** (-jnp.arange(half, dtype=F32) / half)
    ang = pos.astype(F32)[:, None] * inv[None, :]
    cos, sin = jnp.cos(ang), jnp.sin(ang)
    n = pos.shape[0]
    ones = jnp.ones((n, head_dim - rot), F32)
    zr = jnp.zeros((n, head_dim - rot), F32)
    zh = jnp.zeros((n, half), F32)
    c = jnp.concatenate([cos, cos, ones], axis=1)
    sa = jnp.concatenate([-sin, zh, zr], axis=1)
    sb = jnp.concatenate([zh, sin, zr], axis=1)
    reps = LANES // head_dim
    return tuple(jnp.tile(t, (1, reps)) for t in (c, sa, sb))


def _rope(x, c, sa, sb, half):
    return x * c + pltpu.roll(x, LANES - half, 1) * sa + pltpu.roll(x, half, 1) * sb


def _prep_kernel(q_ref, k_ref, v_ref, qi_ref, kw_ref, c1_ref, sa1_ref, sb1_ref, c2_ref, sa2_ref, sb2_ref,
                 qo_ref, kf_ref, kb_ref, vb_ref, qio_ref, kif_ref, wi_ref):
    c1, sa1, sb1 = c1_ref[...], sa1_ref[...], sb1_ref[...]
    c2, sa2, sb2 = c2_ref[...], sa2_ref[...], sb2_ref[...]
    h1 = HEAD_DIM // 8
    h2 = IDX_DIM // 8
    for h in range(q_ref.shape[1] // LANES):
        sl = slice(h * LANES, (h + 1) * LANES)
        qo_ref[:, sl] = _rope(q_ref[:, sl], c1, sa1, sb1, h1).astype(BF16)
    for h in range(k_ref.shape[1] // LANES):
        sl = slice(h * LANES, (h + 1) * LANES)
        kr = _rope(k_ref[:, sl], c1, sa1, sb1, h1)
        kf_ref[:, sl] = kr
        kb_ref[:, sl] = kr.astype(BF16)
    vb_ref[...] = v_ref[...].astype(BF16)
    for h in range(qi_ref.shape[1] // LANES):
        sl = slice(h * LANES, (h + 1) * LANES)
        qio_ref[:, sl] = _rope(qi_ref[:, sl], c2, sa2, sb2, h2).astype(BF16)
    t = kw_ref[...]
    kif_ref[...] = _rope(t, c2, sa2, sb2, h2)[:, :IDX_DIM]
    wi_ref[...] = t[:, IDX_DIM:IDX_DIM + IDX_HEADS] * (IDX_HEADS ** -0.5)


def _prep(y, tabs1, tabs2, d, kvw, iqw, off_k, off_v, off_qi, off_kw):
    n = y.shape[0]
    tm = _tile(n, 128, 16)
    tab = pl.BlockSpec((tm, LANES), lambda i: (i, 0))
    in_specs = [
        pl.BlockSpec((tm, d), lambda i: (i, 0)),
        pl.BlockSpec((tm, kvw), lambda i: (i, off_k // kvw)),
        pl.BlockSpec((tm, kvw), lambda i: (i, off_v // kvw)),
        pl.BlockSpec((tm, iqw), lambda i: (i, off_qi // iqw)),
        pl.BlockSpec((tm, LANES), lambda i: (i, off_kw // LANES)),
        tab, tab, tab, tab, tab, tab,
    ]
    outs = [
        (d, BF16), (kvw, F32), (kvw, BF16), (kvw, BF16), (iqw, BF16), (IDX_DIM, F32), (IDX_HEADS, F32),
    ]
    return pl.pallas_call(
        _prep_kernel, grid=(n // tm,), in_specs=in_specs,
        out_specs=[pl.BlockSpec((tm, w), lambda i: (i, 0)) for w, _ in outs],
        out_shape=[jax.ShapeDtypeStruct((n, w), dt) for w, dt in outs],
        compiler_params=_cparams(("parallel",)), name="prep")(y, y, y, y, y, *tabs1, *tabs2)


def _num_k_tiles(i, tq, tk, q_off, s_valid):
    last_chunk_end = ((q_off + i * tq + tq - 1) // CHUNK + 1) * CHUNK
    return (jnp.minimum(last_chunk_end, s_valid) + tk - 1) // tk


def _idx_kernel(qi_ref, wi_ref, klo_ref, khi_ref, o_ref, key_ref, *, tq, tk, nk, q_off, s_valid, n_sel):
    i = pl.program_id(1)
    q0 = q_off + i * tq
    nkt = _num_k_tiles(i, tq, tk, q_off, s_valid)
    wi = wi_ref[...] * (IDX_DIM ** -0.5)
    wcols = [wi[:, h:h + 1] for h in range(IDX_HEADS)]
    q_chunk = (q0 + lax.broadcasted_iota(jnp.int32, (tq, tk), 0)) // CHUNK
    nt = (((1,), (1,)), ((), ()))

    def score_tile(j, carry):
        k0 = pl.multiple_of(j * tk, tk)
        klo = klo_ref[0, pl.ds(k0, tk), :]
        khi = khi_ref[0, pl.ds(k0, tk), :]
        acc = jnp.zeros((tq, tk), F32)
        for p in range(IDX_HEADS // 2):
            qp = qi_ref[:, p * LANES:(p + 1) * LANES]
            d0 = lax.dot_general(qp, klo, nt, preferred_element_type=F32)
            d1 = lax.dot_general(qp, khi, nt, preferred_element_type=F32)
            acc = acc + jnp.maximum(d0, 0.0) * wcols[2 * p]
            acc = acc + jnp.maximum(d1, 0.0) * wcols[2 * p + 1]
        k_pos = k0 + lax.broadcasted_iota(jnp.int32, (tq, tk), 1)
        vis = jnp.logical_and(k_pos // CHUNK <= q_chunk, k_pos < s_valid)
        bits = pltpu.bitcast(acc, jnp.int32)
        key = bits ^ ((bits >> 31) & 0x7FFFFFFF)
        key_ref[j] = jnp.where(vis, key, INT_MIN)
        return carry

    lax.fori_loop(0, nkt, score_tile, 0)

    def count(pred):
        def count_tile(j, acc):
            m = jnp.where(pred(j, key_ref[j]), 1.0, 0.0)
            part = m[:, 0:LANES]
            for c in range(1, tk // LANES):
                part = part + m[:, c * LANES:(c + 1) * LANES]
            return acc + part

        acc = lax.fori_loop(0, nkt, count_tile, jnp.zeros((tq, LANES), F32))
        return jnp.sum(acc, axis=1, keepdims=True)

    def bit_step(it, carry):
        r, cnt_r = carry
        cand = r | jnp.left_shift(jnp.int32(1), 31 - it)
        cs = cand ^ INT_MIN
        cnt = count(lambda j, key: key >= cs)
        take = cnt >= n_sel
        return jnp.where(take, cand, r), jnp.where(take, cnt, cnt_r)

    r, cnt_ge = lax.fori_loop(0, 32, bit_step,
                              (jnp.zeros((tq, 1), jnp.int32), jnp.full((tq, 1), float(n_sel), F32)))
    thr = jnp.maximum(r ^ INT_MIN, INT_MIN + 1)
    has_ties = jnp.max(cnt_ge) > float(n_sel)

    @pl.when(jnp.logical_not(has_ties))
    def _write_plain():
        def write_tile(j, carry):
            o_ref[0, j] = jnp.where(key_ref[j] >= thr, 0.0, NEG).astype(BF16)
            return carry

        lax.fori_loop(0, nkt, write_tile, 0)

    @pl.when(has_ties)
    def _write_tie_broken():
        need = float(n_sel) - count(lambda j, key: key > thr)
        lane = lax.broadcasted_iota(jnp.int32, (tq, tk), 1)

        def pos_step(it, p):
            cand = p | jnp.left_shift(jnp.int32(1), pos_bits - 1 - it)
            cnt = count(lambda j, key: jnp.logical_and(key == thr, lane + j * tk < cand))
            return jnp.where(cnt < need, cand, p)

        pos_bits = max(1, (nk * tk - 1).bit_length())
        p = lax.fori_loop(0, pos_bits, pos_step, jnp.zeros((tq, 1), jnp.int32))

        def write_tile(j, carry):
            key = key_ref[j]
            sel = jnp.logical_or(key > thr, jnp.logical_and(key == thr, lane + j * tk <= p))
            o_ref[0, j] = jnp.where(sel, 0.0, NEG).astype(BF16)
            return carry

        lax.fori_loop(0, nkt, write_tile, 0)

    def fill_tile(j, carry):
        o_ref[0, j] = jnp.full((tq, tk), NEG, BF16)
        return carry

    lax.fori_loop(nkt, nk, fill_tile, 0)


def _idx_mask(qi, wi, row_off, t, klo, khi, *, tq, tk, q_off, s_valid, n_sel):
    b, s_pad, _ = klo.shape
    iqw = qi.shape[1]
    nk = s_pad // tk
    assert row_off % tq == 0 and t % tq == 0
    qrow = lambda bb, i: (row_off // tq + bb * (t // tq) + i, 0)
    kern = functools.partial(_idx_kernel, tq=tq, tk=tk, nk=nk, q_off=q_off, s_valid=s_valid, n_sel=n_sel)
    return pl.pallas_call(
        kern, grid=(b, t // tq),
        in_specs=[
            pl.BlockSpec((tq, iqw), qrow),
            pl.BlockSpec((tq, IDX_HEADS), qrow),
            pl.BlockSpec((1, s_pad, LANES), lambda bb, i: (bb, 0, 0)),
            pl.BlockSpec((1, s_pad, LANES), lambda bb, i: (bb, 0, 0)),
        ],
        out_specs=pl.BlockSpec((1, nk, tq, tk), lambda bb, i: (bb, 0, i, 0)),
        out_shape=jax.ShapeDtypeStruct((b, nk, t, tk), BF16),
        scratch_shapes=[pltpu.VMEM((nk, tq, tk), jnp.int32)],
        compiler_params=_cparams(("parallel", "parallel")), name="idx_mask")(qi, wi, klo, khi)


def _attn_kernel(q_ref, k_ref, v_ref, b_ref, o_ref, qs_ref, m_ref, acc_ref,
                 *, tq, tk, nk, q_off, s_valid, nkv, grp):
    i = pl.program_id(1)
    j = pl.program_id(2)
    nkt = _num_k_tiles(i, tq, tk, q_off, s_valid)
    scale_log2e = (HEAD_DIM ** -0.5) * 1.4426950408889634
    nt = (((1,), (1,)), ((), ()))

    @pl.when(j == 0)
    def _init():
        for g in range(nkv):
            for hh in range(grp):
                h = g * grp + hh
                qs_ref[g, hh * tq:(hh + 1) * tq, :] = q_ref[:, h * HEAD_DIM:(h + 1) * HEAD_DIM]
        m_ref[...] = jnp.full(m_ref.shape, NEG, F32)
        acc_ref[...] = jnp.zeros(acc_ref.shape, F32)

    @pl.when(j < nkt)
    def _compute():
        bias = b_ref[0, 0].astype(F32)
        bias = jnp.concatenate([bias] * grp, axis=0)
        nc = tk // LANES
        ones = jnp.ones((tk, LANES), BF16)
        chunks = lambda a: [a[:, c * LANES:(c + 1) * LANES] for c in range(nc)]
        for g in range(nkv):
            kg = k_ref[:, g * HEAD_DIM:(g + 1) * HEAD_DIM]
            vg = jnp.concatenate([v_ref[:, g * HEAD_DIM:(g + 1) * HEAD_DIM], ones], axis=1)
            s = lax.dot_general(qs_ref[g], kg, nt, preferred_element_type=F32) * scale_log2e + bias
            m_prev = m_ref[g]
            m_new = jnp.maximum(m_prev, jnp.max(functools.reduce(jnp.maximum, chunks(s)), axis=1, keepdims=True))
            p = jnp.exp2(s - jnp.tile(m_new, (1, nc)))
            alpha = jnp.exp2(m_prev - m_new)
            acc_ref[g] = jnp.tile(alpha, (1, 2)) * acc_ref[g] + jnp.dot(p.astype(BF16), vg, preferred_element_type=F32)
            m_ref[g] = m_new

    @pl.when(j == nk - 1)
    def _finish():
        for g in range(nkv):
            for hh in range(grp):
                h = g * grp + hh
                rows = slice(hh * tq, (hh + 1) * tq)
                o_ref[:, h * HEAD_DIM:(h + 1) * HEAD_DIM] = acc_ref[g, rows, :HEAD_DIM] / acc_ref[g, rows, HEAD_DIM:]


def _attn_kernel_aliased(q_ref, k_ref, v_ref, b_ref, prev_ref, o_ref, *scratch, **kw):
    del prev_ref
    _attn_kernel(q_ref, k_ref, v_ref, b_ref, o_ref, *scratch, **kw)


def _attention(q, row_off, t, k, v, s_pad, bias, n_out, prev=None, *, tq, tk, q_off, s_valid):
    b = bias.shape[0]
    aw, kvw = q.shape[1], k.shape[1]
    nk = s_pad // tk
    nkv = kvw // HEAD_DIM
    grp = aw // kvw
    assert row_off % tq == 0 and t % tq == 0
    kw = dict(tq=tq, tk=tk, nk=nk, q_off=q_off, s_valid=s_valid, nkv=nkv, grp=grp)

    def kj(i, j):
        return jnp.minimum(j, _num_k_tiles(i, tq, tk, q_off, s_valid) - 1)

    qrow = lambda bb, i, j: (row_off // tq + bb * (t // tq) + i, 0)
    krow = lambda bb, i, j: (bb * nk + kj(i, j), 0)
    in_specs = [
        pl.BlockSpec((tq, aw), qrow),
        pl.BlockSpec((tk, kvw), krow),
        pl.BlockSpec((tk, kvw), krow),
        pl.BlockSpec((1, 1, tq, tk), lambda bb, i, j: (bb, kj(i, j), i, 0)),
    ]
    args = (q, k, v, bias)
    if prev is None:
        kern, aliases = functools.partial(_attn_kernel, **kw), {}
    else:
        kern, aliases = functools.partial(_attn_kernel_aliased, **kw), {4: 0}
        in_specs.append(pl.BlockSpec(memory_space=pl.ANY))
        args += (prev,)
    return pl.pallas_call(
        kern, grid=(b, t // tq, nk), in_specs=in_specs,
        out_specs=pl.BlockSpec((tq, aw), qrow),
        out_shape=jax.ShapeDtypeStruct((n_out, aw), F32),
        scratch_shapes=[
            pltpu.VMEM((nkv, grp * tq, HEAD_DIM), BF16),
            pltpu.VMEM((nkv, grp * tq, LANES), F32),
            pltpu.VMEM((nkv, grp * tq, 2 * HEAD_DIM), F32),
        ],
        input_output_aliases=aliases,
        compiler_params=_cparams(("parallel", "parallel", "arbitrary")), name="attention")(*args)


def _gelu(x):
    return 0.5 * x * (1.0 + lax.erf(x * (0.5 ** 0.5)))


def _lru_kernel(x_ref, gl_ref, gb_ref, c0_ref, h0_ref, cw_ref, cb_ref, wa_ref, ba_ref, wi_ref, bi_ref, lam_ref,
                o_ref, cn_ref, hn_ref, xp_ref, a_ref, b_ref, hs_ref, h_ref, *, tt, nblk, bw):
    t = pl.program_id(1)
    pad = SUBLANES

    @pl.when(t == 0)
    def _init():
        xp_ref[0:pad, :] = c0_ref[0]
        h_ref[...] = h0_ref[0]

    xp_ref[pad:pad + tt, :] = x_ref[...]
    cw = cw_ref[...]
    base = pad - (CONV_W - 1)
    xc = cb_ref[...] + xp_ref[base:base + tt, :] * cw[0:1]
    for jj in range(1, CONV_W):
        xc = xc + xp_ref[base + jj:base + jj + tt, :] * cw[jj:jj + 1]
    tail = xp_ref[tt:tt + pad, :]
    cn_ref[0] = tail
    xp_ref[0:pad, :] = tail

    xcb = xc.astype(BF16)
    ra, ri = [], []
    for n in range(nblk):
        xs = xcb[:, n * bw:(n + 1) * bw]
        ra.append(jnp.dot(xs, wa_ref[n], preferred_element_type=F32))
        ri.append(jnp.dot(xs, wi_ref[n], preferred_element_type=F32))
    r = jax.nn.sigmoid(jnp.concatenate(ra, axis=1) + ba_ref[...])
    ig = jax.nn.sigmoid(jnp.concatenate(ri, axis=1) + bi_ref[...])
    z = -lam_ref[...]
    softplus = jnp.maximum(z, 0.0) + jnp.log1p(jnp.exp(-jnp.abs(z)))
    log_a = (-LRU_C * r) * softplus
    x2 = 2.0 * log_a
    neg_expm1 = -jnp.tanh(0.5 * x2) * (jnp.exp(x2) + 1.0)
    a_ref[...] = jnp.exp(log_a)
    b_ref[...] = jnp.sqrt(neg_expm1) * (ig * xc)

    def step(s, h):
        h = a_ref[pl.ds(s, 1), :] * h + b_ref[pl.ds(s, 1), :]
        hs_ref[pl.ds(s, 1), :] = h
        return h

    h = lax.fori_loop(0, tt, step, h_ref[...], unroll=8)
    h_ref[...] = h
    hn_ref[0] = h
    o_ref[...] = jax.nn.sigmoid(gb_ref[...]) * (hs_ref[...] * _gelu(gl_ref[...]))


def _lru_kernel_aliased(*refs, **kw):
    _lru_kernel(*refs[:12], *refs[13:], **kw)


def _lru(y, row_off, b, t, off_x, off_gl, off_gb, conv0, h0, conv_w, conv_b, wa, ba, wi, bi, lam, n_out, prev=None,
         *, tt):
    c = conv_w.shape[1]
    nblk, bw = wa.shape[0], wa.shape[1]
    assert row_off % tt == 0 and t % tt == 0 and off_x % c == 0 and off_gl % c == 0 and off_gb % c == 0
    rows = lambda bb, i: row_off // tt + bb * (t // tt) + i
    col = lambda off: pl.BlockSpec((tt, c), lambda bb, i: (rows(bb, i), off // c))
    vec = pl.BlockSpec((1, c), lambda bb, i: (0, 0))
    wsp = pl.BlockSpec((nblk, bw, bw), lambda bb, i: (0, 0, 0))
    in_specs = [
        col(off_x), col(off_gl), col(off_gb),
        pl.BlockSpec((1, SUBLANES, c), lambda bb, i: (bb, 0, 0)),
        pl.BlockSpec((1, 1, c), lambda bb, i: (bb, 0, 0)),
        pl.BlockSpec((CONV_W, c), lambda bb, i: (0, 0)),
        vec, wsp, vec, wsp, vec, vec,
    ]
    args = (y, y, y, conv0, h0, conv_w, conv_b.reshape(1, c), wa, ba.reshape(1, c), wi, bi.reshape(1, c),
            lam.reshape(1, c))
    kw = dict(tt=tt, nblk=nblk, bw=bw)
    if prev is None:
        kern, aliases = functools.partial(_lru_kernel, **kw), {}
    else:
        kern, aliases = functools.partial(_lru_kernel_aliased, **kw), {12: 0}
        in_specs.append(pl.BlockSpec(memory_space=pl.ANY))
        args += (prev,)
    return pl.pallas_call(
        kern, grid=(b, t // tt), in_specs=in_specs,
        out_specs=[
            pl.BlockSpec((tt, c), lambda bb, i: (rows(bb, i), 0)),
            pl.BlockSpec((1, SUBLANES, c), lambda bb, i: (bb, 0, 0)),
            pl.BlockSpec((1, 1, c), lambda bb, i: (bb, 0, 0)),
        ],
        out_shape=[
            jax.ShapeDtypeStruct((n_out, c), F32),
            jax.ShapeDtypeStruct((b, SUBLANES, c), F32),
            jax.ShapeDtypeStruct((b, 1, c), F32),
        ],
        scratch_shapes=[
            pltpu.VMEM((tt + SUBLANES, c), F32),
            pltpu.VMEM((tt, c), F32),
            pltpu.VMEM((tt, c), F32),
            pltpu.VMEM((tt, c), F32),
            pltpu.VMEM((1, c), F32),
        ],
        input_output_aliases=aliases,
        compiler_params=_cparams(("arbitrary", "arbitrary")), name="rg_lru")(*args)


def _merge_kernel(ga_ref, att_ref, lru_ref, o_ref):
    o_ref[...] = (jax.nn.sigmoid(ga_ref[...]) * att_ref[...] + lru_ref[...]).astype(o_ref.dtype)


def _merge(y, off_ga, att, lru):
    n, d = att.shape
    tm = _tile(n, 128, 16)
    row = pl.BlockSpec((tm, d), lambda i: (i, 0))
    return pl.pallas_call(
        _merge_kernel, grid=(n // tm,),
        in_specs=[pl.BlockSpec((tm, d), lambda i: (i, off_ga // d)), row, row],
        out_specs=row, out_shape=jax.ShapeDtypeStruct((n, d), BF16),
        compiler_params=_cparams(("parallel",)), name="merge")(y, att, lru)


def _mem_attn_kernel(q_ref, mk_ref, mv_ref, o_ref, *, heads, hd):
    scale = hd ** -0.5
    nt = (((1,), (1,)), ((), ()))
    for h in range(heads):
        sl = slice(h * hd, (h + 1) * hd)
        s = lax.dot_general(q_ref[:, sl].astype(BF16), mk_ref[0, :, sl], nt, preferred_element_type=F32) * scale
        s = s - jnp.max(s, axis=1, keepdims=True)
        e = jnp.exp(s)
        p = e / jnp.sum(e, axis=1, keepdims=True)
        o_ref[:, sl] = jnp.dot(p.astype(BF16), mv_ref[0, :, sl], preferred_element_type=F32).astype(o_ref.dtype)


def _mem_attn_kernel_aliased(q_ref, mk_ref, mv_ref, prev_ref, o_ref, **kw):
    del prev_ref
    _mem_attn_kernel(q_ref, mk_ref, mv_ref, o_ref, **kw)


def _mem_attention(q, row_off, t, mk, mv, prev=None, *, tm):
    b, nm, w = mk.shape
    assert row_off % tm == 0 and t % tm == 0
    kw = dict(heads=MEM_HEADS, hd=w // MEM_HEADS)
    qrow = lambda bb, i: (row_off // tm + bb * (t // tm) + i, 0)
    in_specs = [
        pl.BlockSpec((tm, w), qrow),
        pl.BlockSpec((1, nm, w), lambda bb, i: (bb, 0, 0)),
        pl.BlockSpec((1, nm, w), lambda bb, i: (bb, 0, 0)),
    ]
    args = (q, mk, mv)
    if prev is None:
        kern, aliases = functools.partial(_mem_attn_kernel, **kw), {}
    else:
        kern, aliases = functools.partial(_mem_attn_kernel_aliased, **kw), {3: 0}
        in_specs.append(pl.BlockSpec(memory_space=pl.ANY))
        args += (prev,)
    return pl.pallas_call(
        kern, grid=(b, t // tm), in_specs=in_specs,
        out_specs=pl.BlockSpec((tm, w), qrow),
        out_shape=jax.ShapeDtypeStruct((q.shape[0], w), BF16),
        input_output_aliases=aliases,
        compiler_params=_cparams(("parallel", "parallel")), name="mem_attention")(*args)


def _top_values(x, k):
    vals = []
    for _ in range(k):
        m = jnp.max(x, axis=0, keepdims=True)
        vals.append(m)
        x = jnp.where(x == m, -jnp.inf, x)
    return vals


def _peer_route_kernel(q_ref, sub_ref, ta_ref, sb_ref, ca_ref, eb_ref, *, heads):
    nt = (((1,), (1,)), ((), ()))
    k = PEER_TOPK
    for h in range(heads):
        st = []
        sv = []
        for c in range(2):
            col = (h * 2 + c) * PEER_HALF
            qh = q_ref[:, col:col + PEER_HALF].astype(BF16)
            s = lax.dot_general(sub_ref[h * 2 + c], qh, nt, preferred_element_type=F32)
            st.append(s)
            sv.append(_top_values(s, k + 1))
        rows = [sv[0][a] + sv[1][b] for a in range(k + 1) for b in range((k + 1) // (a + 1))]
        rows += [jnp.full_like(rows[0], -jnp.inf)] * (-len(rows) % SUBLANES)
        tv = _top_values(jnp.concatenate(rows, axis=0), k + 1)
        z = jnp.zeros_like(tv[0])
        for v in tv[:k]:
            z = z + jnp.exp(v - tv[0])
        ta_ref[h] = 0.5 * (tv[k - 1] + tv[k]) - st[0]
        sb_ref[h] = st[1]
        ca_ref[h] = jnp.exp(st[0] - sv[0][0]) / z
        eb_ref[h] = jnp.exp(st[1] - sv[1][0])


def _peer_route(qp, sub, *, tn):
    n = qp.shape[0]
    heads = PEER_HEADS
    kern = functools.partial(_peer_route_kernel, heads=heads)
    big = pl.BlockSpec((heads, PEER_NKEYS, tn), lambda i: (0, 0, i))
    big_shape = jax.ShapeDtypeStruct((heads, PEER_NKEYS, n), F32)
    return pl.pallas_call(
        kern, grid=(n // tn,),
        in_specs=[
            pl.BlockSpec((tn, qp.shape[1]), lambda i: (i, 0)),
            pl.BlockSpec(sub.shape, lambda i: (0, 0, 0)),
        ],
        out_specs=[big, big, big, big],
        out_shape=[big_shape, big_shape, big_shape, big_shape],
        compiler_params=_cparams(("parallel",)), name="peer_route")(qp, sub)


def _peer_dense_kernel(x_ref, u_ref, v_ref, ta_ref, ca_ref, sb_ref, eb_ref, o_ref, *, heads, rows):
    j = pl.program_id(1)

    @pl.when(j == 0)
    def _init():
        o_ref[...] = jnp.zeros(o_ref.shape, F32)

    nt = (((1,), (1,)), ((), ()))
    act = _gelu(lax.dot_general(x_ref[...], u_ref[...], nt, preferred_element_type=F32))
    tiles = []
    for r in range(rows):
        w = None
        for h in range(heads):
            term = jnp.where(sb_ref[h] >= ta_ref[r, h:h + 1, :], eb_ref[h], 0.0) * ca_ref[r, h:h + 1, :]
            w = term if w is None else w + term
        tiles.append(w)
    gate = jnp.concatenate(tiles, axis=0) if rows > 1 else tiles[0]
    coef = (gate.T * act).astype(BF16)
    o_ref[...] += jnp.dot(coef, v_ref[...], preferred_element_type=F32)


def _peer_dense(xn, u, v, ta, ca, sb, eb, *, tn, te):
    n, d = xn.shape
    e = u.shape[0]
    heads = sb.shape[0]
    rows = te // PEER_NKEYS
    kern = functools.partial(_peer_dense_kernel, heads=heads, rows=rows)
    once = pl.Buffered(1)
    row_blk = pl.BlockSpec((rows, heads, tn), lambda i, j: (j, 0, i))
    big = pl.BlockSpec((heads, PEER_NKEYS, tn), lambda i, j: (0, 0, i), pipeline_mode=once)
    return pl.pallas_call(
        kern, grid=(n // tn, e // te),
        in_specs=[
            pl.BlockSpec((tn, d), lambda i, j: (i, 0), pipeline_mode=once),
            pl.BlockSpec((te, d), lambda i, j: (j, 0)),
            pl.BlockSpec((te, d), lambda i, j: (j, 0)),
            row_blk, row_blk, big, big,
        ],
        out_specs=pl.BlockSpec((tn, d), lambda i, j: (i, 0)),
        out_shape=jax.ShapeDtypeStruct((n, d), F32),
        compiler_params=_cparams(("parallel", "arbitrary")), name="peer_dense")(xn, u, v, ta, ca, sb, eb)


def _pad_rows(x, n):
    if x.shape[0] == n:
        return x
    return jnp.pad(x, ((0, n - x.shape[0]),) + ((0, 0),) * (x.ndim - 1))


def _index_keys(ki, s_pad):
    kb = ki.astype(BF16)
    extra = s_pad - ki.shape[1]
    lo = jnp.pad(kb, ((0, 0), (0, extra), (0, LANES - IDX_DIM)))
    hi = jnp.pad(kb, ((0, 0), (0, extra), (LANES - IDX_DIM, 0)))
    return lo, hi


def kernel(x_prompt, x_sample, mem_prompt, cache_k, cache_v, cache_kidx, state_conv, state_lru, cache_mem_k, cache_mem_v, norm_mix_g, w_in, conv_w, conv_b, lru_wa, lru_ba, lru_wi, lru_bi, lru_lambda, w_out, norm_mem_g, norm_memkv_g, mem_wq, mem_wk, mem_wv, mem_wo, norm_ffn_g, peer_wq, peer_subkeys, peer_u, peer_v, norm_final_g):
    assert w_in.shape[0] == 1, "single-layer step"
    assert IDX_DIM * 2 == LANES and PEER_HALF == LANES and PEER_NKEYS == LANES and HEAD_DIM == LANES
    bp, tp, d = x_prompt.shape
    bs, ts, _ = x_sample.shape
    past = cache_k.shape[2]
    n_p, n_s = bp * tp, bs * ts
    n_tok = n_p + n_s
    n_pad = -(-n_tok // TOKEN_TILE) * TOKEN_TILE
    kvw = N_KV_HEADS * HEAD_DIM
    iqw = IDX_HEADS * IDX_DIM
    n_mem = mem_prompt.shape[1]
    mem_w = mem_wq.shape[2]

    x0 = _pad_rows(jnp.concatenate([x_prompt.reshape(n_p, d), x_sample.reshape(n_s, d)], axis=0), n_pad)
    pos = _pad_rows(jnp.concatenate([jnp.tile(jnp.arange(tp, dtype=jnp.int32), bp),
                                     jnp.tile(past + jnp.arange(ts, dtype=jnp.int32), bs)]), n_pad)

    w = w_in[0]
    o_k, o_v, o_qi, o_ki = d, d + kvw, d + 2 * kvw, d + 2 * kvw + iqw
    o_wi = o_ki + IDX_DIM
    o_xl = o_wi + IDX_HEADS
    assert o_xl + 4 * d == w.shape[1]
    y_xl, y_gl, y_ga, y_gb = d, 2 * d, 3 * d, 4 * d
    y_k, y_v, y_qi = 5 * d, 5 * d + kvw, 5 * d + 2 * kvw
    y_kw = y_qi + iqw
    y_width = -(-(y_kw + LANES) // (2 * LANES)) * (2 * LANES)
    w_all = jnp.concatenate([
        w[:, :d].astype(BF16), w[:, o_xl:o_xl + 4 * d].astype(BF16), w[:, o_k:o_ki].astype(BF16),
        jnp.pad(w[:, o_ki:o_xl].astype(BF16), ((0, 0), (0, y_width - y_kw - IDX_DIM - IDX_HEADS)))], axis=1)

    xn = _rmsnorm(x0, norm_mix_g[0], BF16)
    y = _matmul(xn, w_all, name="in_proj")

    tabs1 = _rope_tables(pos, HEAD_DIM)
    tabs2 = _rope_tables(pos, IDX_DIM)
    q_bf, k_f, k_bf, v_bf, qi_bf, ki_f, wi = _prep(y, tabs1, tabs2, d, kvw, iqw, y_k, y_v, y_qi, y_kw)
    v_f = y[:n_tok, y_v:y_v + kvw]

    n_sel_p = min(TOPK_MAX, tp // 4)
    tk_p = _tile(tp, 512, LANES)
    klo, khi = _index_keys(ki_f[:n_p].reshape(bp, tp, IDX_DIM), tp)
    bias_p = _idx_mask(qi_bf, wi, 0, tp, klo, khi, tq=_tile(tp, 128, 16), tk=tk_p, q_off=0, s_valid=tp,
                       n_sel=n_sel_p)
    att = _attention(q_bf, 0, tp, k_bf, v_bf, tp, bias_p, n_pad,
                     tq=_tile(tp, 256, 16), tk=tk_p, q_off=0, s_valid=tp)

    s_s = past + ts
    s_pad = -(-s_s // LANES) * LANES
    n_sel_s = min(TOPK_MAX, s_s // 4)
    ki_s = jnp.concatenate([cache_kidx[0], ki_f[n_p:n_tok].reshape(bs, ts, IDX_DIM)], axis=1)
    klo_s, khi_s = _index_keys(ki_s, s_pad)
    bias_s = _idx_mask(qi_bf, wi, n_p, ts, klo_s, khi_s, tq=ts, tk=s_pad, q_off=past, s_valid=s_s, n_sel=n_sel_s)

    def with_cache(cache, new):
        full = jnp.concatenate([cache.reshape(bs, past, kvw).astype(BF16), new.reshape(bs, ts, kvw)], axis=1)
        return jnp.pad(full, ((0, 0), (0, s_pad - s_s), (0, 0))).reshape(bs * s_pad, kvw)

    att = _attention(q_bf, n_p, ts, with_cache(cache_k[0], k_bf[n_p:n_tok]), with_cache(cache_v[0], v_bf[n_p:n_tok]),
                     s_pad, bias_s, n_pad, prev=att, tq=ts, tk=s_pad, q_off=past, s_valid=s_s)

    wa_bf, wi_bf = lru_wa[0].astype(BF16), lru_wi[0].astype(BF16)
    lru_args = (conv_w[0], conv_b[0], wa_bf, lru_ba[0], wi_bf, lru_bi[0], lru_lambda[0])
    state_pad = SUBLANES - (CONV_W - 1)
    lru, cn_p, hn_p = _lru(y, 0, bp, tp, y_xl, y_gl, y_gb, jnp.zeros((bp, SUBLANES, d), F32),
                           jnp.zeros((bp, 1, d), F32), *lru_args, n_pad, tt=_tile(tp, 128, SUBLANES))
    conv0_s = jnp.pad(state_conv[0], ((0, 0), (state_pad, 0), (0, 0)))
    lru, cn_s, hn_s = _lru(y, n_p, bs, ts, y_xl, y_gl, y_gb, conv0_s, state_lru[0].reshape(bs, 1, d), *lru_args,
                           n_pad, prev=lru, tt=ts)
    if n_pad > n_tok:
        att = att.at[n_tok:].set(0.0)
        lru = lru.at[n_tok:].set(0.0)

    merged = _merge(y, y_ga, att, lru)
    x1 = _matmul(merged, w_out[0].astype(BF16), res=x0, name="out_proj")

    mem_n = _rmsnorm(mem_prompt.reshape(bp * n_mem, d), norm_memkv_g[0], BF16)
    w_kv = jnp.concatenate([mem_wk[0], mem_wv[0]], axis=1).astype(BF16)
    mkv = _matmul(mem_n, w_kv, name="mem_kv")
    mk_p = mkv[:, :mem_w].reshape(bp, n_mem, mem_w)
    mv_p = mkv[:, mem_w:].reshape(bp, n_mem, mem_w)
    xn2 = _rmsnorm(x1, norm_mem_g[0], BF16)
    qm = _matmul(xn2, mem_wq[0].astype(BF16), name="mem_q")
    om = _mem_attention(qm, 0, tp, mk_p.astype(BF16), mv_p.astype(BF16), tm=_tile(tp, 512, 16))
    om = _mem_attention(qm, n_p, ts, cache_mem_k[0].reshape(bs, n_mem, mem_w).astype(BF16),
                        cache_mem_v[0].reshape(bs, n_mem, mem_w).astype(BF16), prev=om, tm=ts)
    if n_pad > n_tok:
        om = om.at[n_tok:].set(0.0)
    x2 = _matmul(om, mem_wo[0].astype(BF16), res=x1, name="mem_out")

    xn3 = _rmsnorm(x2, norm_ffn_g[0], BF16)
    qp = _matmul(xn3, peer_wq[0].astype(BF16), name="peer_q")
    sub = peer_subkeys[0].reshape(PEER_HEADS * 2, PEER_NKEYS, PEER_HALF).astype(BF16)
    ta, sb, ca, eb = _peer_route(qp, sub, tn=LANES)
    peer = _peer_dense(xn3, peer_u[0].astype(BF16), peer_v[0].astype(BF16),
                       jnp.transpose(ta, (1, 0, 2)), jnp.transpose(ca, (1, 0, 2)), sb, eb,
                       tn=_tile(n_pad, 640, LANES), te=512)

    y_p, y_s = _final_norm(x2, peer, norm_final_g, n_p, n_s)

    kv_shape_p = (1, bp, tp, N_KV_HEADS, HEAD_DIM)
    kv_shape_s = (1, bs, ts, N_KV_HEADS, HEAD_DIM)
    tail = slice(SUBLANES - (CONV_W - 1), SUBLANES)
    return (
        y_p.reshape(bp, tp, d), y_s.reshape(bs, ts, d),
        k_f[:n_p].reshape(kv_shape_p), v_f[:n_p].reshape(kv_shape_p), ki_f[:n_p].reshape(1, bp, tp, IDX_DIM),
        cn_p[:, tail][None], hn_p.reshape(1, bp, d),
        mk_p.reshape(1, bp, n_mem, MEM_HEADS, mem_w // MEM_HEADS),
        mv_p.reshape(1, bp, n_mem, MEM_HEADS, mem_w // MEM_HEADS),
        k_f[n_p:n_tok].reshape(kv_shape_s), v_f[n_p:].reshape(kv_shape_s),
        ki_f[n_p:n_tok].reshape(1, bs, ts, IDX_DIM),
        cn_s[:, tail][None], hn_s.reshape(1, bs, d),
    )
```

```python
import functools

import jax
import jax.numpy as jnp
from jax import lax
from jax.experimental import pallas as pl
from jax.experimental.pallas import tpu as pltpu

F32 = jnp.float32
BF16 = jnp.bfloat16

CHUNK = 64
HEAD_DIM = 128
N_KV_HEADS = 8
IDX_HEADS = 16
IDX_DIM = 64
TOPK_MAX = 256
ROPE_THETA = 500000.0
LRU_BLOCKS = 16
CONV_W = 4
LRU_C = 8.0
MEM_HEADS = 4
PEER_HEADS = 8
PEER_NKEYS = 128
PEER_HALF = 128
PEER_TOPK = 16
EPS = 1e-6

LANES = 128
SUBLANES = 8
TOKEN_TILE = 128
VMEM_LIMIT = 56 * 1024 * 1024
NEG = -1e30
QK_SCALE_LOG2E = (HEAD_DIM ** -0.5) * 1.4426950408889634
INT_MIN = -(2 ** 31)


def _cparams(sem, vmem=VMEM_LIMIT):
    return pltpu.CompilerParams(dimension_semantics=sem, vmem_limit_bytes=vmem)


def _tile(n, pref, mult):
    if n <= pref:
        return n
    t = (pref // mult) * mult
    while t > mult and n % t:
        t -= mult
    assert n % t == 0, (n, pref, mult)
    return t


def _rmsnorm_kernel(x_ref, g_ref, o_ref):
    x = x_ref[...]
    r = lax.rsqrt(jnp.mean(x * x, axis=-1, keepdims=True) + EPS)
    o_ref[...] = ((x * r) * g_ref[...]).astype(o_ref.dtype)


def _final_norm_kernel(x_ref, y_ref, g_ref, op_ref, os_ref, *, first_blocks):
    x = x_ref[...] + y_ref[...]
    r = lax.rsqrt(jnp.mean(x * x, axis=-1, keepdims=True) + EPS)
    val = (x * r) * g_ref[...]

    @pl.when(pl.program_id(0) < first_blocks)
    def _():
        op_ref[...] = val

    @pl.when(pl.program_id(0) >= first_blocks)
    def _():
        os_ref[...] = val


def _final_norm(x, y, g, n_first, n_second):
    d = x.shape[1]
    tm = _tile(n_second, 128, SUBLANES)
    assert n_first % tm == 0
    fb = n_first // tm
    row = pl.BlockSpec((tm, d), lambda i: (i, 0))
    return pl.pallas_call(
        functools.partial(_final_norm_kernel, first_blocks=fb), grid=(fb + n_second // tm,),
        in_specs=[row, row, pl.BlockSpec((1, d), lambda i: (0, 0))],
        out_specs=[pl.BlockSpec((tm, d), lambda i: (jnp.minimum(i, fb - 1), 0)),
                   pl.BlockSpec((tm, d), lambda i: (jnp.maximum(i - fb, 0), 0))],
        out_shape=[jax.ShapeDtypeStruct((n_first, d), F32), jax.ShapeDtypeStruct((n_second, d), F32)],
        compiler_params=_cparams(("arbitrary",)), name="final_norm")(x, y, g.reshape(1, d))


def _rmsnorm(x, g, out_dtype):
    m, d = x.shape
    tm = _tile(m, 640, SUBLANES)
    row = pl.BlockSpec((tm, d), lambda i: (i, 0))
    return pl.pallas_call(
        _rmsnorm_kernel, grid=(m // tm,), in_specs=[row, pl.BlockSpec((1, d), lambda i: (0, 0))], out_specs=row,
        out_shape=jax.ShapeDtypeStruct((m, d), out_dtype),
        compiler_params=_cparams(("parallel",)), name="rmsnorm")(x, g.reshape(1, d))


def _mm_kernel(a_ref, b_ref, o_ref):
    o_ref[...] = jnp.dot(a_ref[...], b_ref[...], preferred_element_type=F32).astype(o_ref.dtype)


def _mm_res_kernel(a_ref, b_ref, r_ref, o_ref):
    o_ref[...] = (r_ref[...] + jnp.dot(a_ref[...], b_ref[...], preferred_element_type=F32)).astype(o_ref.dtype)


def _matmul(a, b, res=None, tm_pref=640, tn_pref=1024, out_dtype=F32, name="matmul"):
    m, k = a.shape
    n = b.shape[1]
    tm = _tile(m, tm_pref, 16)
    tn = _tile(n, tn_pref, 2 * LANES if n % (2 * LANES) == 0 else LANES)
    a_spec = pl.BlockSpec((tm, k), lambda j, i: (i, 0))
    b_spec = pl.BlockSpec((k, tn), lambda j, i: (0, j))
    o_spec = pl.BlockSpec((tm, tn), lambda j, i: (i, j))
    if res is None:
        kern, specs, args = _mm_kernel, [a_spec, b_spec], (a, b)
    else:
        kern, specs, args = _mm_res_kernel, [a_spec, b_spec, o_spec], (a, b, res)
    return pl.pallas_call(
        kern, grid=(n // tn, m // tm), in_specs=specs, out_specs=o_spec,
        out_shape=jax.ShapeDtypeStruct((m, n), out_dtype),
        compiler_params=_cparams(("parallel", "parallel")), name=name)(*args)


def _rope_tables(pos, head_dim):
    rot = head_dim // 4
    half = rot // 2
    inv = ROPE_THETA ** (-jnp.arange(half, dtype=F32) / half)
    ang = pos.astype(F32)[:, None] * inv[None, :]
    cos, sin = jnp.cos(ang), jnp.sin(ang)
    n = pos.shape[0]
    ones = jnp.ones((n, head_dim - rot), F32)
    zr = jnp.zeros((n, head_dim - rot), F32)
    zh = jnp.zeros((n, half), F32)
    c = jnp.concatenate([cos, cos, ones], axis=1)
    sa = jnp.concatenate([-sin, zh, zr], axis=1)
    sb = jnp.concatenate([zh, sin, zr], axis=1)
    reps = LANES // head_dim
    return tuple(jnp.tile(t, (1, reps)) for t in (c, sa, sb))


def _rope(x, c, sa, sb, half):
    return x * c + pltpu.roll(x, LANES - half, 1) * sa + pltpu.roll(x, half, 1) * sb


def _prep_kernel(q_ref, k_ref, v_ref, qi_ref, kw_ref, c1_ref, sa1_ref, sb1_ref, c2_ref, sa2_ref, sb2_ref,
                 qo_ref, kf_ref, kb_ref, vb_ref, qio_ref, kif_ref, wi_ref):
    c1, sa1, sb1 = c1_ref[...], sa1_ref[...], sb1_ref[...]
    c2, sa2, sb2 = c2_ref[...], sa2_ref[...], sb2_ref[...]
    h1 = HEAD_DIM // 8
    h2 = IDX_DIM // 8
    for h in range(q_ref.shape[1] // LANES):
        sl = slice(h * LANES, (h + 1) * LANES)
        qo_ref[:, sl] = (_rope(q_ref[:, sl], c1, sa1, sb1, h1) * QK_SCALE_LOG2E).astype(BF16)
    for h in range(k_ref.shape[1] // LANES):
        sl = slice(h * LANES, (h + 1) * LANES)
        kr = _rope(k_ref[:, sl], c1, sa1, sb1, h1)
        kf_ref[:, sl] = kr
        kb_ref[:, sl] = kr.astype(BF16)
    vb_ref[...] = v_ref[...].astype(BF16)
    for h in range(qi_ref.shape[1] // LANES):
        sl = slice(h * LANES, (h + 1) * LANES)
        qio_ref[:, sl] = _rope(qi_ref[:, sl], c2, sa2, sb2, h2).astype(BF16)
    t = kw_ref[...]
    kif_ref[...] = _rope(t, c2, sa2, sb2, h2)[:, :IDX_DIM]
    wi_ref[...] = t[:, IDX_DIM:IDX_DIM + IDX_HEADS] * (IDX_HEADS ** -0.5)


def _prep(y, tabs1, tabs2, d, kvw, iqw, off_k, off_v, off_qi, off_kw):
    n = y.shape[0]
    tm = _tile(n, 128, 16)
    tab = pl.BlockSpec((tm, LANES), lambda i: (i, 0))
    in_specs = [
        pl.BlockSpec((tm, d), lambda i: (i, 0)),
        pl.BlockSpec((tm, kvw), lambda i: (i, off_k // kvw)),
        pl.BlockSpec((tm, kvw), lambda i: (i, off_v // kvw)),
        pl.BlockSpec((tm, iqw), lambda i: (i, off_qi // iqw)),
        pl.BlockSpec((tm, LANES), lambda i: (i, off_kw // LANES)),
        tab, tab, tab, tab, tab, tab,
    ]
    outs = [
        (d, BF16), (kvw, F32), (kvw, BF16), (kvw, BF16), (iqw, BF16), (IDX_DIM, F32), (IDX_HEADS, F32),
    ]
    return pl.pallas_call(
        _prep_kernel, grid=(n // tm,), in_specs=in_specs,
        out_specs=[pl.BlockSpec((tm, w), lambda i: (i, 0)) for w, _ in outs],
        out_shape=[jax.ShapeDtypeStruct((n, w), dt) for w, dt in outs],
        compiler_params=_cparams(("parallel",)), name="prep")(y, y, y, y, y, *tabs1, *tabs2)


def _num_k_tiles(i, tq, tk, q_off, s_valid):
    last_chunk_end = ((q_off + i * tq + tq - 1) // CHUNK + 1) * CHUNK
    return (jnp.minimum(last_chunk_end, s_valid) + tk - 1) // tk


def _idx_kernel(qi_ref, wi_ref, klo_ref, khi_ref, o_ref, key_ref, *, tq, tk, nk, q_off, s_valid, n_sel):
    i = pl.program_id(1)
    q0 = q_off + i * tq
    nkt = _num_k_tiles(i, tq, tk, q_off, s_valid)
    wi = wi_ref[...] * (IDX_DIM ** -0.5)
    wcols = [wi[:, h:h + 1] for h in range(IDX_HEADS)]
    q_chunk = (q0 + lax.broadcasted_iota(jnp.int32, (tq, tk), 0)) // CHUNK
    nt = (((1,), (1,)), ((), ()))

    def score_tile(j, carry):
        k0 = pl.multiple_of(j * tk, tk)
        klo = klo_ref[0, pl.ds(k0, tk), :]
        khi = khi_ref[0, pl.ds(k0, tk), :]
        acc = jnp.zeros((tq, tk), F32)
        for p in range(IDX_HEADS // 2):
            qp = qi_ref[:, p * LANES:(p + 1) * LANES]
            d0 = lax.dot_general(qp, klo, nt, preferred_element_type=F32)
            d1 = lax.dot_general(qp, khi, nt, preferred_element_type=F32)
            acc = acc + jnp.maximum(d0, 0.0) * wcols[2 * p]
            acc = acc + jnp.maximum(d1, 0.0) * wcols[2 * p + 1]
        k_pos = k0 + lax.broadcasted_iota(jnp.int32, (tq, tk), 1)
        vis = jnp.logical_and(k_pos // CHUNK <= q_chunk, k_pos < s_valid)
        bits = pltpu.bitcast(acc, jnp.int32)
        key = bits ^ ((bits >> 31) & 0x7FFFFFFF)
        key_ref[j] = jnp.where(vis, key, INT_MIN)
        return carry

    lax.fori_loop(0, nkt, score_tile, 0)

    def count(pred):
        def count_tile(j, acc):
            m = jnp.where(pred(j, key_ref[j]), 1.0, 0.0)
            part = m[:, 0:LANES]
            for c in range(1, tk // LANES):
                part = part + m[:, c * LANES:(c + 1) * LANES]
            return acc + part

        acc = lax.fori_loop(0, nkt, count_tile, jnp.zeros((tq, LANES), F32))
        return jnp.sum(acc, axis=1, keepdims=True)

    def bit_step(it, carry):
        r, cnt_r = carry
        cand = r | jnp.left_shift(jnp.int32(1), 31 - it)
        cs = cand ^ INT_MIN
        cnt = count(lambda j, key: key >= cs)
        take = cnt >= n_sel
        return jnp.where(take, cand, r), jnp.where(take, cnt, cnt_r)

    r, cnt_ge = lax.fori_loop(0, 32, bit_step,
                              (jnp.zeros((tq, 1), jnp.int32), jnp.full((tq, 1), float(n_sel), F32)))
    thr = jnp.maximum(r ^ INT_MIN, INT_MIN + 1)
    has_ties = jnp.max(cnt_ge) > float(n_sel)

    @pl.when(jnp.logical_not(has_ties))
    def _write_plain():
        def write_tile(j, carry):
            o_ref[0, j] = jnp.where(key_ref[j] >= thr, 0.0, NEG).astype(BF16)
            return carry

        lax.fori_loop(0, nkt, write_tile, 0)

    @pl.when(has_ties)
    def _write_tie_broken():
        need = float(n_sel) - count(lambda j, key: key > thr)
        lane = lax.broadcasted_iota(jnp.int32, (tq, tk), 1)

        def pos_step(it, p):
            cand = p | jnp.left_shift(jnp.int32(1), pos_bits - 1 - it)
            cnt = count(lambda j, key: jnp.logical_and(key == thr, lane + j * tk < cand))
            return jnp.where(cnt < need, cand, p)

        pos_bits = max(1, (nk * tk - 1).bit_length())
        p = lax.fori_loop(0, pos_bits, pos_step, jnp.zeros((tq, 1), jnp.int32))

        def write_tile(j, carry):
            key = key_ref[j]
            sel = jnp.logical_or(key > thr, jnp.logical_and(key == thr, lane + j * tk <= p))
            o_ref[0, j] = jnp.where(sel, 0.0, NEG).astype(BF16)
            return carry

        lax.fori_loop(0, nkt, write_tile, 0)

    def fill_tile(j, carry):
        o_ref[0, j] = jnp.full((tq, tk), NEG, BF16)
        return carry

    lax.fori_loop(nkt, nk, fill_tile, 0)


def _idx_mask(qi, wi, row_off, t, klo, khi, *, tq, tk, q_off, s_valid, n_sel):
    b, s_pad, _ = klo.shape
    iqw = qi.shape[1]
    nk = s_pad // tk
    assert row_off % tq == 0 and t % tq == 0
    qrow = lambda bb, i: (row_off // tq + bb * (t // tq) + i, 0)
    kern = functools.partial(_idx_kernel, tq=tq, tk=tk, nk=nk, q_off=q_off, s_valid=s_valid, n_sel=n_sel)
    return pl.pallas_call(
        kern, grid=(b, t // tq),
        in_specs=[
            pl.BlockSpec((tq, iqw), qrow),
            pl.BlockSpec((tq, IDX_HEADS), qrow),
            pl.BlockSpec((1, s_pad, LANES), lambda bb, i: (bb, 0, 0)),
            pl.BlockSpec((1, s_pad, LANES), lambda bb, i: (bb, 0, 0)),
        ],
        out_specs=pl.BlockSpec((1, nk, tq, tk), lambda bb, i: (bb, 0, i, 0)),
        out_shape=jax.ShapeDtypeStruct((b, nk, t, tk), BF16),
        scratch_shapes=[pltpu.VMEM((nk, tq, tk), jnp.int32)],
        compiler_params=_cparams(("parallel", "parallel")), name="idx_mask")(qi, wi, klo, khi)


def _attn_kernel(q_ref, k_ref, v_ref, b_ref, o_ref, qs_ref, m_ref, acc_ref,
                 *, tq, tk, nk, q_off, s_valid, nkv, grp):
    i = pl.program_id(1)
    j = pl.program_id(2)
    nkt = _num_k_tiles(i, tq, tk, q_off, s_valid)
    nt = (((1,), (1,)), ((), ()))

    @pl.when(j == 0)
    def _init():
        eye = (lax.broadcasted_iota(jnp.int32, (tq, LANES), 0) ==
               lax.broadcasted_iota(jnp.int32, (tq, LANES), 1)).astype(F32).astype(BF16)
        for g in range(nkv):
            for hh in range(grp):
                h = g * grp + hh
                qs_ref[g, hh * tq:(hh + 1) * tq, :HEAD_DIM] = q_ref[:, h * HEAD_DIM:(h + 1) * HEAD_DIM]
                qs_ref[g, hh * tq:(hh + 1) * tq, HEAD_DIM:] = eye
        m_ref[...] = jnp.full(m_ref.shape, NEG, F32)
        acc_ref[...] = jnp.zeros(acc_ref.shape, F32)

    @pl.when(j < nkt)
    def _compute():
        mask = jnp.concatenate([b_ref[0, c] for c in range(b_ref.shape[1])], axis=1).astype(F32)
        if tq < LANES:
            mask = jnp.concatenate([mask, jnp.zeros((LANES - tq, tk), F32)], axis=0)
        mask_t = mask.T.astype(BF16)
        nc = tk // LANES
        ones = jnp.ones((tk, LANES), BF16)
        chunks = lambda a: [a[:, c * LANES:(c + 1) * LANES] for c in range(nc)]
        for g in range(nkv):
            kg = jnp.concatenate([k_ref[:, g * HEAD_DIM:(g + 1) * HEAD_DIM], mask_t], axis=1)
            vg = jnp.concatenate([v_ref[:, g * HEAD_DIM:(g + 1) * HEAD_DIM], ones], axis=1)
            s = lax.dot_general(qs_ref[g], kg, nt, preferred_element_type=F32)
            m_prev = m_ref[g]
            m_new = jnp.maximum(m_prev, jnp.max(functools.reduce(jnp.maximum, chunks(s)), axis=1, keepdims=True))
            p = jnp.exp2(s - jnp.tile(m_new, (1, nc)))
            alpha = jnp.exp2(m_prev - m_new)
            acc_ref[g] = jnp.tile(alpha, (1, 2)) * acc_ref[g] + jnp.dot(p.astype(BF16), vg, preferred_element_type=F32)
            m_ref[g] = m_new

    @pl.when(j == nk - 1)
    def _finish():
        for g in range(nkv):
            for hh in range(grp):
                h = g * grp + hh
                rows = slice(hh * tq, (hh + 1) * tq)
                o_ref[:, h * HEAD_DIM:(h + 1) * HEAD_DIM] = acc_ref[g, rows, :HEAD_DIM] / acc_ref[g, rows, HEAD_DIM:]


def _attn_kernel_aliased(q_ref, k_ref, v_ref, b_ref, prev_ref, o_ref, *scratch, **kw):
    del prev_ref
    _attn_kernel(q_ref, k_ref, v_ref, b_ref, o_ref, *scratch, **kw)


def _attention(q, row_off, t, k, v, s_pad, bias, n_out, prev=None, *, tq, tk, q_off, s_valid):
    b = bias.shape[0]
    aw, kvw = q.shape[1], k.shape[1]
    nk = s_pad // tk
    nkv = kvw // HEAD_DIM
    grp = aw // kvw
    tkb = bias.shape[3]
    assert row_off % tq == 0 and t % tq == 0 and tq <= LANES and tk % tkb == 0
    kw = dict(tq=tq, tk=tk, nk=nk, q_off=q_off, s_valid=s_valid, nkv=nkv, grp=grp)

    def kj(i, j):
        return jnp.minimum(j, _num_k_tiles(i, tq, tk, q_off, s_valid) - 1)

    qrow = lambda bb, i, j: (row_off // tq + bb * (t // tq) + i, 0)
    krow = lambda bb, i, j: (bb * nk + kj(i, j), 0)
    in_specs = [
        pl.BlockSpec((tq, aw), qrow),
        pl.BlockSpec((tk, kvw), krow),
        pl.BlockSpec((tk, kvw), krow),
        pl.BlockSpec((1, tk // tkb, tq, tkb), lambda bb, i, j: (bb, kj(i, j), i, 0)),
    ]
    args = (q, k, v, bias)
    if prev is None:
        kern, aliases = functools.partial(_attn_kernel, **kw), {}
    else:
        kern, aliases = functools.partial(_attn_kernel_aliased, **kw), {4: 0}
        in_specs.append(pl.BlockSpec(memory_space=pl.ANY))
        args += (prev,)
    return pl.pallas_call(
        kern, grid=(b, t // tq, nk), in_specs=in_specs,
        out_specs=pl.BlockSpec((tq, aw), qrow),
        out_shape=jax.ShapeDtypeStruct((n_out, aw), F32),
        scratch_shapes=[
            pltpu.VMEM((nkv, grp * tq, HEAD_DIM + LANES), BF16),
            pltpu.VMEM((nkv, grp * tq, LANES), F32),
            pltpu.VMEM((nkv, grp * tq, 2 * HEAD_DIM), F32),
        ],
        input_output_aliases=aliases,
        compiler_params=_cparams(("parallel", "parallel", "arbitrary")), name="attention")(*args)


def _sigmoid(x):
    return 0.5 * jnp.tanh(0.5 * x) + 0.5


def _gelu(x):
    return 0.5 * x * (1.0 + lax.erf(x * (0.5 ** 0.5)))


def _lru_kernel(x_ref, gl_ref, gb_ref, ga_ref, att_ref, c0_ref, h0_ref, cw_ref, cb_ref, wa_ref, ba_ref, wi_ref, bi_ref,
                lam_ref, o_ref, cn_ref, hn_ref, xp_ref, a_ref, b_ref, hs_ref, h_ref, *, tt, nblk, bw):
    t = pl.program_id(1)
    pad = SUBLANES

    @pl.when(t == 0)
    def _init():
        xp_ref[0:pad, :] = c0_ref[0]
        h_ref[...] = h0_ref[0]

    xp_ref[pad:pad + tt, :] = x_ref[...]
    cw = cw_ref[...]
    base = pad - (CONV_W - 1)
    xc = cb_ref[...] + xp_ref[base:base + tt, :] * cw[0:1]
    for jj in range(1, CONV_W):
        xc = xc + xp_ref[base + jj:base + jj + tt, :] * cw[jj:jj + 1]
    tail = xp_ref[tt:tt + pad, :]
    cn_ref[0] = tail
    xp_ref[0:pad, :] = tail

    xcb = xc.astype(BF16)
    ra, ri = [], []
    for n in range(nblk):
        xs = xcb[:, n * bw:(n + 1) * bw]
        ra.append(jnp.dot(xs, wa_ref[n], preferred_element_type=F32))
        ri.append(jnp.dot(xs, wi_ref[n], preferred_element_type=F32))
    r = _sigmoid(jnp.concatenate(ra, axis=1) + ba_ref[...])
    ig = _sigmoid(jnp.concatenate(ri, axis=1) + bi_ref[...])
    z = -lam_ref[...]
    softplus = jnp.maximum(z, 0.0) + jnp.log1p(jnp.exp(-jnp.abs(z)))
    log_a = (-LRU_C * r) * softplus
    a = jnp.exp(log_a)
    neg_expm1 = -jnp.tanh(log_a) * (a * a + 1.0)
    a_ref[...] = a
    b_ref[...] = jnp.sqrt(neg_expm1) * (ig * xc)

    def step(s, h):
        h = a_ref[pl.ds(s, 1), :] * h + b_ref[pl.ds(s, 1), :]
        hs_ref[pl.ds(s, 1), :] = h
        return h

    h = lax.fori_loop(0, tt, step, h_ref[...], unroll=8)
    h_ref[...] = h
    hn_ref[0] = h
    lru = hs_ref[...] * _gelu(gl_ref[...])
    o_ref[...] = (_sigmoid(ga_ref[...]) * att_ref[...] + _sigmoid(gb_ref[...]) * lru).astype(o_ref.dtype)


def _lru_kernel_aliased(*refs, **kw):
    _lru_kernel(*refs[:14], *refs[15:], **kw)


def _lru(y, row_off, b, t, off_x, off_gl, off_ga, off_gb, att, conv0, h0, conv_w, conv_b, wa, ba, wi, bi, lam,
         prev=None, *, tt):
    c = conv_w.shape[1]
    nblk, bw = wa.shape[0], wa.shape[1]
    assert row_off % tt == 0 and t % tt == 0 and all(o % c == 0 for o in (off_x, off_gl, off_ga, off_gb))
    rows = lambda bb, i: row_off // tt + bb * (t // tt) + i
    col = lambda off: pl.BlockSpec((tt, c), lambda bb, i: (rows(bb, i), off // c))
    vec = pl.BlockSpec((1, c), lambda bb, i: (0, 0))
    wsp = pl.BlockSpec((nblk, bw, bw), lambda bb, i: (0, 0, 0))
    in_specs = [
        col(off_x), col(off_gl), col(off_gb), col(off_ga), col(0),
        pl.BlockSpec((1, SUBLANES, c), lambda bb, i: (bb, 0, 0)),
        pl.BlockSpec((1, 1, c), lambda bb, i: (bb, 0, 0)),
        pl.BlockSpec((CONV_W, c), lambda bb, i: (0, 0)),
        vec, wsp, vec, wsp, vec, vec,
    ]
    args = (y, y, y, y, att, conv0, h0, conv_w, conv_b.reshape(1, c), wa, ba.reshape(1, c), wi, bi.reshape(1, c),
            lam.reshape(1, c))
    kw = dict(tt=tt, nblk=nblk, bw=bw)
    if prev is None:
        kern, aliases = functools.partial(_lru_kernel, **kw), {}
    else:
        kern, aliases = functools.partial(_lru_kernel_aliased, **kw), {14: 0}
        in_specs.append(pl.BlockSpec(memory_space=pl.ANY))
        args += (prev,)
    return pl.pallas_call(
        kern, grid=(b, t // tt), in_specs=in_specs,
        out_specs=[
            pl.BlockSpec((tt, c), lambda bb, i: (rows(bb, i), 0)),
            pl.BlockSpec((1, SUBLANES, c), lambda bb, i: (bb, 0, 0)),
            pl.BlockSpec((1, 1, c), lambda bb, i: (bb, 0, 0)),
        ],
        out_shape=[
            jax.ShapeDtypeStruct(att.shape, BF16),
            jax.ShapeDtypeStruct((b, SUBLANES, c), F32),
            jax.ShapeDtypeStruct((b, 1, c), F32),
        ],
        scratch_shapes=[
            pltpu.VMEM((tt + SUBLANES, c), F32),
            pltpu.VMEM((tt, c), F32),
            pltpu.VMEM((tt, c), F32),
            pltpu.VMEM((tt, c), F32),
            pltpu.VMEM((1, c), F32),
        ],
        input_output_aliases=aliases,
        compiler_params=_cparams(("arbitrary", "arbitrary")), name="rg_lru")(*args)


def _mem_attn_kernel(q_ref, mk_ref, mv_ref, o_ref, *, heads, hd):
    scale = hd ** -0.5
    nt = (((1,), (1,)), ((), ()))
    for h in range(heads):
        sl = slice(h * hd, (h + 1) * hd)
        s = lax.dot_general(q_ref[:, sl].astype(BF16), mk_ref[0, :, sl], nt, preferred_element_type=F32) * scale
        s = s - jnp.max(s, axis=1, keepdims=True)
        e = jnp.exp(s)
        p = e / jnp.sum(e, axis=1, keepdims=True)
        o_ref[:, sl] = jnp.dot(p.astype(BF16), mv_ref[0, :, sl], preferred_element_type=F32).astype(o_ref.dtype)


def _mem_attn_kernel_aliased(q_ref, mk_ref, mv_ref, prev_ref, o_ref, **kw):
    del prev_ref
    _mem_attn_kernel(q_ref, mk_ref, mv_ref, o_ref, **kw)


def _mem_attention(q, row_off, t, mk, mv, prev=None, *, tm):
    b, nm, w = mk.shape
    assert row_off % tm == 0 and t % tm == 0
    kw = dict(heads=MEM_HEADS, hd=w // MEM_HEADS)
    qrow = lambda bb, i: (row_off // tm + bb * (t // tm) + i, 0)
    in_specs = [
        pl.BlockSpec((tm, w), qrow),
        pl.BlockSpec((1, nm, w), lambda bb, i: (bb, 0, 0)),
        pl.BlockSpec((1, nm, w), lambda bb, i: (bb, 0, 0)),
    ]
    args = (q, mk, mv)
    if prev is None:
        kern, aliases = functools.partial(_mem_attn_kernel, **kw), {}
    else:
        kern, aliases = functools.partial(_mem_attn_kernel_aliased, **kw), {3: 0}
        in_specs.append(pl.BlockSpec(memory_space=pl.ANY))
        args += (prev,)
    return pl.pallas_call(
        kern, grid=(b, t // tm), in_specs=in_specs,
        out_specs=pl.BlockSpec((tm, w), qrow),
        out_shape=jax.ShapeDtypeStruct((q.shape[0], w), BF16),
        input_output_aliases=aliases,
        compiler_params=_cparams(("parallel", "parallel")), name="mem_attention")(*args)


def _top_values(x, k):
    vals = []
    for _ in range(k):
        m = jnp.max(x, axis=0, keepdims=True)
        vals.append(m)
        x = jnp.where(x == m, -jnp.inf, x)
    return vals


def _peer_route_kernel(q_ref, sub_ref, ta_ref, sb_ref, ca_ref, eb_ref, *, heads):
    nt = (((1,), (1,)), ((), ()))
    k = PEER_TOPK
    for h in range(heads):
        st = []
        sv = []
        for c in range(2):
            col = (h * 2 + c) * PEER_HALF
            qh = q_ref[:, col:col + PEER_HALF].astype(BF16)
            s = lax.dot_general(sub_ref[h * 2 + c], qh, nt, preferred_element_type=F32)
            st.append(s)
            sv.append(_top_values(s, k + 1))
        rows = [sv[0][a] + sv[1][b] for a in range(k + 1) for b in range((k + 1) // (a + 1))]
        rows += [jnp.full_like(rows[0], -jnp.inf)] * (-len(rows) % SUBLANES)
        tv = _top_values(jnp.concatenate(rows, axis=0), k + 1)
        z = jnp.zeros_like(tv[0])
        for v in tv[:k]:
            z = z + jnp.exp(v - tv[0])
        ta_ref[h] = 0.5 * (tv[k - 1] + tv[k]) - st[0]
        sb_ref[h] = st[1]
        ca_ref[h] = jnp.exp(st[0] - sv[0][0]) / z
        eb_ref[h] = jnp.exp(st[1] - sv[1][0])


def _peer_route(qp, sub, *, tn):
    n = qp.shape[0]
    heads = PEER_HEADS
    kern = functools.partial(_peer_route_kernel, heads=heads)
    big = pl.BlockSpec((heads, PEER_NKEYS, tn), lambda i: (0, 0, i))
    big_shape = jax.ShapeDtypeStruct((heads, PEER_NKEYS, n), F32)
    return pl.pallas_call(
        kern, grid=(n // tn,),
        in_specs=[
            pl.BlockSpec((tn, qp.shape[1]), lambda i: (i, 0)),
            pl.BlockSpec(sub.shape, lambda i: (0, 0, 0)),
        ],
        out_specs=[big, big, big, big],
        out_shape=[big_shape, big_shape, big_shape, big_shape],
        compiler_params=_cparams(("parallel",)), name="peer_route")(qp, sub)


def _peer_dense_kernel(x_ref, u_ref, v_ref, ta_ref, ca_ref, sb_ref, eb_ref, o_ref, *, heads, rows):
    j = pl.program_id(1)

    @pl.when(j == 0)
    def _init():
        o_ref[...] = jnp.zeros(o_ref.shape, F32)

    nt = (((1,), (1,)), ((), ()))
    act = _gelu(lax.dot_general(x_ref[...], u_ref[...], nt, preferred_element_type=F32))
    tiles = []
    for r in range(rows):
        w = None
        for h in range(heads):
            term = jnp.where(sb_ref[h] >= ta_ref[r, h:h + 1, :], eb_ref[h], 0.0) * ca_ref[r, h:h + 1, :]
            w = term if w is None else w + term
        tiles.append(w)
    gate = jnp.concatenate(tiles, axis=0) if rows > 1 else tiles[0]
    coef = (gate.T * act).astype(BF16)
    o_ref[...] += jnp.dot(coef, v_ref[...], preferred_element_type=F32)


def _peer_dense(xn, u, v, ta, ca, sb, eb, *, tn, te):
    n, d = xn.shape
    e = u.shape[0]
    heads = sb.shape[0]
    rows = te // PEER_NKEYS
    kern = functools.partial(_peer_dense_kernel, heads=heads, rows=rows)
    once = pl.Buffered(1)
    row_blk = pl.BlockSpec((rows, heads, tn), lambda i, j: (j, 0, i))
    big = pl.BlockSpec((heads, PEER_NKEYS, tn), lambda i, j: (0, 0, i), pipeline_mode=once)
    return pl.pallas_call(
        kern, grid=(n // tn, e // te),
        in_specs=[
            pl.BlockSpec((tn, d), lambda i, j: (i, 0), pipeline_mode=once),
            pl.BlockSpec((te, d), lambda i, j: (j, 0)),
            pl.BlockSpec((te, d), lambda i, j: (j, 0)),
            row_blk, row_blk, big, big,
        ],
        out_specs=pl.BlockSpec((tn, d), lambda i, j: (i, 0)),
        out_shape=jax.ShapeDtypeStruct((n, d), F32),
        compiler_params=_cparams(("parallel", "arbitrary")), name="peer_dense")(xn, u, v, ta, ca, sb, eb)


def _pad_rows(x, n):
    if x.shape[0] == n:
        return x
    return jnp.pad(x, ((0, n - x.shape[0]),) + ((0, 0),) * (x.ndim - 1))


def _index_keys(ki, s_pad):
    kb = ki.astype(BF16)
    extra = s_pad - ki.shape[1]
    lo = jnp.pad(kb, ((0, 0), (0, extra), (0, LANES - IDX_DIM)))
    hi = jnp.pad(kb, ((0, 0), (0, extra), (LANES - IDX_DIM, 0)))
    return lo, hi


def kernel(x_prompt, x_sample, mem_prompt, cache_k, cache_v, cache_kidx, state_conv, state_lru, cache_mem_k, cache_mem_v, norm_mix_g, w_in, conv_w, conv_b, lru_wa, lru_ba, lru_wi, lru_bi, lru_lambda, w_out, norm_mem_g, norm_memkv_g, mem_wq, mem_wk, mem_wv, mem_wo, norm_ffn_g, peer_wq, peer_subkeys, peer_u, peer_v, norm_final_g):
    assert w_in.shape[0] == 1, "single-layer step"
    assert IDX_DIM * 2 == LANES and PEER_HALF == LANES and PEER_NKEYS == LANES and HEAD_DIM == LANES
    bp, tp, d = x_prompt.shape
    bs, ts, _ = x_sample.shape
    past = cache_k.shape[2]
    n_p, n_s = bp * tp, bs * ts
    n_tok = n_p + n_s
    n_pad = -(-n_tok // TOKEN_TILE) * TOKEN_TILE
    kvw = N_KV_HEADS * HEAD_DIM
    iqw = IDX_HEADS * IDX_DIM
    n_mem = mem_prompt.shape[1]
    mem_w = mem_wq.shape[2]

    x0 = _pad_rows(jnp.concatenate([x_prompt.reshape(n_p, d), x_sample.reshape(n_s, d)], axis=0), n_pad)
    pos = _pad_rows(jnp.concatenate([jnp.tile(jnp.arange(tp, dtype=jnp.int32), bp),
                                     jnp.tile(past + jnp.arange(ts, dtype=jnp.int32), bs)]), n_pad)

    w = w_in[0]
    o_k, o_v, o_qi, o_ki = d, d + kvw, d + 2 * kvw, d + 2 * kvw + iqw
    o_wi = o_ki + IDX_DIM
    o_xl = o_wi + IDX_HEADS
    assert o_xl + 4 * d == w.shape[1]
    y_xl, y_gl, y_ga, y_gb = d, 2 * d, 3 * d, 4 * d
    y_k, y_v, y_qi = 5 * d, 5 * d + kvw, 5 * d + 2 * kvw
    y_kw = y_qi + iqw
    y_width = -(-(y_kw + LANES) // (2 * LANES)) * (2 * LANES)
    w_all = jnp.concatenate([
        w[:, :d].astype(BF16), w[:, o_xl:o_xl + 4 * d].astype(BF16), w[:, o_k:o_ki].astype(BF16),
        jnp.pad(w[:, o_ki:o_xl].astype(BF16), ((0, 0), (0, y_width - y_kw - IDX_DIM - IDX_HEADS)))], axis=1)

    xn = _rmsnorm(x0, norm_mix_g[0], BF16)
    y = _matmul(xn, w_all, name="in_proj")

    tabs1 = _rope_tables(pos, HEAD_DIM)
    tabs2 = _rope_tables(pos, IDX_DIM)
    q_bf, k_f, k_bf, v_bf, qi_bf, ki_f, wi = _prep(y, tabs1, tabs2, d, kvw, iqw, y_k, y_v, y_qi, y_kw)
    v_f = y[:n_tok, y_v:y_v + kvw]

    n_sel_p = min(TOPK_MAX, tp // 4)
    tk_p = _tile(tp, 512, LANES)
    klo, khi = _index_keys(ki_f[:n_p].reshape(bp, tp, IDX_DIM), tp)
    bias_p = _idx_mask(qi_bf, wi, 0, tp, klo, khi, tq=_tile(tp, 128, 16), tk=tk_p, q_off=0, s_valid=tp,
                       n_sel=n_sel_p)
    att = _attention(q_bf, 0, tp, k_bf, v_bf, tp, bias_p, n_pad,
                     tq=_tile(tp, 128, 16), tk=_tile(tp, 1024, tk_p), q_off=0, s_valid=tp)

    s_s = past + ts
    s_pad = -(-s_s // LANES) * LANES
    n_sel_s = min(TOPK_MAX, s_s // 4)
    ki_s = jnp.concatenate([cache_kidx[0], ki_f[n_p:n_tok].reshape(bs, ts, IDX_DIM)], axis=1)
    klo_s, khi_s = _index_keys(ki_s, s_pad)
    bias_s = _idx_mask(qi_bf, wi, n_p, ts, klo_s, khi_s, tq=ts, tk=s_pad, q_off=past, s_valid=s_s, n_sel=n_sel_s)

    def with_cache(cache, new):
        full = jnp.concatenate([cache.reshape(bs, past, kvw).astype(BF16), new.reshape(bs, ts, kvw)], axis=1)
        return jnp.pad(full, ((0, 0), (0, s_pad - s_s), (0, 0))).reshape(bs * s_pad, kvw)

    att = _attention(q_bf, n_p, ts, with_cache(cache_k[0], k_bf[n_p:n_tok]), with_cache(cache_v[0], v_bf[n_p:n_tok]),
                     s_pad, bias_s, n_pad, prev=att, tq=ts, tk=s_pad, q_off=past, s_valid=s_s)

    wa_bf, wi_bf = lru_wa[0].astype(BF16), lru_wi[0].astype(BF16)
    lru_args = (conv_w[0], conv_b[0], wa_bf, lru_ba[0], wi_bf, lru_bi[0], lru_lambda[0])
    state_pad = SUBLANES - (CONV_W - 1)
    merged, cn_p, hn_p = _lru(y, 0, bp, tp, y_xl, y_gl, y_ga, y_gb, att, jnp.zeros((bp, SUBLANES, d), F32),
                              jnp.zeros((bp, 1, d), F32), *lru_args, tt=_tile(tp, 128, SUBLANES))
    conv0_s = jnp.pad(state_conv[0], ((0, 0), (state_pad, 0), (0, 0)))
    merged, cn_s, hn_s = _lru(y, n_p, bs, ts, y_xl, y_gl, y_ga, y_gb, att, conv0_s, state_lru[0].reshape(bs, 1, d),
                              *lru_args, prev=merged, tt=ts)
    if n_pad > n_tok:
        merged = merged.at[n_tok:].set(0.0)

    x1 = _matmul(merged, w_out[0].astype(BF16), res=x0, name="out_proj")

    mem_n = _rmsnorm(mem_prompt.reshape(bp * n_mem, d), norm_memkv_g[0], BF16)
    w_kv = jnp.concatenate([mem_wk[0], mem_wv[0]], axis=1).astype(BF16)
    mkv = _matmul(mem_n, w_kv, name="mem_kv")
    mk_p = mkv[:, :mem_w].reshape(bp, n_mem, mem_w)
    mv_p = mkv[:, mem_w:].reshape(bp, n_mem, mem_w)
    xn2 = _rmsnorm(x1, norm_mem_g[0], BF16)
    qm = _matmul(xn2, mem_wq[0].astype(BF16), name="mem_q")
    om = _mem_attention(qm, 0, tp, mk_p.astype(BF16), mv_p.astype(BF16), tm=_tile(tp, 512, 16))
    om = _mem_attention(qm, n_p, ts, cache_mem_k[0].reshape(bs, n_mem, mem_w).astype(BF16),
                        cache_mem_v[0].reshape(bs, n_mem, mem_w).astype(BF16), prev=om, tm=ts)
    if n_pad > n_tok:
        om = om.at[n_tok:].set(0.0)
    x2 = _matmul(om, mem_wo[0].astype(BF16), res=x1, name="mem_out")

    xn3 = _rmsnorm(x2, norm_ffn_g[0], BF16)
    qp = _matmul(xn3, peer_wq[0].astype(BF16), name="peer_q")
    sub = peer_subkeys[0].reshape(PEER_HEADS * 2, PEER_NKEYS, PEER_HALF).astype(BF16)
    ta, sb, ca, eb = _peer_route(qp, sub, tn=LANES)
    peer = _peer_dense(xn3, peer_u[0].astype(BF16), peer_v[0].astype(BF16),
                       jnp.transpose(ta, (1, 0, 2)), jnp.transpose(ca, (1, 0, 2)), sb, eb,
                       tn=_tile(n_pad, 640, LANES), te=512)

    y_p, y_s = _final_norm(x2, peer, norm_final_g, n_p, n_s)

    kv_shape_p = (1, bp, tp, N_KV_HEADS, HEAD_DIM)
    kv_shape_s = (1, bs, ts, N_KV_HEADS, HEAD_DIM)
    tail = slice(SUBLANES - (CONV_W - 1), SUBLANES)
    return (
        y_p.reshape(bp, tp, d), y_s.reshape(bs, ts, d),
        k_f[:n_p].reshape(kv_shape_p), v_f[:n_p].reshape(kv_shape_p), ki_f[:n_p].reshape(1, bp, tp, IDX_DIM),
        cn_p[:, tail][None], hn_p.reshape(1, bp, d),
        mk_p.reshape(1, bp, n_mem, MEM_HEADS, mem_w // MEM_HEADS),
        mv_p.reshape(1, bp, n_mem, MEM_HEADS, mem_w // MEM_HEADS),
        k_f[n_p:n_tok].reshape(kv_shape_s), v_f[n_p:].reshape(kv_shape_s),
        ki_f[n_p:n_tok].reshape(1, bs, ts, IDX_DIM),
        cn_s[:, tail][None], hn_s.reshape(1, bs, d),
    )
```

```python
import functools

import jax
import jax.numpy as jnp
from jax import lax
from jax.experimental import pallas as pl
from jax.experimental.pallas import tpu as pltpu

F32 = jnp.float32
BF16 = jnp.bfloat16

CHUNK = 64
HEAD_DIM = 128
N_KV_HEADS = 8
IDX_HEADS = 16
IDX_DIM = 64
TOPK_MAX = 256
ROPE_THETA = 500000.0
LRU_BLOCKS = 16
CONV_W = 4
LRU_C = 8.0
MEM_HEADS = 4
PEER_HEADS = 8
PEER_NKEYS = 128
PEER_HALF = 128
PEER_TOPK = 16
EPS = 1e-6

LANES = 128
SUBLANES = 8
TOKEN_TILE = 128
VMEM_LIMIT = 56 * 1024 * 1024
NEG = -1e30
QK_SCALE_LOG2E = (HEAD_DIM ** -0.5) * 1.4426950408889634
INT_MIN = -(2 ** 31)


def _cparams(sem, vmem=VMEM_LIMIT):
    return pltpu.CompilerParams(dimension_semantics=sem, vmem_limit_bytes=vmem)


def _tile(n, pref, mult):
    if n <= pref:
        return n
    t = (pref // mult) * mult
    while t > mult and n % t:
        t -= mult
    assert n % t == 0, (n, pref, mult)
    return t


def _rmsnorm_kernel(x_ref, g_ref, o_ref):
    x = x_ref[...]
    r = lax.rsqrt(jnp.mean(x * x, axis=-1, keepdims=True) + EPS)
    o_ref[...] = ((x * r) * g_ref[...]).astype(o_ref.dtype)


def _final_norm_kernel(x_ref, y_ref, g_ref, op_ref, os_ref, *, first_blocks):
    x = x_ref[...] + y_ref[...]
    r = lax.rsqrt(jnp.mean(x * x, axis=-1, keepdims=True) + EPS)
    val = (x * r) * g_ref[...]

    @pl.when(pl.program_id(0) < first_blocks)
    def _():
        op_ref[...] = val

    @pl.when(pl.program_id(0) >= first_blocks)
    def _():
        os_ref[...] = val


def _final_norm(x, y, g, n_first, n_second):
    d = x.shape[1]
    tm = _tile(n_second, 128, SUBLANES)
    assert n_first % tm == 0
    fb = n_first // tm
    row = pl.BlockSpec((tm, d), lambda i: (i, 0))
    return pl.pallas_call(
        functools.partial(_final_norm_kernel, first_blocks=fb), grid=(fb + n_second // tm,),
        in_specs=[row, row, pl.BlockSpec((1, d), lambda i: (0, 0))],
        out_specs=[pl.BlockSpec((tm, d), lambda i: (jnp.minimum(i, fb - 1), 0)),
                   pl.BlockSpec((tm, d), lambda i: (jnp.maximum(i - fb, 0), 0))],
        out_shape=[jax.ShapeDtypeStruct((n_first, d), F32), jax.ShapeDtypeStruct((n_second, d), F32)],
        compiler_params=_cparams(("arbitrary",)), name="final_norm")(x, y, g.reshape(1, d))


def _rmsnorm(x, g, out_dtype):
    m, d = x.shape
    tm = _tile(m, 640, SUBLANES)
    row = pl.BlockSpec((tm, d), lambda i: (i, 0))
    return pl.pallas_call(
        _rmsnorm_kernel, grid=(m // tm,), in_specs=[row, pl.BlockSpec((1, d), lambda i: (0, 0))], out_specs=row,
        out_shape=jax.ShapeDtypeStruct((m, d), out_dtype),
        compiler_params=_cparams(("parallel",)), name="rmsnorm")(x, g.reshape(1, d))


def _mm_kernel(a_ref, b_ref, *rest, has_res, cast_b):
    rest = list(rest)
    r_ref = rest.pop(0) if has_res else None
    o_ref = rest.pop(0)
    if cast_b:
        bb_ref = rest.pop(0)

        @pl.when(pl.program_id(1) == 0)
        def _():
            bb_ref[...] = b_ref[...].astype(BF16)

        b = bb_ref[...]
    else:
        b = b_ref[...]
    acc = jnp.dot(a_ref[...], b, preferred_element_type=F32)
    if has_res:
        acc = r_ref[...] + acc
    o_ref[...] = acc.astype(o_ref.dtype)


def _matmul(a, b, res=None, n_cols=None, tm_pref=640, out_dtype=F32, name="matmul"):
    m, k = a.shape
    n = b.shape[1] if n_cols is None else n_cols
    cast_b = b.dtype != BF16
    tm = _tile(m, tm_pref, 16)
    tn = _tile(n, 512 if cast_b else 1024, 2 * LANES if n % (2 * LANES) == 0 else LANES)
    a_spec = pl.BlockSpec((tm, k), lambda j, i: (i, 0))
    b_spec = pl.BlockSpec((k, tn), lambda j, i: (0, j))
    o_spec = pl.BlockSpec((tm, tn), lambda j, i: (i, j))
    specs, args = [a_spec, b_spec], [a, b]
    if res is not None:
        specs.append(o_spec)
        args.append(res)
    return pl.pallas_call(
        functools.partial(_mm_kernel, has_res=res is not None, cast_b=cast_b),
        grid=(n // tn, m // tm), in_specs=specs, out_specs=o_spec,
        out_shape=jax.ShapeDtypeStruct((m, n), out_dtype),
        scratch_shapes=[pltpu.VMEM((k, tn), BF16)] if cast_b else [],
        compiler_params=_cparams(("parallel", "arbitrary" if cast_b else "parallel")), name=name)(*args)


def _rope_tables(pos, head_dim):
    rot = head_dim // 4
    half = rot // 2
    inv = ROPE_THETA ** (-jnp.arange(half, dtype=F32) / half)
    ang = pos.astype(F32)[:, None] * inv[None, :]
    cos, sin = jnp.cos(ang), jnp.sin(ang)
    n = pos.shape[0]
    ones = jnp.ones((n, head_dim - rot), F32)
    zr = jnp.zeros((n, head_dim - rot), F32)
    zh = jnp.zeros((n, half), F32)
    c = jnp.concatenate([cos, cos, ones], axis=1)
    sa = jnp.concatenate([-sin, zh, zr], axis=1)
    sb = jnp.concatenate([zh, sin, zr], axis=1)
    reps = LANES // head_dim
    return tuple(jnp.tile(t, (1, reps)) for t in (c, sa, sb))


def _rope(x, c, sa, sb, half):
    return x * c + pltpu.roll(x, LANES - half, 1) * sa + pltpu.roll(x, half, 1) * sb


def _prep_kernel(q_ref, k_ref, v_ref, qi_ref, kw_ref, c1_ref, sa1_ref, sb1_ref, c2_ref, sa2_ref, sb2_ref,
                 qo_ref, kf_ref, kt_ref, vb_ref, qio_ref, kif_ref, wi_ref):
    c1, sa1, sb1 = c1_ref[...], sa1_ref[...], sb1_ref[...]
    c2, sa2, sb2 = c2_ref[...], sa2_ref[...], sb2_ref[...]
    h1 = HEAD_DIM // 8
    h2 = IDX_DIM // 8
    for h in range(q_ref.shape[1] // LANES):
        sl = slice(h * LANES, (h + 1) * LANES)
        qo_ref[:, sl] = (_rope(q_ref[:, sl], c1, sa1, sb1, h1) * QK_SCALE_LOG2E).astype(BF16)
    for h in range(k_ref.shape[1] // LANES):
        sl = slice(h * LANES, (h + 1) * LANES)
        kr = _rope(k_ref[:, sl], c1, sa1, sb1, h1)
        kf_ref[:, sl] = kr
        kt_ref[sl, :] = kr.T.astype(BF16)
    vb_ref[...] = v_ref[...].astype(BF16)
    for h in range(qi_ref.shape[1] // LANES):
        sl = slice(h * LANES, (h + 1) * LANES)
        qio_ref[:, sl] = _rope(qi_ref[:, sl], c2, sa2, sb2, h2).astype(BF16)
    t = kw_ref[...]
    kif_ref[...] = _rope(t, c2, sa2, sb2, h2)[:, :IDX_DIM]
    wi_ref[...] = t[:, IDX_DIM:IDX_DIM + IDX_HEADS] * (IDX_HEADS ** -0.5)


def _prep(y, y_kw, tabs1, tabs2, d, kvw, iqw, off_k, off_v, off_qi):
    n = y.shape[0]
    tm = _tile(n, LANES, LANES)
    tab = pl.BlockSpec((tm, LANES), lambda i: (i, 0))
    in_specs = [
        pl.BlockSpec((tm, d), lambda i: (i, 0)),
        pl.BlockSpec((tm, kvw), lambda i: (i, off_k // kvw)),
        pl.BlockSpec((tm, kvw), lambda i: (i, off_v // kvw)),
        pl.BlockSpec((tm, iqw), lambda i: (i, off_qi // iqw)),
        pl.BlockSpec((tm, LANES), lambda i: (i, 0)),
        tab, tab, tab, tab, tab, tab,
    ]
    outs = [
        (d, BF16), (kvw, F32), None, (kvw, BF16), (iqw, BF16), (IDX_DIM, F32), (IDX_HEADS, F32),
    ]
    row_spec = lambda o: pl.BlockSpec((tm, o[0]), lambda i: (i, 0))
    row_shape = lambda o: jax.ShapeDtypeStruct((n, o[0]), o[1])
    return pl.pallas_call(
        _prep_kernel, grid=(n // tm,), in_specs=in_specs,
        out_specs=[row_spec(o) if o else pl.BlockSpec((kvw, tm), lambda i: (0, i)) for o in outs],
        out_shape=[row_shape(o) if o else jax.ShapeDtypeStruct((kvw, n), BF16) for o in outs],
        compiler_params=_cparams(("parallel",)), name="prep")(y, y, y, y, y_kw, *tabs1, *tabs2)


def _num_k_tiles(i, tq, tk, q_off, s_valid):
    last_chunk_end = ((q_off + i * tq + tq - 1) // CHUNK + 1) * CHUNK
    return (jnp.minimum(last_chunk_end, s_valid) + tk - 1) // tk


def _idx_kernel(qi_ref, wi_ref, klo_ref, khi_ref, o_ref, key_ref, *, tq, tk, nk, q_off, s_valid, n_sel):
    i = pl.program_id(1)
    q0 = q_off + i * tq
    nkt = _num_k_tiles(i, tq, tk, q_off, s_valid)
    wi = wi_ref[...] * (IDX_DIM ** -0.5)
    wcols = [wi[:, h:h + 1] for h in range(IDX_HEADS)]
    q_chunk = (q0 + lax.broadcasted_iota(jnp.int32, (tq, tk), 0)) // CHUNK

    def score_tile(j, carry):
        k0 = j * tk
        klo = klo_ref[0, j]
        khi = khi_ref[0, j]
        acc = jnp.zeros((tq, tk), F32)
        for p in range(IDX_HEADS // 2):
            qp = qi_ref[:, p * LANES:(p + 1) * LANES]
            d0 = jnp.dot(qp, klo, preferred_element_type=F32)
            d1 = jnp.dot(qp, khi, preferred_element_type=F32)
            acc = acc + jnp.maximum(d0, 0.0) * wcols[2 * p]
            acc = acc + jnp.maximum(d1, 0.0) * wcols[2 * p + 1]
        k_pos = k0 + lax.broadcasted_iota(jnp.int32, (tq, tk), 1)
        vis = jnp.logical_and(k_pos // CHUNK <= q_chunk, k_pos < s_valid)
        bits = pltpu.bitcast(acc, jnp.int32)
        key = bits ^ ((bits >> 31) & 0x7FFFFFFF)
        key_ref[j] = jnp.where(vis, key, INT_MIN)
        return carry

    lax.fori_loop(0, nkt, score_tile, 0)

    def count(pred):
        def count_tile(j, acc):
            m = jnp.where(pred(j, key_ref[j]), 1.0, 0.0)
            part = m[:, 0:LANES]
            for c in range(1, tk // LANES):
                part = part + m[:, c * LANES:(c + 1) * LANES]
            return acc + part

        acc = lax.fori_loop(0, nkt, count_tile, jnp.zeros((tq, LANES), F32))
        return jnp.sum(acc, axis=1, keepdims=True)

    def bit_step(it, carry):
        r, cnt_r = carry
        cand = r | jnp.left_shift(jnp.int32(1), 31 - it)
        cs = cand ^ INT_MIN
        cnt = count(lambda j, key: key >= cs)
        take = cnt >= n_sel
        return jnp.where(take, cand, r), jnp.where(take, cnt, cnt_r)

    r, cnt_ge = lax.fori_loop(0, 32, bit_step,
                              (jnp.zeros((tq, 1), jnp.int32), jnp.full((tq, 1), float(n_sel), F32)))
    thr = jnp.maximum(r ^ INT_MIN, INT_MIN + 1)
    has_ties = jnp.max(cnt_ge) > float(n_sel)

    @pl.when(jnp.logical_not(has_ties))
    def _write_plain():
        def write_tile(j, carry):
            o_ref[0, j] = jnp.where(key_ref[j] >= thr, 0.0, NEG).astype(BF16)
            return carry

        lax.fori_loop(0, nkt, write_tile, 0)

    @pl.when(has_ties)
    def _write_tie_broken():
        need = float(n_sel) - count(lambda j, key: key > thr)
        lane = lax.broadcasted_iota(jnp.int32, (tq, tk), 1)

        def pos_step(it, p):
            cand = p | jnp.left_shift(jnp.int32(1), pos_bits - 1 - it)
            cnt = count(lambda j, key: jnp.logical_and(key == thr, lane + j * tk < cand))
            return jnp.where(cnt < need, cand, p)

        pos_bits = max(1, (nk * tk - 1).bit_length())
        p = lax.fori_loop(0, pos_bits, pos_step, jnp.zeros((tq, 1), jnp.int32))

        def write_tile(j, carry):
            key = key_ref[j]
            sel = jnp.logical_or(key > thr, jnp.logical_and(key == thr, lane + j * tk <= p))
            o_ref[0, j] = jnp.where(sel, 0.0, NEG).astype(BF16)
            return carry

        lax.fori_loop(0, nkt, write_tile, 0)

    def fill_tile(j, carry):
        o_ref[0, j] = jnp.full((tq, tk), NEG, BF16)
        return carry

    lax.fori_loop(nkt, nk, fill_tile, 0)


def _idx_mask(qi, wi, row_off, t, klo, khi, *, tq, tk, q_off, s_valid, n_sel):
    b, nk = klo.shape[:2]
    assert klo.shape[3] == tk
    iqw = qi.shape[1]
    assert row_off % tq == 0 and t % tq == 0
    qrow = lambda bb, i: (row_off // tq + bb * (t // tq) + i, 0)
    kern = functools.partial(_idx_kernel, tq=tq, tk=tk, nk=nk, q_off=q_off, s_valid=s_valid, n_sel=n_sel)
    return pl.pallas_call(
        kern, grid=(b, t // tq),
        in_specs=[
            pl.BlockSpec((tq, iqw), qrow),
            pl.BlockSpec((tq, IDX_HEADS), qrow),
            pl.BlockSpec((1, nk, LANES, tk), lambda bb, i: (bb, 0, 0, 0)),
            pl.BlockSpec((1, nk, LANES, tk), lambda bb, i: (bb, 0, 0, 0)),
        ],
        out_specs=pl.BlockSpec((1, nk, tq, tk), lambda bb, i: (bb, 0, i, 0)),
        out_shape=jax.ShapeDtypeStruct((b, nk, t, tk), BF16),
        scratch_shapes=[pltpu.VMEM((nk, tq, tk), jnp.int32)],
        compiler_params=_cparams(("parallel", "parallel")), name="idx_mask")(qi, wi, klo, khi)


def _attn_kernel(q_ref, k_ref, v_ref, b_ref, o_ref, qs_ref, m_ref, acc_ref,
                 *, tq, tk, nk, q_off, s_valid, nkv, grp):
    i = pl.program_id(1)
    j = pl.program_id(2)
    nkt = _num_k_tiles(i, tq, tk, q_off, s_valid)

    @pl.when(j == 0)
    def _init():
        eye = (lax.broadcasted_iota(jnp.int32, (tq, LANES), 0) ==
               lax.broadcasted_iota(jnp.int32, (tq, LANES), 1)).astype(F32).astype(BF16)
        for g in range(nkv):
            for hh in range(grp):
                h = g * grp + hh
                qs_ref[g, hh * tq:(hh + 1) * tq, :HEAD_DIM] = q_ref[:, h * HEAD_DIM:(h + 1) * HEAD_DIM]
                qs_ref[g, hh * tq:(hh + 1) * tq, HEAD_DIM:] = eye
        m_ref[...] = jnp.full(m_ref.shape, NEG, F32)
        acc_ref[...] = jnp.zeros(acc_ref.shape, F32)

    @pl.when(j < nkt)
    def _compute():
        mask = jnp.concatenate([b_ref[0, c] for c in range(b_ref.shape[1])], axis=1)
        if tq < LANES:
            mask = jnp.concatenate([mask, jnp.zeros((LANES - tq, tk), BF16)], axis=0)
        nc = tk // LANES
        ones = jnp.ones((tk, LANES), BF16)
        chunks = lambda a: [a[:, c * LANES:(c + 1) * LANES] for c in range(nc)]
        for g in range(nkv):
            kg = jnp.concatenate([k_ref[g * HEAD_DIM:(g + 1) * HEAD_DIM, :], mask], axis=0)
            vg = jnp.concatenate([v_ref[:, g * HEAD_DIM:(g + 1) * HEAD_DIM], ones], axis=1)
            s = jnp.dot(qs_ref[g], kg, preferred_element_type=F32)
            m_prev = m_ref[g]
            m_new = jnp.maximum(m_prev, jnp.max(functools.reduce(jnp.maximum, chunks(s)), axis=1, keepdims=True))
            p = jnp.exp2(s - jnp.tile(m_new, (1, nc)))
            alpha = jnp.exp2(m_prev - m_new)
            acc_ref[g] = jnp.tile(alpha, (1, 2)) * acc_ref[g] + jnp.dot(p.astype(BF16), vg, preferred_element_type=F32)
            m_ref[g] = m_new

    @pl.when(j == nk - 1)
    def _finish():
        for g in range(nkv):
            for hh in range(grp):
                h = g * grp + hh
                rows = slice(hh * tq, (hh + 1) * tq)
                o_ref[:, h * HEAD_DIM:(h + 1) * HEAD_DIM] = acc_ref[g, rows, :HEAD_DIM] / acc_ref[g, rows, HEAD_DIM:]


def _attn_kernel_aliased(q_ref, k_ref, v_ref, b_ref, prev_ref, o_ref, *scratch, **kw):
    del prev_ref
    _attn_kernel(q_ref, k_ref, v_ref, b_ref, o_ref, *scratch, **kw)


def _attention(q, row_off, t, k, v, s_pad, bias, n_out, prev=None, *, tq, tk, q_off, s_valid):
    b = bias.shape[0]
    aw, kvw = q.shape[1], v.shape[1]
    nk = s_pad // tk
    nkv = kvw // HEAD_DIM
    grp = aw // kvw
    tkb = bias.shape[3]
    assert row_off % tq == 0 and t % tq == 0 and tq <= LANES and tk % tkb == 0
    kw = dict(tq=tq, tk=tk, nk=nk, q_off=q_off, s_valid=s_valid, nkv=nkv, grp=grp)

    def kj(i, j):
        return jnp.minimum(j, _num_k_tiles(i, tq, tk, q_off, s_valid) - 1)

    qrow = lambda bb, i, j: (row_off // tq + bb * (t // tq) + i, 0)
    krow = lambda bb, i, j: (bb * nk + kj(i, j), 0)
    in_specs = [
        pl.BlockSpec((tq, aw), qrow),
        pl.BlockSpec((kvw, tk), lambda bb, i, j: (0, bb * nk + kj(i, j))),
        pl.BlockSpec((tk, kvw), krow),
        pl.BlockSpec((1, tk // tkb, tq, tkb), lambda bb, i, j: (bb, kj(i, j), i, 0)),
    ]
    args = (q, k, v, bias)
    if prev is None:
        kern, aliases = functools.partial(_attn_kernel, **kw), {}
    else:
        kern, aliases = functools.partial(_attn_kernel_aliased, **kw), {4: 0}
        in_specs.append(pl.BlockSpec(memory_space=pl.ANY))
        args += (prev,)
    return pl.pallas_call(
        kern, grid=(b, t // tq, nk), in_specs=in_specs,
        out_specs=pl.BlockSpec((tq, aw), qrow),
        out_shape=jax.ShapeDtypeStruct((n_out, aw), F32),
        scratch_shapes=[
            pltpu.VMEM((nkv, grp * tq, HEAD_DIM + LANES), BF16),
            pltpu.VMEM((nkv, grp * tq, LANES), F32),
            pltpu.VMEM((nkv, grp * tq, 2 * HEAD_DIM), F32),
        ],
        input_output_aliases=aliases,
        compiler_params=_cparams(("parallel", "parallel", "arbitrary")), name="attention")(*args)


def _sigmoid(x):
    return 0.5 * jnp.tanh(0.5 * x) + 0.5


def _gelu(x):
    return 0.5 * x * (1.0 + lax.erf(x * (0.5 ** 0.5)))


def _lru_kernel(x_ref, gl_ref, gb_ref, ga_ref, att_ref, c0_ref, h0_ref, cw_ref, cb_ref, wa_ref, ba_ref, wi_ref, bi_ref,
                lam_ref, o_ref, cn_ref, hn_ref, xp_ref, a_ref, b_ref, hs_ref, h_ref, *, tt, nblk, bw):
    t = pl.program_id(1)
    pad = SUBLANES

    @pl.when(t == 0)
    def _init():
        xp_ref[0:pad, :] = c0_ref[0]
        h_ref[...] = h0_ref[0]

    xp_ref[pad:pad + tt, :] = x_ref[...]
    cw = cw_ref[...]
    base = pad - (CONV_W - 1)
    xc = cb_ref[...] + xp_ref[base:base + tt, :] * cw[0:1]
    for jj in range(1, CONV_W):
        xc = xc + xp_ref[base + jj:base + jj + tt, :] * cw[jj:jj + 1]
    tail = xp_ref[tt:tt + pad, :]
    cn_ref[0] = tail
    xp_ref[0:pad, :] = tail

    xcb = xc.astype(BF16)
    ra, ri = [], []
    for n in range(nblk):
        xs = xcb[:, n * bw:(n + 1) * bw]
        ra.append(jnp.dot(xs, wa_ref[n], preferred_element_type=F32))
        ri.append(jnp.dot(xs, wi_ref[n], preferred_element_type=F32))
    r = _sigmoid(jnp.concatenate(ra, axis=1) + ba_ref[...])
    ig = _sigmoid(jnp.concatenate(ri, axis=1) + bi_ref[...])
    z = -lam_ref[...]
    softplus = jnp.maximum(z, 0.0) + jnp.log1p(jnp.exp(-jnp.abs(z)))
    log_a = (-LRU_C * r) * softplus
    a = jnp.exp(log_a)
    neg_expm1 = -jnp.tanh(log_a) * (a * a + 1.0)
    a_ref[...] = a
    b_ref[...] = jnp.sqrt(neg_expm1) * (ig * xc)

    def step(s, h):
        h = a_ref[pl.ds(s, 1), :] * h + b_ref[pl.ds(s, 1), :]
        hs_ref[pl.ds(s, 1), :] = h
        return h

    h = lax.fori_loop(0, tt, step, h_ref[...], unroll=8)
    h_ref[...] = h
    hn_ref[0] = h
    lru = hs_ref[...] * _gelu(gl_ref[...])
    o_ref[...] = (_sigmoid(ga_ref[...]) * att_ref[...] + _sigmoid(gb_ref[...]) * lru).astype(o_ref.dtype)


def _lru_kernel_aliased(*refs, **kw):
    _lru_kernel(*refs[:14], *refs[15:], **kw)


def _lru(y, row_off, b, t, off_x, off_gl, off_ga, off_gb, att, conv0, h0, conv_w, conv_b, wa, ba, wi, bi, lam,
         prev=None, *, tt):
    c = conv_w.shape[1]
    nblk, bw = wa.shape[0], wa.shape[1]
    assert row_off % tt == 0 and t % tt == 0 and all(o % c == 0 for o in (off_x, off_gl, off_ga, off_gb))
    rows = lambda bb, i: row_off // tt + bb * (t // tt) + i
    col = lambda off: pl.BlockSpec((tt, c), lambda bb, i: (rows(bb, i), off // c))
    vec = pl.BlockSpec((1, c), lambda bb, i: (0, 0))
    wsp = pl.BlockSpec((nblk, bw, bw), lambda bb, i: (0, 0, 0))
    in_specs = [
        col(off_x), col(off_gl), col(off_gb), col(off_ga), col(0),
        pl.BlockSpec((1, SUBLANES, c), lambda bb, i: (bb, 0, 0)),
        pl.BlockSpec((1, 1, c), lambda bb, i: (bb, 0, 0)),
        pl.BlockSpec((CONV_W, c), lambda bb, i: (0, 0)),
        vec, wsp, vec, wsp, vec, vec,
    ]
    args = (y, y, y, y, att, conv0, h0, conv_w, conv_b.reshape(1, c), wa, ba.reshape(1, c), wi, bi.reshape(1, c),
            lam.reshape(1, c))
    kw = dict(tt=tt, nblk=nblk, bw=bw)
    if prev is None:
        kern, aliases = functools.partial(_lru_kernel, **kw), {}
    else:
        kern, aliases = functools.partial(_lru_kernel_aliased, **kw), {14: 0}
        in_specs.append(pl.BlockSpec(memory_space=pl.ANY))
        args += (prev,)
    return pl.pallas_call(
        kern, grid=(b, t // tt), in_specs=in_specs,
        out_specs=[
            pl.BlockSpec((tt, c), lambda bb, i: (rows(bb, i), 0)),
            pl.BlockSpec((1, SUBLANES, c), lambda bb, i: (bb, 0, 0)),
            pl.BlockSpec((1, 1, c), lambda bb, i: (bb, 0, 0)),
        ],
        out_shape=[
            jax.ShapeDtypeStruct(att.shape, BF16),
            jax.ShapeDtypeStruct((b, SUBLANES, c), F32),
            jax.ShapeDtypeStruct((b, 1, c), F32),
        ],
        scratch_shapes=[
            pltpu.VMEM((tt + SUBLANES, c), F32),
            pltpu.VMEM((tt, c), F32),
            pltpu.VMEM((tt, c), F32),
            pltpu.VMEM((tt, c), F32),
            pltpu.VMEM((1, c), F32),
        ],
        input_output_aliases=aliases,
        compiler_params=_cparams(("arbitrary", "arbitrary")), name="rg_lru")(*args)


def _mem_attn_kernel(q_ref, mk_ref, mv_ref, o_ref, *, heads, hd):
    scale = hd ** -0.5
    nt = (((1,), (1,)), ((), ()))
    for h in range(heads):
        sl = slice(h * hd, (h + 1) * hd)
        s = lax.dot_general(q_ref[:, sl].astype(BF16), mk_ref[0, :, sl], nt, preferred_element_type=F32) * scale
        s = s - jnp.max(s, axis=1, keepdims=True)
        e = jnp.exp(s)
        p = e / jnp.sum(e, axis=1, keepdims=True)
        o_ref[:, sl] = jnp.dot(p.astype(BF16), mv_ref[0, :, sl], preferred_element_type=F32).astype(o_ref.dtype)


def _mem_attn_kernel_aliased(q_ref, mk_ref, mv_ref, prev_ref, o_ref, **kw):
    del prev_ref
    _mem_attn_kernel(q_ref, mk_ref, mv_ref, o_ref, **kw)


def _mem_attention(q, row_off, t, mk, mv, prev=None, *, tm):
    b, nm, w = mk.shape
    assert row_off % tm == 0 and t % tm == 0
    kw = dict(heads=MEM_HEADS, hd=w // MEM_HEADS)
    qrow = lambda bb, i: (row_off // tm + bb * (t // tm) + i, 0)
    in_specs = [
        pl.BlockSpec((tm, w), qrow),
        pl.BlockSpec((1, nm, w), lambda bb, i: (bb, 0, 0)),
        pl.BlockSpec((1, nm, w), lambda bb, i: (bb, 0, 0)),
    ]
    args = (q, mk, mv)
    if prev is None:
        kern, aliases = functools.partial(_mem_attn_kernel, **kw), {}
    else:
        kern, aliases = functools.partial(_mem_attn_kernel_aliased, **kw), {3: 0}
        in_specs.append(pl.BlockSpec(memory_space=pl.ANY))
        args += (prev,)
    return pl.pallas_call(
        kern, grid=(b, t // tm), in_specs=in_specs,
        out_specs=pl.BlockSpec((tm, w), qrow),
        out_shape=jax.ShapeDtypeStruct((q.shape[0], w), BF16),
        input_output_aliases=aliases,
        compiler_params=_cparams(("parallel", "parallel")), name="mem_attention")(*args)


def _top_values(x, k):
    vals = []
    for _ in range(k):
        m = jnp.max(x, axis=0, keepdims=True)
        vals.append(m)
        x = jnp.where(x == m, -jnp.inf, x)
    return vals


def _peer_route_kernel(q_ref, sub_ref, ta_ref, sb_ref, ca_ref, eb_ref, *, heads):
    nt = (((1,), (1,)), ((), ()))
    k = PEER_TOPK
    for h in range(heads):
        st = []
        sv = []
        for c in range(2):
            col = (h * 2 + c) * PEER_HALF
            qh = q_ref[:, col:col + PEER_HALF].astype(BF16)
            s = lax.dot_general(sub_ref[h * 2 + c], qh, nt, preferred_element_type=F32)
            st.append(s)
            sv.append(_top_values(s, k + 1))
        rows = [sv[0][a] + sv[1][b] for a in range(k + 1) for b in range((k + 1) // (a + 1))]
        rows += [jnp.full_like(rows[0], -jnp.inf)] * (-len(rows) % SUBLANES)
        tv = _top_values(jnp.concatenate(rows, axis=0), k + 1)
        z = jnp.zeros_like(tv[0])
        for v in tv[:k]:
            z = z + jnp.exp(v - tv[0])
        ta_ref[h] = 0.5 * (tv[k - 1] + tv[k]) - st[0]
        sb_ref[h] = st[1]
        ca_ref[h] = jnp.exp(st[0] - sv[0][0]) / z
        eb_ref[h] = jnp.exp(st[1] - sv[1][0])


def _peer_route(qp, sub, *, tn):
    n = qp.shape[0]
    heads = PEER_HEADS
    kern = functools.partial(_peer_route_kernel, heads=heads)
    big = pl.BlockSpec((heads, PEER_NKEYS, tn), lambda i: (0, 0, i))
    big_shape = jax.ShapeDtypeStruct((heads, PEER_NKEYS, n), F32)
    return pl.pallas_call(
        kern, grid=(n // tn,),
        in_specs=[
            pl.BlockSpec((tn, qp.shape[1]), lambda i: (i, 0)),
            pl.BlockSpec(sub.shape, lambda i: (0, 0, 0)),
        ],
        out_specs=[big, big, big, big],
        out_shape=[big_shape, big_shape, big_shape, big_shape],
        compiler_params=_cparams(("parallel",)), name="peer_route")(qp, sub)


def _peer_dense_kernel(x_ref, u_ref, v_ref, ta_ref, ca_ref, sb_ref, eb_ref, o_ref, *, heads, rows):
    j = pl.program_id(1)

    @pl.when(j == 0)
    def _init():
        o_ref[...] = jnp.zeros(o_ref.shape, F32)

    act = _gelu(jnp.dot(x_ref[...], u_ref[...], preferred_element_type=F32))
    tiles = []
    for r in range(rows):
        w = None
        for h in range(heads):
            term = jnp.where(sb_ref[h] >= ta_ref[r, h:h + 1, :], eb_ref[h], 0.0) * ca_ref[r, h:h + 1, :]
            w = term if w is None else w + term
        tiles.append(w)
    gate = jnp.concatenate(tiles, axis=0) if rows > 1 else tiles[0]
    coef = (gate.T * act).astype(BF16)
    o_ref[...] += jnp.dot(coef, v_ref[...], preferred_element_type=F32)


def _peer_dense(xn, u, v, ta, ca, sb, eb, *, tn, te):
    n, d = xn.shape
    e = v.shape[0]
    heads = sb.shape[0]
    rows = te // PEER_NKEYS
    kern = functools.partial(_peer_dense_kernel, heads=heads, rows=rows)
    once = pl.Buffered(1)
    row_blk = pl.BlockSpec((rows, heads, tn), lambda i, j: (j, 0, i))
    big = pl.BlockSpec((heads, PEER_NKEYS, tn), lambda i, j: (0, 0, i), pipeline_mode=once)
    return pl.pallas_call(
        kern, grid=(n // tn, e // te),
        in_specs=[
            pl.BlockSpec((tn, d), lambda i, j: (i, 0), pipeline_mode=once),
            pl.BlockSpec((d, te), lambda i, j: (0, j)),
            pl.BlockSpec((te, d), lambda i, j: (j, 0)),
            row_blk, row_blk, big, big,
        ],
        out_specs=pl.BlockSpec((tn, d), lambda i, j: (i, 0)),
        out_shape=jax.ShapeDtypeStruct((n, d), F32),
        compiler_params=_cparams(("parallel", "arbitrary")), name="peer_dense")(xn, u, v, ta, ca, sb, eb)


def _pad_rows(x, n):
    if x.shape[0] == n:
        return x
    return jnp.pad(x, ((0, n - x.shape[0]),) + ((0, 0),) * (x.ndim - 1))


def _index_keys(ki, s_pad, tk):
    b = ki.shape[0]
    kt = jnp.swapaxes(jnp.pad(ki.astype(BF16), ((0, 0), (0, s_pad - ki.shape[1]), (0, 0))), 1, 2)
    tiles = lambda x: jnp.swapaxes(x.reshape(b, LANES, s_pad // tk, tk), 1, 2)
    lo = tiles(jnp.pad(kt, ((0, 0), (0, LANES - IDX_DIM), (0, 0))))
    hi = tiles(jnp.pad(kt, ((0, 0), (LANES - IDX_DIM, 0), (0, 0))))
    return lo, hi


def kernel(x_prompt, x_sample, mem_prompt, cache_k, cache_v, cache_kidx, state_conv, state_lru, cache_mem_k, cache_mem_v, norm_mix_g, w_in, conv_w, conv_b, lru_wa, lru_ba, lru_wi, lru_bi, lru_lambda, w_out, norm_mem_g, norm_memkv_g, mem_wq, mem_wk, mem_wv, mem_wo, norm_ffn_g, peer_wq, peer_subkeys, peer_u, peer_v, norm_final_g):
    assert w_in.shape[0] == 1, "single-layer step"
    assert IDX_DIM * 2 == LANES and PEER_HALF == LANES and PEER_NKEYS == LANES and HEAD_DIM == LANES
    bp, tp, d = x_prompt.shape
    bs, ts, _ = x_sample.shape
    past = cache_k.shape[2]
    n_p, n_s = bp * tp, bs * ts
    n_tok = n_p + n_s
    n_pad = -(-n_tok // TOKEN_TILE) * TOKEN_TILE
    kvw = N_KV_HEADS * HEAD_DIM
    iqw = IDX_HEADS * IDX_DIM
    n_mem = mem_prompt.shape[1]
    mem_w = mem_wq.shape[2]

    x0 = _pad_rows(jnp.concatenate([x_prompt.reshape(n_p, d), x_sample.reshape(n_s, d)], axis=0), n_pad)
    pos = _pad_rows(jnp.concatenate([jnp.tile(jnp.arange(tp, dtype=jnp.int32), bp),
                                     jnp.tile(past + jnp.arange(ts, dtype=jnp.int32), bs)]), n_pad)

    w = w_in[0]
    o_k, o_v, o_qi, o_ki = d, d + kvw, d + 2 * kvw, d + 2 * kvw + iqw
    o_xl = o_ki + IDX_DIM + IDX_HEADS
    assert o_xl + 4 * d == w.shape[1]
    y_xl, y_gl, y_ga, y_gb = 0, d, 2 * d, 3 * d

    xn = _rmsnorm(x0, norm_mix_g[0], BF16)
    ya = _matmul(xn, w, n_cols=o_ki, name="in_proj_qkv")
    yb = _matmul(xn, w[:, o_xl:].astype(BF16), name="in_proj_lru")
    yc = _matmul(xn, jnp.pad(w[:, o_ki:o_xl], ((0, 0), (0, LANES - IDX_DIM - IDX_HEADS))), name="in_proj_idx")

    tabs1 = _rope_tables(pos, HEAD_DIM)
    tabs2 = _rope_tables(pos, IDX_DIM)
    q_bf, k_f, k_t, v_bf, qi_bf, ki_f, wi = _prep(ya, yc, tabs1, tabs2, d, kvw, iqw, o_k, o_v, o_qi)
    v_f = ya[:n_tok, o_v:o_v + kvw]

    n_sel_p = min(TOPK_MAX, tp // 4)
    tk_p = _tile(tp, 512, LANES)
    klo, khi = _index_keys(ki_f[:n_p].reshape(bp, tp, IDX_DIM), tp, tk_p)
    bias_p = _idx_mask(qi_bf, wi, 0, tp, klo, khi, tq=_tile(tp, 128, 16), tk=tk_p, q_off=0, s_valid=tp,
                       n_sel=n_sel_p)
    att = _attention(q_bf, 0, tp, k_t, v_bf, tp, bias_p, n_pad,
                     tq=_tile(tp, 128, 16), tk=_tile(tp, 1024, tk_p), q_off=0, s_valid=tp)

    s_s = past + ts
    s_pad = -(-s_s // LANES) * LANES
    n_sel_s = min(TOPK_MAX, s_s // 4)
    ki_s = jnp.concatenate([cache_kidx[0], ki_f[n_p:n_tok].reshape(bs, ts, IDX_DIM)], axis=1)
    klo_s, khi_s = _index_keys(ki_s, s_pad, s_pad)
    bias_s = _idx_mask(qi_bf, wi, n_p, ts, klo_s, khi_s, tq=ts, tk=s_pad, q_off=past, s_valid=s_s, n_sel=n_sel_s)

    v_s = jnp.concatenate([cache_v[0].reshape(bs, past, kvw).astype(BF16), v_bf[n_p:n_tok].reshape(bs, ts, kvw)],
                          axis=1)
    v_s = jnp.pad(v_s, ((0, 0), (0, s_pad - s_s), (0, 0))).reshape(bs * s_pad, kvw)
    kt_s = jnp.concatenate([jnp.transpose(cache_k[0].reshape(bs, past, kvw).astype(BF16), (2, 0, 1)),
                            k_t[:, n_p:n_tok].reshape(kvw, bs, ts)], axis=2)
    kt_s = jnp.pad(kt_s, ((0, 0), (0, 0), (0, s_pad - s_s))).reshape(kvw, bs * s_pad)
    att = _attention(q_bf, n_p, ts, kt_s, v_s, s_pad, bias_s, n_pad, prev=att,
                     tq=ts, tk=s_pad, q_off=past, s_valid=s_s)

    wa_bf, wi_bf = lru_wa[0].astype(BF16), lru_wi[0].astype(BF16)
    lru_args = (conv_w[0], conv_b[0], wa_bf, lru_ba[0], wi_bf, lru_bi[0], lru_lambda[0])
    state_pad = SUBLANES - (CONV_W - 1)
    merged, cn_p, hn_p = _lru(yb, 0, bp, tp, y_xl, y_gl, y_ga, y_gb, att, jnp.zeros((bp, SUBLANES, d), F32),
                              jnp.zeros((bp, 1, d), F32), *lru_args, tt=_tile(tp, 128, SUBLANES))
    conv0_s = jnp.pad(state_conv[0], ((0, 0), (state_pad, 0), (0, 0)))
    merged, cn_s, hn_s = _lru(yb, n_p, bs, ts, y_xl, y_gl, y_ga, y_gb, att, conv0_s, state_lru[0].reshape(bs, 1, d),
                              *lru_args, prev=merged, tt=ts)
    if n_pad > n_tok:
        merged = merged.at[n_tok:].set(0.0)

    x1 = _matmul(merged, w_out[0], res=x0, name="out_proj")

    mem_n = _rmsnorm(mem_prompt.reshape(bp * n_mem, d), norm_memkv_g[0], BF16)
    mkv = _matmul(mem_n, jnp.concatenate([mem_wk[0], mem_wv[0]], axis=1), name="mem_kv")
    mk_p = mkv[:, :mem_w].reshape(bp, n_mem, mem_w)
    mv_p = mkv[:, mem_w:].reshape(bp, n_mem, mem_w)
    xn2 = _rmsnorm(x1, norm_mem_g[0], BF16)
    qm = _matmul(xn2, mem_wq[0], name="mem_q")
    om = _mem_attention(qm, 0, tp, mk_p.astype(BF16), mv_p.astype(BF16), tm=_tile(tp, 512, 16))
    om = _mem_attention(qm, n_p, ts, cache_mem_k[0].reshape(bs, n_mem, mem_w).astype(BF16),
                        cache_mem_v[0].reshape(bs, n_mem, mem_w).astype(BF16), prev=om, tm=ts)
    if n_pad > n_tok:
        om = om.at[n_tok:].set(0.0)
    x2 = _matmul(om, mem_wo[0], res=x1, name="mem_out")

    xn3 = _rmsnorm(x2, norm_ffn_g[0], BF16)
    qp = _matmul(xn3, peer_wq[0], name="peer_q")
    sub = peer_subkeys[0].reshape(PEER_HEADS * 2, PEER_NKEYS, PEER_HALF).astype(BF16)
    ta, sb, ca, eb = _peer_route(qp, sub, tn=LANES)
    peer = _peer_dense(xn3, peer_u[0].T.astype(BF16), peer_v[0].astype(BF16),
                       jnp.transpose(ta, (1, 0, 2)), jnp.transpose(ca, (1, 0, 2)), sb, eb,
                       tn=_tile(n_pad, 640, LANES), te=512)

    y_p, y_s = _final_norm(x2, peer, norm_final_g, n_p, n_s)

    kv_shape_p = (1, bp, tp, N_KV_HEADS, HEAD_DIM)
    kv_shape_s = (1, bs, ts, N_KV_HEADS, HEAD_DIM)
    tail = slice(SUBLANES - (CONV_W - 1), SUBLANES)
    return (
        y_p.reshape(bp, tp, d), y_s.reshape(bs, ts, d),
        k_f[:n_p].reshape(kv_shape_p), v_f[:n_p].reshape(kv_shape_p), ki_f[:n_p].reshape(1, bp, tp, IDX_DIM),
        cn_p[:, tail][None], hn_p.reshape(1, bp, d),
        mk_p.reshape(1, bp, n_mem, MEM_HEADS, mem_w // MEM_HEADS),
        mv_p.reshape(1, bp, n_mem, MEM_HEADS, mem_w // MEM_HEADS),
        k_f[n_p:n_tok].reshape(kv_shape_s), v_f[n_p:].reshape(kv_shape_s),
        ki_f[n_p:n_tok].reshape(1, bs, ts, IDX_DIM),
        cn_s[:, tail][None], hn_s.reshape(1, bs, d),
    )
```

```python
import functools

import jax
import jax.numpy as jnp
from jax import lax
from jax.experimental import pallas as pl
from jax.experimental.pallas import tpu as pltpu

F32 = jnp.float32
BF16 = jnp.bfloat16

CHUNK = 64
HEAD_DIM = 128
N_KV_HEADS = 8
IDX_HEADS = 16
IDX_DIM = 64
TOPK_MAX = 256
ROPE_THETA = 500000.0
LRU_BLOCKS = 16
CONV_W = 4
LRU_C = 8.0
MEM_HEADS = 4
PEER_HEADS = 8
PEER_NKEYS = 128
PEER_HALF = 128
PEER_TOPK = 16
EPS = 1e-6

LANES = 128
SUBLANES = 8
TOKEN_TILE = 128
VMEM_LIMIT = 56 * 1024 * 1024
NEG = -1e30
QK_SCALE_LOG2E = (HEAD_DIM ** -0.5) * 1.4426950408889634
INT_MIN = -(2 ** 31)


def _cparams(sem, vmem=VMEM_LIMIT):
    return pltpu.CompilerParams(dimension_semantics=sem, vmem_limit_bytes=vmem)


def _tile(n, pref, mult):
    if n <= pref:
        return n
    t = (pref // mult) * mult
    while t > mult and n % t:
        t -= mult
    assert n % t == 0, (n, pref, mult)
    return t


def _rmsnorm_kernel(x_ref, g_ref, o_ref):
    x = x_ref[...]
    r = lax.rsqrt(jnp.mean(x * x, axis=-1, keepdims=True) + EPS)
    o_ref[...] = ((x * r) * g_ref[...]).astype(o_ref.dtype)


def _final_norm_kernel(x_ref, y_ref, g_ref, op_ref, os_ref, *, first_blocks):
    x = x_ref[...] + y_ref[...]
    r = lax.rsqrt(jnp.mean(x * x, axis=-1, keepdims=True) + EPS)
    val = (x * r) * g_ref[...]

    @pl.when(pl.program_id(0) < first_blocks)
    def _():
        op_ref[...] = val

    @pl.when(pl.program_id(0) >= first_blocks)
    def _():
        os_ref[...] = val


def _final_norm(x, y, g, n_first, n_second):
    d = x.shape[1]
    tm = _tile(n_second, 128, SUBLANES)
    assert n_first % tm == 0
    fb = n_first // tm
    row = pl.BlockSpec((tm, d), lambda i: (i, 0))
    return pl.pallas_call(
        functools.partial(_final_norm_kernel, first_blocks=fb), grid=(fb + n_second // tm,),
        in_specs=[row, row, pl.BlockSpec((1, d), lambda i: (0, 0))],
        out_specs=[pl.BlockSpec((tm, d), lambda i: (jnp.minimum(i, fb - 1), 0)),
                   pl.BlockSpec((tm, d), lambda i: (jnp.maximum(i - fb, 0), 0))],
        out_shape=[jax.ShapeDtypeStruct((n_first, d), F32), jax.ShapeDtypeStruct((n_second, d), F32)],
        compiler_params=_cparams(("arbitrary",)), name="final_norm")(x, y, g.reshape(1, d))


def _rmsnorm(x, g, out_dtype):
    m, d = x.shape
    tm = _tile(m, 640, SUBLANES)
    row = pl.BlockSpec((tm, d), lambda i: (i, 0))
    return pl.pallas_call(
        _rmsnorm_kernel, grid=(m // tm,), in_specs=[row, pl.BlockSpec((1, d), lambda i: (0, 0))], out_specs=row,
        out_shape=jax.ShapeDtypeStruct((m, d), out_dtype),
        compiler_params=_cparams(("parallel",)), name="rmsnorm")(x, g.reshape(1, d))


def _mm_kernel(a_ref, b_ref, *rest, has_res, cast_b):
    rest = list(rest)
    r_ref = rest.pop(0) if has_res else None
    o_ref = rest.pop(0)
    if cast_b:
        bb_ref = rest.pop(0)

        @pl.when(pl.program_id(1) == 0)
        def _():
            bb_ref[...] = b_ref[...].astype(BF16)

        b = bb_ref[...]
    else:
        b = b_ref[...]
    acc = jnp.dot(a_ref[...], b, preferred_element_type=F32)
    if has_res:
        acc = r_ref[...] + acc
    o_ref[...] = acc.astype(o_ref.dtype)


def _matmul(a, b, res=None, n_cols=None, tm_pref=640, out_dtype=F32, name="matmul"):
    m, k = a.shape
    n = b.shape[-1] if n_cols is None else n_cols
    cast_b = b.dtype != BF16
    tm = _tile(m, tm_pref, 16)
    tn = _tile(n, 512 if cast_b else 1024, 2 * LANES if n % (2 * LANES) == 0 else LANES)
    a_spec = pl.BlockSpec((tm, k), lambda j, i: (i, 0))
    b_spec = (pl.BlockSpec((k, tn), lambda j, i: (0, j)) if b.ndim == 2 else
              pl.BlockSpec((None, k, tn), lambda j, i: (0, 0, j)))
    o_spec = pl.BlockSpec((tm, tn), lambda j, i: (i, j))
    specs, args = [a_spec, b_spec], [a, b]
    if res is not None:
        specs.append(o_spec)
        args.append(res)
    return pl.pallas_call(
        functools.partial(_mm_kernel, has_res=res is not None, cast_b=cast_b),
        grid=(n // tn, m // tm), in_specs=specs, out_specs=o_spec,
        out_shape=jax.ShapeDtypeStruct((m, n), out_dtype),
        scratch_shapes=[pltpu.VMEM((k, tn), BF16)] if cast_b else [],
        compiler_params=_cparams(("parallel", "arbitrary" if cast_b else "parallel")), name=name)(*args)


def _mm_nt_kernel(a_ref, bt_ref, o_ref, bb_ref):
    @pl.when(pl.program_id(1) == 0)
    def _():
        bb_ref[...] = bt_ref[...].astype(BF16)

    nt = (((1,), (1,)), ((), ()))
    o_ref[...] = lax.dot_general(a_ref[...], bb_ref[...], nt, preferred_element_type=F32)


def _matmul_nt(a, bt, row_start, n, name):
    m, k = a.shape
    tm = _tile(m, 640, 16)
    tn = _tile(n, 512, LANES)
    assert row_start % SUBLANES == 0 and row_start + n <= bt.shape[0], (row_start, n, bt.shape)
    return pl.pallas_call(
        _mm_nt_kernel, grid=(n // tn, m // tm),
        in_specs=[pl.BlockSpec((tm, k), lambda j, i: (i, 0)),
                  pl.BlockSpec((pl.Element(tn), pl.Element(k)),
                               lambda j, i: (pl.multiple_of(row_start + j * tn, SUBLANES), 0))],
        out_specs=pl.BlockSpec((tm, tn), lambda j, i: (i, j)),
        out_shape=jax.ShapeDtypeStruct((m, n), F32),
        scratch_shapes=[pltpu.VMEM((tn, k), BF16)],
        compiler_params=_cparams(("parallel", "arbitrary")), name=name)(a, bt)


def _rope_tables(pos, head_dim):
    rot = head_dim // 4
    half = rot // 2
    inv = ROPE_THETA ** (-jnp.arange(half, dtype=F32) / half)
    ang = pos.astype(F32)[:, None] * inv[None, :]
    cos, sin = jnp.cos(ang), jnp.sin(ang)
    n = pos.shape[0]
    ones = jnp.ones((n, head_dim - rot), F32)
    zr = jnp.zeros((n, head_dim - rot), F32)
    zh = jnp.zeros((n, half), F32)
    c = jnp.concatenate([cos, cos, ones], axis=1)
    sa = jnp.concatenate([-sin, zh, zr], axis=1)
    sb = jnp.concatenate([zh, sin, zr], axis=1)
    reps = LANES // head_dim
    return tuple(jnp.tile(t, (1, reps)) for t in (c, sa, sb))


def _rope(x, c, sa, sb, half):
    return x * c + pltpu.roll(x, LANES - half, 1) * sa + pltpu.roll(x, half, 1) * sb


def _prep_kernel(q_ref, k_ref, v_ref, qi_ref, kw_ref, c1_ref, sa1_ref, sb1_ref, c2_ref, sa2_ref, sb2_ref,
                 qo_ref, kf_ref, kt_ref, vb_ref, qio_ref, kif_ref, wi_ref):
    c1, sa1, sb1 = c1_ref[...], sa1_ref[...], sb1_ref[...]
    c2, sa2, sb2 = c2_ref[...], sa2_ref[...], sb2_ref[...]
    h1 = HEAD_DIM // 8
    h2 = IDX_DIM // 8
    for h in range(q_ref.shape[1] // LANES):
        sl = slice(h * LANES, (h + 1) * LANES)
        qo_ref[:, sl] = (_rope(q_ref[:, sl], c1, sa1, sb1, h1) * QK_SCALE_LOG2E).astype(BF16)
    for h in range(k_ref.shape[1] // LANES):
        sl = slice(h * LANES, (h + 1) * LANES)
        kr = _rope(k_ref[:, sl], c1, sa1, sb1, h1)
        kf_ref[:, sl] = kr
        kt_ref[sl, :] = kr.T.astype(BF16)
    vb_ref[...] = v_ref[...].astype(BF16)
    for h in range(qi_ref.shape[1] // LANES):
        sl = slice(h * LANES, (h + 1) * LANES)
        qio_ref[:, sl] = _rope(qi_ref[:, sl], c2, sa2, sb2, h2).astype(BF16)
    t = kw_ref[...]
    kif_ref[...] = _rope(t, c2, sa2, sb2, h2)[:, :IDX_DIM]
    wi_ref[...] = t[:, IDX_DIM:IDX_DIM + IDX_HEADS] * (IDX_HEADS ** -0.5)


def _prep(y, y_kw, tabs1, tabs2, d, kvw, iqw, off_k, off_v, off_qi):
    n = y.shape[0]
    tm = _tile(n, LANES, LANES)
    tab = pl.BlockSpec((tm, LANES), lambda i: (i, 0))
    in_specs = [
        pl.BlockSpec((tm, d), lambda i: (i, 0)),
        pl.BlockSpec((tm, kvw), lambda i: (i, off_k // kvw)),
        pl.BlockSpec((tm, kvw), lambda i: (i, off_v // kvw)),
        pl.BlockSpec((tm, iqw), lambda i: (i, off_qi // iqw)),
        pl.BlockSpec((tm, LANES), lambda i: (i, 0)),
        tab, tab, tab, tab, tab, tab,
    ]
    outs = [
        (d, BF16), (kvw, F32), None, (kvw, BF16), (iqw, BF16), (IDX_DIM, F32), (IDX_HEADS, F32),
    ]
    row_spec = lambda o: pl.BlockSpec((tm, o[0]), lambda i: (i, 0))
    row_shape = lambda o: jax.ShapeDtypeStruct((n, o[0]), o[1])
    return pl.pallas_call(
        _prep_kernel, grid=(n // tm,), in_specs=in_specs,
        out_specs=[row_spec(o) if o else pl.BlockSpec((kvw, tm), lambda i: (0, i)) for o in outs],
        out_shape=[row_shape(o) if o else jax.ShapeDtypeStruct((kvw, n), BF16) for o in outs],
        compiler_params=_cparams(("parallel",)), name="prep")(y, y, y, y, y_kw, *tabs1, *tabs2)


def _num_k_tiles(i, tq, tk, q_off, s_valid):
    last_chunk_end = ((q_off + i * tq + tq - 1) // CHUNK + 1) * CHUNK
    return (jnp.minimum(last_chunk_end, s_valid) + tk - 1) // tk


def _idx_kernel(qi_ref, wi_ref, klo_ref, khi_ref, o_ref, key_ref, *, tq, tk, nk, q_off, s_valid, n_sel):
    i = pl.program_id(1)
    q0 = q_off + i * tq
    nkt = _num_k_tiles(i, tq, tk, q_off, s_valid)
    wi = wi_ref[...] * (IDX_DIM ** -0.5)
    wcols = [wi[:, h:h + 1] for h in range(IDX_HEADS)]
    q_chunk = (q0 + lax.broadcasted_iota(jnp.int32, (tq, tk), 0)) // CHUNK

    def score_tile(j, carry):
        k0 = j * tk
        klo = klo_ref[0, j]
        khi = khi_ref[0, j]
        acc = jnp.zeros((tq, tk), F32)
        for p in range(IDX_HEADS // 2):
            qp = qi_ref[:, p * LANES:(p + 1) * LANES]
            d0 = jnp.dot(qp, klo, preferred_element_type=F32)
            d1 = jnp.dot(qp, khi, preferred_element_type=F32)
            acc = acc + jnp.maximum(d0, 0.0) * wcols[2 * p]
            acc = acc + jnp.maximum(d1, 0.0) * wcols[2 * p + 1]
        k_pos = k0 + lax.broadcasted_iota(jnp.int32, (tq, tk), 1)
        vis = jnp.logical_and(k_pos // CHUNK <= q_chunk, k_pos < s_valid)
        bits = pltpu.bitcast(acc, jnp.int32)
        key = bits ^ ((bits >> 31) & 0x7FFFFFFF)
        key_ref[j] = jnp.where(vis, key, INT_MIN)
        return carry

    lax.fori_loop(0, nkt, score_tile, 0)

    def count(pred):
        def count_tile(j, acc):
            m = jnp.where(pred(j, key_ref[j]), 1.0, 0.0)
            part = m[:, 0:LANES]
            for c in range(1, tk // LANES):
                part = part + m[:, c * LANES:(c + 1) * LANES]
            return acc + part

        acc = lax.fori_loop(0, nkt, count_tile, jnp.zeros((tq, LANES), F32))
        return jnp.sum(acc, axis=1, keepdims=True)

    def bit_step(it, carry):
        r, cnt_r = carry
        cand = r | jnp.left_shift(jnp.int32(1), 31 - it)
        cs = cand ^ INT_MIN
        cnt = count(lambda j, key: key >= cs)
        take = cnt >= n_sel
        return jnp.where(take, cand, r), jnp.where(take, cnt, cnt_r)

    r, cnt_ge = lax.fori_loop(0, 32, bit_step,
                              (jnp.zeros((tq, 1), jnp.int32), jnp.full((tq, 1), float(n_sel), F32)))
    thr = jnp.maximum(r ^ INT_MIN, INT_MIN + 1)
    has_ties = jnp.max(cnt_ge) > float(n_sel)

    @pl.when(jnp.logical_not(has_ties))
    def _write_plain():
        def write_tile(j, carry):
            o_ref[0, j] = jnp.where(key_ref[j] >= thr, 0.0, NEG).astype(BF16)
            return carry

        lax.fori_loop(0, nkt, write_tile, 0)

    @pl.when(has_ties)
    def _write_tie_broken():
        need = float(n_sel) - count(lambda j, key: key > thr)
        lane = lax.broadcasted_iota(jnp.int32, (tq, tk), 1)

        def pos_step(it, p):
            cand = p | jnp.left_shift(jnp.int32(1), pos_bits - 1 - it)
            cnt = count(lambda j, key: jnp.logical_and(key == thr, lane + j * tk < cand))
            return jnp.where(cnt < need, cand, p)

        pos_bits = max(1, (nk * tk - 1).bit_length())
        p = lax.fori_loop(0, pos_bits, pos_step, jnp.zeros((tq, 1), jnp.int32))

        def write_tile(j, carry):
            key = key_ref[j]
            sel = jnp.logical_or(key > thr, jnp.logical_and(key == thr, lane + j * tk <= p))
            o_ref[0, j] = jnp.where(sel, 0.0, NEG).astype(BF16)
            return carry

        lax.fori_loop(0, nkt, write_tile, 0)

    def fill_tile(j, carry):
        o_ref[0, j] = jnp.full((tq, tk), NEG, BF16)
        return carry

    lax.fori_loop(nkt, nk, fill_tile, 0)


def _idx_mask(qi, wi, row_off, t, klo, khi, *, tq, tk, q_off, s_valid, n_sel):
    b, nk = klo.shape[:2]
    assert klo.shape[3] == tk
    iqw = qi.shape[1]
    assert row_off % tq == 0 and t % tq == 0
    qrow = lambda bb, i: (row_off // tq + bb * (t // tq) + i, 0)
    kern = functools.partial(_idx_kernel, tq=tq, tk=tk, nk=nk, q_off=q_off, s_valid=s_valid, n_sel=n_sel)
    return pl.pallas_call(
        kern, grid=(b, t // tq),
        in_specs=[
            pl.BlockSpec((tq, iqw), qrow),
            pl.BlockSpec((tq, IDX_HEADS), qrow),
            pl.BlockSpec((1, nk, LANES, tk), lambda bb, i: (bb, 0, 0, 0)),
            pl.BlockSpec((1, nk, LANES, tk), lambda bb, i: (bb, 0, 0, 0)),
        ],
        out_specs=pl.BlockSpec((1, nk, tq, tk), lambda bb, i: (bb, 0, i, 0)),
        out_shape=jax.ShapeDtypeStruct((b, nk, t, tk), BF16),
        scratch_shapes=[pltpu.VMEM((nk, tq, tk), jnp.int32)],
        compiler_params=_cparams(("parallel", "parallel")), name="idx_mask")(qi, wi, klo, khi)


def _attn_kernel(q_ref, k_ref, v_ref, b_ref, o_ref, qs_ref, m_ref, acc_ref,
                 *, tq, tk, nk, q_off, s_valid, nkv, grp):
    i = pl.program_id(1)
    j = pl.program_id(2)
    nkt = _num_k_tiles(i, tq, tk, q_off, s_valid)

    @pl.when(j == 0)
    def _init():
        eye = (lax.broadcasted_iota(jnp.int32, (tq, LANES), 0) ==
               lax.broadcasted_iota(jnp.int32, (tq, LANES), 1)).astype(F32).astype(BF16)
        for g in range(nkv):
            for hh in range(grp):
                h = g * grp + hh
                qs_ref[g, hh * tq:(hh + 1) * tq, :HEAD_DIM] = q_ref[:, h * HEAD_DIM:(h + 1) * HEAD_DIM]
                qs_ref[g, hh * tq:(hh + 1) * tq, HEAD_DIM:] = eye
        m_ref[...] = jnp.full(m_ref.shape, NEG, F32)
        acc_ref[...] = jnp.zeros(acc_ref.shape, F32)

    @pl.when(j < nkt)
    def _compute():
        mask = jnp.concatenate([b_ref[0, c] for c in range(b_ref.shape[1])], axis=1)
        if tq < LANES:
            mask = jnp.concatenate([mask, jnp.zeros((LANES - tq, tk), BF16)], axis=0)
        nc = tk // LANES
        ones = jnp.ones((tk, LANES), BF16)
        chunks = lambda a: [a[:, c * LANES:(c + 1) * LANES] for c in range(nc)]
        for g in range(nkv):
            kg = jnp.concatenate([k_ref[g * HEAD_DIM:(g + 1) * HEAD_DIM, :], mask], axis=0)
            vg = jnp.concatenate([v_ref[:, g * HEAD_DIM:(g + 1) * HEAD_DIM], ones], axis=1)
            s = jnp.dot(qs_ref[g], kg, preferred_element_type=F32)
            m_prev = m_ref[g]
            m_new = jnp.maximum(m_prev, jnp.max(functools.reduce(jnp.maximum, chunks(s)), axis=1, keepdims=True))
            p = jnp.exp2(s - jnp.tile(m_new, (1, nc)))
            alpha = jnp.exp2(m_prev - m_new)
            acc_ref[g] = jnp.tile(alpha, (1, 2)) * acc_ref[g] + jnp.dot(p.astype(BF16), vg, preferred_element_type=F32)
            m_ref[g] = m_new

    @pl.when(j == nk - 1)
    def _finish():
        for g in range(nkv):
            for hh in range(grp):
                h = g * grp + hh
                rows = slice(hh * tq, (hh + 1) * tq)
                o_ref[:, h * HEAD_DIM:(h + 1) * HEAD_DIM] = acc_ref[g, rows, :HEAD_DIM] / acc_ref[g, rows, HEAD_DIM:]


def _attn_kernel_aliased(q_ref, k_ref, v_ref, b_ref, prev_ref, o_ref, *scratch, **kw):
    del prev_ref
    _attn_kernel(q_ref, k_ref, v_ref, b_ref, o_ref, *scratch, **kw)


def _attention(q, row_off, t, k, v, s_pad, bias, n_out, prev=None, *, tq, tk, q_off, s_valid):
    b = bias.shape[0]
    aw, kvw = q.shape[1], v.shape[1]
    nk = s_pad // tk
    nkv = kvw // HEAD_DIM
    grp = aw // kvw
    tkb = bias.shape[3]
    assert row_off % tq == 0 and t % tq == 0 and tq <= LANES and tk % tkb == 0
    kw = dict(tq=tq, tk=tk, nk=nk, q_off=q_off, s_valid=s_valid, nkv=nkv, grp=grp)

    def kj(i, j):
        return jnp.minimum(j, _num_k_tiles(i, tq, tk, q_off, s_valid) - 1)

    qrow = lambda bb, i, j: (row_off // tq + bb * (t // tq) + i, 0)
    krow = lambda bb, i, j: (bb * nk + kj(i, j), 0)
    in_specs = [
        pl.BlockSpec((tq, aw), qrow),
        pl.BlockSpec((kvw, tk), lambda bb, i, j: (0, bb * nk + kj(i, j))),
        pl.BlockSpec((tk, kvw), krow),
        pl.BlockSpec((1, tk // tkb, tq, tkb), lambda bb, i, j: (bb, kj(i, j), i, 0)),
    ]
    args = (q, k, v, bias)
    if prev is None:
        kern, aliases = functools.partial(_attn_kernel, **kw), {}
    else:
        kern, aliases = functools.partial(_attn_kernel_aliased, **kw), {4: 0}
        in_specs.append(pl.BlockSpec(memory_space=pl.ANY))
        args += (prev,)
    return pl.pallas_call(
        kern, grid=(b, t // tq, nk), in_specs=in_specs,
        out_specs=pl.BlockSpec((tq, aw), qrow),
        out_shape=jax.ShapeDtypeStruct((n_out, aw), F32),
        scratch_shapes=[
            pltpu.VMEM((nkv, grp * tq, HEAD_DIM + LANES), BF16),
            pltpu.VMEM((nkv, grp * tq, LANES), F32),
            pltpu.VMEM((nkv, grp * tq, 2 * HEAD_DIM), F32),
        ],
        input_output_aliases=aliases,
        compiler_params=_cparams(("parallel", "parallel", "arbitrary")), name="attention")(*args)


def _sigmoid(x):
    return 0.5 * jnp.tanh(0.5 * x) + 0.5


def _gelu(x):
    return 0.5 * x * (1.0 + lax.erf(x * (0.5 ** 0.5)))


def _lru_kernel(x_ref, gl_ref, gb_ref, ga_ref, att_ref, c0_ref, h0_ref, cw_ref, cb_ref, wa_ref, ba_ref, wi_ref, bi_ref,
                lam_ref, o_ref, cn_ref, hn_ref, xp_ref, a_ref, b_ref, hs_ref, h_ref, *, tt, nblk, bw):
    t = pl.program_id(1)
    pad = SUBLANES

    @pl.when(t == 0)
    def _init():
        xp_ref[0:pad, :] = c0_ref[0]
        h_ref[...] = h0_ref[0]

    xp_ref[pad:pad + tt, :] = x_ref[...]
    cw = cw_ref[...]
    base = pad - (CONV_W - 1)
    xc = cb_ref[...] + xp_ref[base:base + tt, :] * cw[0:1]
    for jj in range(1, CONV_W):
        xc = xc + xp_ref[base + jj:base + jj + tt, :] * cw[jj:jj + 1]
    tail = xp_ref[tt:tt + pad, :]
    cn_ref[0] = tail
    xp_ref[0:pad, :] = tail

    xcb = xc.astype(BF16)
    ra, ri = [], []
    for n in range(nblk):
        xs = xcb[:, n * bw:(n + 1) * bw]
        ra.append(jnp.dot(xs, wa_ref[n], preferred_element_type=F32))
        ri.append(jnp.dot(xs, wi_ref[n], preferred_element_type=F32))
    r = _sigmoid(jnp.concatenate(ra, axis=1) + ba_ref[...])
    ig = _sigmoid(jnp.concatenate(ri, axis=1) + bi_ref[...])
    z = -lam_ref[...]
    softplus = jnp.maximum(z, 0.0) + jnp.log1p(jnp.exp(-jnp.abs(z)))
    log_a = (-LRU_C * r) * softplus
    a = jnp.exp(log_a)
    neg_expm1 = -jnp.tanh(log_a) * (a * a + 1.0)
    a_ref[...] = a
    b_ref[...] = jnp.sqrt(neg_expm1) * (ig * xc)

    def step(s, h):
        h = a_ref[pl.ds(s, 1), :] * h + b_ref[pl.ds(s, 1), :]
        hs_ref[pl.ds(s, 1), :] = h
        return h

    h = lax.fori_loop(0, tt, step, h_ref[...], unroll=8)
    h_ref[...] = h
    hn_ref[0] = h
    lru = hs_ref[...] * _gelu(gl_ref[...])
    o_ref[...] = (_sigmoid(ga_ref[...]) * att_ref[...] + _sigmoid(gb_ref[...]) * lru).astype(o_ref.dtype)


def _lru_kernel_aliased(*refs, **kw):
    _lru_kernel(*refs[:14], *refs[15:], **kw)


def _lru(y, row_off, b, t, off_x, off_gl, off_ga, off_gb, att, conv0, h0, conv_w, conv_b, wa, ba, wi, bi, lam,
         prev=None, *, tt):
    c = conv_w.shape[1]
    nblk, bw = wa.shape[0], wa.shape[1]
    assert row_off % tt == 0 and t % tt == 0 and all(o % c == 0 for o in (off_x, off_gl, off_ga, off_gb))
    rows = lambda bb, i: row_off // tt + bb * (t // tt) + i
    col = lambda off: pl.BlockSpec((tt, c), lambda bb, i: (rows(bb, i), off // c))
    vec = pl.BlockSpec((1, c), lambda bb, i: (0, 0))
    wsp = pl.BlockSpec((nblk, bw, bw), lambda bb, i: (0, 0, 0))
    in_specs = [
        col(off_x), col(off_gl), col(off_gb), col(off_ga), col(0),
        pl.BlockSpec((1, SUBLANES, c), lambda bb, i: (bb, 0, 0)),
        pl.BlockSpec((1, 1, c), lambda bb, i: (bb, 0, 0)),
        pl.BlockSpec((CONV_W, c), lambda bb, i: (0, 0)),
        vec, wsp, vec, wsp, vec, vec,
    ]
    args = (y, y, y, y, att, conv0, h0, conv_w, conv_b.reshape(1, c), wa, ba.reshape(1, c), wi, bi.reshape(1, c),
            lam.reshape(1, c))
    kw = dict(tt=tt, nblk=nblk, bw=bw)
    if prev is None:
        kern, aliases = functools.partial(_lru_kernel, **kw), {}
    else:
        kern, aliases = functools.partial(_lru_kernel_aliased, **kw), {14: 0}
        in_specs.append(pl.BlockSpec(memory_space=pl.ANY))
        args += (prev,)
    return pl.pallas_call(
        kern, grid=(b, t // tt), in_specs=in_specs,
        out_specs=[
            pl.BlockSpec((tt, c), lambda bb, i: (rows(bb, i), 0)),
            pl.BlockSpec((1, SUBLANES, c), lambda bb, i: (bb, 0, 0)),
            pl.BlockSpec((1, 1, c), lambda bb, i: (bb, 0, 0)),
        ],
        out_shape=[
            jax.ShapeDtypeStruct(att.shape, BF16),
            jax.ShapeDtypeStruct((b, SUBLANES, c), F32),
            jax.ShapeDtypeStruct((b, 1, c), F32),
        ],
        scratch_shapes=[
            pltpu.VMEM((tt + SUBLANES, c), F32),
            pltpu.VMEM((tt, c), F32),
            pltpu.VMEM((tt, c), F32),
            pltpu.VMEM((tt, c), F32),
            pltpu.VMEM((1, c), F32),
        ],
        input_output_aliases=aliases,
        compiler_params=_cparams(("arbitrary", "arbitrary")), name="rg_lru")(*args)


def _mem_attn_kernel(q_ref, mk_ref, mv_ref, o_ref, *, heads, hd):
    scale = hd ** -0.5
    nt = (((1,), (1,)), ((), ()))
    for h in range(heads):
        sl = slice(h * hd, (h + 1) * hd)
        s = lax.dot_general(q_ref[:, sl].astype(BF16), mk_ref[0, :, sl], nt, preferred_element_type=F32) * scale
        s = s - jnp.max(s, axis=1, keepdims=True)
        e = jnp.exp(s)
        p = e / jnp.sum(e, axis=1, keepdims=True)
        o_ref[:, sl] = jnp.dot(p.astype(BF16), mv_ref[0, :, sl], preferred_element_type=F32).astype(o_ref.dtype)


def _mem_attn_kernel_aliased(q_ref, mk_ref, mv_ref, prev_ref, o_ref, **kw):
    del prev_ref
    _mem_attn_kernel(q_ref, mk_ref, mv_ref, o_ref, **kw)


def _mem_attention(q, row_off, t, mk, mv, prev=None, *, tm):
    b, nm, w = mk.shape
    assert row_off % tm == 0 and t % tm == 0
    kw = dict(heads=MEM_HEADS, hd=w // MEM_HEADS)
    qrow = lambda bb, i: (row_off // tm + bb * (t // tm) + i, 0)
    in_specs = [
        pl.BlockSpec((tm, w), qrow),
        pl.BlockSpec((1, nm, w), lambda bb, i: (bb, 0, 0)),
        pl.BlockSpec((1, nm, w), lambda bb, i: (bb, 0, 0)),
    ]
    args = (q, mk, mv)
    if prev is None:
        kern, aliases = functools.partial(_mem_attn_kernel, **kw), {}
    else:
        kern, aliases = functools.partial(_mem_attn_kernel_aliased, **kw), {3: 0}
        in_specs.append(pl.BlockSpec(memory_space=pl.ANY))
        args += (prev,)
    return pl.pallas_call(
        kern, grid=(b, t // tm), in_specs=in_specs,
        out_specs=pl.BlockSpec((tm, w), qrow),
        out_shape=jax.ShapeDtypeStruct((q.shape[0], w), BF16),
        input_output_aliases=aliases,
        compiler_params=_cparams(("parallel", "parallel")), name="mem_attention")(*args)


def _top_values(x, k):
    vals = []
    for _ in range(k):
        m = jnp.max(x, axis=0, keepdims=True)
        vals.append(m)
        x = jnp.where(x == m, -jnp.inf, x)
    return vals


def _peer_route_kernel(q_ref, sub_ref, ta_ref, sb_ref, ca_ref, eb_ref, *, heads):
    nt = (((1,), (1,)), ((), ()))
    k = PEER_TOPK
    for h in range(heads):
        st = []
        sv = []
        for c in range(2):
            col = (h * 2 + c) * PEER_HALF
            qh = q_ref[:, col:col + PEER_HALF].astype(BF16)
            s = lax.dot_general(sub_ref[h * 2 + c], qh, nt, preferred_element_type=F32)
            st.append(s)
            sv.append(_top_values(s, k + 1))
        rows = [sv[0][a] + sv[1][b] for a in range(k + 1) for b in range((k + 1) // (a + 1))]
        rows += [jnp.full_like(rows[0], -jnp.inf)] * (-len(rows) % SUBLANES)
        tv = _top_values(jnp.concatenate(rows, axis=0), k + 1)
        z = jnp.zeros_like(tv[0])
        for v in tv[:k]:
            z = z + jnp.exp(v - tv[0])
        ta_ref[h] = 0.5 * (tv[k - 1] + tv[k]) - st[0]
        sb_ref[h] = st[1]
        ca_ref[h] = jnp.exp(st[0] - sv[0][0]) / z
        eb_ref[h] = jnp.exp(st[1] - sv[1][0])


def _peer_route(qp, sub, *, tn):
    n = qp.shape[0]
    heads = PEER_HEADS
    kern = functools.partial(_peer_route_kernel, heads=heads)
    big = pl.BlockSpec((heads, PEER_NKEYS, tn), lambda i: (0, 0, i))
    big_shape = jax.ShapeDtypeStruct((heads, PEER_NKEYS, n), F32)
    return pl.pallas_call(
        kern, grid=(n // tn,),
        in_specs=[
            pl.BlockSpec((tn, qp.shape[1]), lambda i: (i, 0)),
            pl.BlockSpec(sub.shape, lambda i: (0, 0, 0)),
        ],
        out_specs=[big, big, big, big],
        out_shape=[big_shape, big_shape, big_shape, big_shape],
        compiler_params=_cparams(("parallel",)), name="peer_route")(qp, sub)


def _peer_dense_kernel(x_ref, u_ref, v_ref, ta_ref, ca_ref, sb_ref, eb_ref, o_ref, *, heads, rows):
    j = pl.program_id(1)

    @pl.when(j == 0)
    def _init():
        o_ref[...] = jnp.zeros(o_ref.shape, F32)

    act = _gelu(jnp.dot(x_ref[...], u_ref[...], preferred_element_type=F32))
    tiles = []
    for r in range(rows):
        w = None
        for h in range(heads):
            term = jnp.where(sb_ref[h] >= ta_ref[r, h:h + 1, :], eb_ref[h], 0.0) * ca_ref[r, h:h + 1, :]
            w = term if w is None else w + term
        tiles.append(w)
    gate = jnp.concatenate(tiles, axis=0) if rows > 1 else tiles[0]
    coef = (gate.T * act).astype(BF16)
    o_ref[...] += jnp.dot(coef, v_ref[...], preferred_element_type=F32)


def _peer_dense(xn, u, v, ta, ca, sb, eb, *, tn, te):
    n, d = xn.shape
    e = v.shape[0]
    heads = sb.shape[0]
    rows = te // PEER_NKEYS
    kern = functools.partial(_peer_dense_kernel, heads=heads, rows=rows)
    once = pl.Buffered(1)
    row_blk = pl.BlockSpec((rows, heads, tn), lambda i, j: (j, 0, i))
    big = pl.BlockSpec((heads, PEER_NKEYS, tn), lambda i, j: (0, 0, i), pipeline_mode=once)
    return pl.pallas_call(
        kern, grid=(n // tn, e // te),
        in_specs=[
            pl.BlockSpec((tn, d), lambda i, j: (i, 0), pipeline_mode=once),
            pl.BlockSpec((d, te), lambda i, j: (0, j)),
            pl.BlockSpec((te, d), lambda i, j: (j, 0)),
            row_blk, row_blk, big, big,
        ],
        out_specs=pl.BlockSpec((tn, d), lambda i, j: (i, 0)),
        out_shape=jax.ShapeDtypeStruct((n, d), F32),
        compiler_params=_cparams(("parallel", "arbitrary")), name="peer_dense")(xn, u, v, ta, ca, sb, eb)


def _pad_rows(x, n):
    if x.shape[0] == n:
        return x
    return jnp.pad(x, ((0, n - x.shape[0]),) + ((0, 0),) * (x.ndim - 1))


def _index_keys(ki, s_pad, tk):
    b = ki.shape[0]
    kt = jnp.swapaxes(jnp.pad(ki.astype(BF16), ((0, 0), (0, s_pad - ki.shape[1]), (0, 0))), 1, 2)
    tiles = lambda x: jnp.swapaxes(x.reshape(b, LANES, s_pad // tk, tk), 1, 2)
    lo = tiles(jnp.pad(kt, ((0, 0), (0, LANES - IDX_DIM), (0, 0))))
    hi = tiles(jnp.pad(kt, ((0, 0), (LANES - IDX_DIM, 0), (0, 0))))
    return lo, hi


def kernel(x_prompt, x_sample, mem_prompt, cache_k, cache_v, cache_kidx, state_conv, state_lru, cache_mem_k, cache_mem_v, norm_mix_g, w_in, conv_w, conv_b, lru_wa, lru_ba, lru_wi, lru_bi, lru_lambda, w_out, norm_mem_g, norm_memkv_g, mem_wq, mem_wk, mem_wv, mem_wo, norm_ffn_g, peer_wq, peer_subkeys, peer_u, peer_v, norm_final_g):
    assert w_in.shape[0] == 1, "single-layer step"
    assert IDX_DIM * 2 == LANES and PEER_HALF == LANES and PEER_NKEYS == LANES and HEAD_DIM == LANES
    bp, tp, d = x_prompt.shape
    bs, ts, _ = x_sample.shape
    past = cache_k.shape[2]
    n_p, n_s = bp * tp, bs * ts
    n_tok = n_p + n_s
    n_pad = -(-n_tok // TOKEN_TILE) * TOKEN_TILE
    kvw = N_KV_HEADS * HEAD_DIM
    iqw = IDX_HEADS * IDX_DIM
    n_mem = mem_prompt.shape[1]
    mem_w = mem_wq.shape[2]

    x0 = _pad_rows(jnp.concatenate([x_prompt.reshape(n_p, d), x_sample.reshape(n_s, d)], axis=0), n_pad)
    pos = _pad_rows(jnp.concatenate([jnp.tile(jnp.arange(tp, dtype=jnp.int32), bp),
                                     jnp.tile(past + jnp.arange(ts, dtype=jnp.int32), bs)]), n_pad)

    o_k, o_v, o_qi, o_ki = d, d + kvw, d + 2 * kvw, d + 2 * kvw + iqw
    o_xl = o_ki + IDX_DIM + IDX_HEADS
    assert o_xl + 4 * d == w_in.shape[2]
    y_xl, y_gl, y_ga, y_gb = 0, d, 2 * d, 3 * d

    xn = _rmsnorm(x0, norm_mix_g[0], BF16)
    wt = jnp.swapaxes(w_in, 1, 2)[0]
    ya = _matmul_nt(xn, wt, 0, o_ki, "in_proj_qkv")
    yb = _matmul_nt(xn, wt, o_xl, 4 * d, "in_proj_lru")
    yc = _matmul_nt(xn, wt, o_ki, LANES, "in_proj_idx")

    tabs1 = _rope_tables(pos, HEAD_DIM)
    tabs2 = _rope_tables(pos, IDX_DIM)
    q_bf, k_f, k_t, v_bf, qi_bf, ki_f, wi = _prep(ya, yc, tabs1, tabs2, d, kvw, iqw, o_k, o_v, o_qi)
    v_f = ya[:n_tok, o_v:o_v + kvw]

    n_sel_p = min(TOPK_MAX, tp // 4)
    tk_p = _tile(tp, 512, LANES)
    klo, khi = _index_keys(ki_f[:n_p].reshape(bp, tp, IDX_DIM), tp, tk_p)
    bias_p = _idx_mask(qi_bf, wi, 0, tp, klo, khi, tq=_tile(tp, 128, 16), tk=tk_p, q_off=0, s_valid=tp,
                       n_sel=n_sel_p)
    att = _attention(q_bf, 0, tp, k_t, v_bf, tp, bias_p, n_pad,
                     tq=_tile(tp, 128, 16), tk=_tile(tp, 1024, tk_p), q_off=0, s_valid=tp)

    s_s = past + ts
    s_pad = -(-s_s // LANES) * LANES
    n_sel_s = min(TOPK_MAX, s_s // 4)
    ki_s = jnp.concatenate([cache_kidx[0], ki_f[n_p:n_tok].reshape(bs, ts, IDX_DIM)], axis=1)
    klo_s, khi_s = _index_keys(ki_s, s_pad, s_pad)
    bias_s = _idx_mask(qi_bf, wi, n_p, ts, klo_s, khi_s, tq=ts, tk=s_pad, q_off=past, s_valid=s_s, n_sel=n_sel_s)

    v_s = jnp.concatenate([cache_v[0].reshape(bs, past, kvw).astype(BF16), v_bf[n_p:n_tok].reshape(bs, ts, kvw)],
                          axis=1)
    v_s = jnp.pad(v_s, ((0, 0), (0, s_pad - s_s), (0, 0))).reshape(bs * s_pad, kvw)
    kt_s = jnp.concatenate([jnp.transpose(cache_k[0].reshape(bs, past, kvw).astype(BF16), (2, 0, 1)),
                            k_t[:, n_p:n_tok].reshape(kvw, bs, ts)], axis=2)
    kt_s = jnp.pad(kt_s, ((0, 0), (0, 0), (0, s_pad - s_s))).reshape(kvw, bs * s_pad)
    att = _attention(q_bf, n_p, ts, kt_s, v_s, s_pad, bias_s, n_pad, prev=att,
                     tq=ts, tk=s_pad, q_off=past, s_valid=s_s)

    wa_bf, wi_bf = lru_wa[0].astype(BF16), lru_wi[0].astype(BF16)
    lru_args = (conv_w[0], conv_b[0], wa_bf, lru_ba[0], wi_bf, lru_bi[0], lru_lambda[0])
    state_pad = SUBLANES - (CONV_W - 1)
    merged, cn_p, hn_p = _lru(yb, 0, bp, tp, y_xl, y_gl, y_ga, y_gb, att, jnp.zeros((bp, SUBLANES, d), F32),
                              jnp.zeros((bp, 1, d), F32), *lru_args, tt=_tile(tp, 128, SUBLANES))
    conv0_s = jnp.pad(state_conv[0], ((0, 0), (state_pad, 0), (0, 0)))
    merged, cn_s, hn_s = _lru(yb, n_p, bs, ts, y_xl, y_gl, y_ga, y_gb, att, conv0_s, state_lru[0].reshape(bs, 1, d),
                              *lru_args, prev=merged, tt=ts)
    if n_pad > n_tok:
        merged = merged.at[n_tok:].set(0.0)

    x1 = _matmul(merged, w_out[0].astype(BF16), res=x0, name="out_proj")

    mem_n = _rmsnorm(mem_prompt.reshape(bp * n_mem, d), norm_memkv_g[0], BF16)
    mkv = _matmul(mem_n, jnp.concatenate([mem_wk[0], mem_wv[0]], axis=1), name="mem_kv")
    mk_p = mkv[:, :mem_w].reshape(bp, n_mem, mem_w)
    mv_p = mkv[:, mem_w:].reshape(bp, n_mem, mem_w)
    xn2 = _rmsnorm(x1, norm_mem_g[0], BF16)
    qm = _matmul(xn2, mem_wq[0].astype(BF16), name="mem_q")
    om = _mem_attention(qm, 0, tp, mk_p.astype(BF16), mv_p.astype(BF16), tm=_tile(tp, 512, 16))
    om = _mem_attention(qm, n_p, ts, cache_mem_k[0].reshape(bs, n_mem, mem_w).astype(BF16),
                        cache_mem_v[0].reshape(bs, n_mem, mem_w).astype(BF16), prev=om, tm=ts)
    if n_pad > n_tok:
        om = om.at[n_tok:].set(0.0)
    x2 = _matmul(om, mem_wo[0].astype(BF16), res=x1, name="mem_out")

    xn3 = _rmsnorm(x2, norm_ffn_g[0], BF16)
    qp = _matmul(xn3, peer_wq[0].astype(BF16), name="peer_q")
    sub = peer_subkeys[0].reshape(PEER_HEADS * 2, PEER_NKEYS, PEER_HALF).astype(BF16)
    ta, sb, ca, eb = _peer_route(qp, sub, tn=LANES)
    peer = _peer_dense(xn3, peer_u[0].T.astype(BF16), peer_v[0].astype(BF16),
                       jnp.transpose(ta, (1, 0, 2)), jnp.transpose(ca, (1, 0, 2)), sb, eb,
                       tn=_tile(n_pad, 640, LANES), te=512)

    y_p, y_s = _final_norm(x2, peer, norm_final_g, n_p, n_s)

    kv_shape_p = (1, bp, tp, N_KV_HEADS, HEAD_DIM)
    kv_shape_s = (1, bs, ts, N_KV_HEADS, HEAD_DIM)
    tail = slice(SUBLANES - (CONV_W - 1), SUBLANES)
    return (
        y_p.reshape(bp, tp, d), y_s.reshape(bs, ts, d),
        k_f[:n_p].reshape(kv_shape_p), v_f[:n_p].reshape(kv_shape_p), ki_f[:n_p].reshape(1, bp, tp, IDX_DIM),
        cn_p[:, tail][None], hn_p.reshape(1, bp, d),
        mk_p.reshape(1, bp, n_mem, MEM_HEADS, mem_w // MEM_HEADS),
        mv_p.reshape(1, bp, n_mem, MEM_HEADS, mem_w // MEM_HEADS),
        k_f[n_p:n_tok].reshape(kv_shape_s), v_f[n_p:].reshape(kv_shape_s),
        ki_f[n_p:n_tok].reshape(1, bs, ts, IDX_DIM),
        cn_s[:, tail][None], hn_s.reshape(1, bs, d),
    )
```

```python
import functools

import jax
import jax.numpy as jnp
from jax import lax
from jax.experimental import pallas as pl
from jax.experimental.pallas import tpu as pltpu

F32 = jnp.float32
BF16 = jnp.bfloat16

CHUNK = 64
HEAD_DIM = 128
N_KV_HEADS = 8
IDX_HEADS = 16
IDX_DIM = 64
TOPK_MAX = 256
ROPE_THETA = 500000.0
LRU_BLOCKS = 16
CONV_W = 4
LRU_C = 8.0
MEM_HEADS = 4
PEER_HEADS = 8
PEER_NKEYS = 128
PEER_HALF = 128
PEER_TOPK = 16
EPS = 1e-6

LANES = 128
SUBLANES = 8
TOKEN_TILE = 128
VMEM_LIMIT = 56 * 1024 * 1024
NEG = -1e30
QK_SCALE_LOG2E = (HEAD_DIM ** -0.5) * 1.4426950408889634
INT_MIN = -(2 ** 31)


def _cparams(sem, vmem=VMEM_LIMIT):
    return pltpu.CompilerParams(dimension_semantics=sem, vmem_limit_bytes=vmem)


def _tile(n, pref, mult):
    if n <= pref:
        return n
    t = (pref // mult) * mult
    while t > mult and n % t:
        t -= mult
    assert n % t == 0, (n, pref, mult)
    return t


def _rmsnorm_kernel(x_ref, g_ref, o_ref):
    x = x_ref[...]
    r = lax.rsqrt(jnp.mean(x * x, axis=-1, keepdims=True) + EPS)
    o_ref[...] = ((x * r) * g_ref[...]).astype(o_ref.dtype)


def _final_norm_kernel(x_ref, y_ref, g_ref, op_ref, os_ref, *, first_blocks):
    x = x_ref[...] + y_ref[...]
    r = lax.rsqrt(jnp.mean(x * x, axis=-1, keepdims=True) + EPS)
    val = (x * r) * g_ref[...]

    @pl.when(pl.program_id(0) < first_blocks)
    def _():
        op_ref[...] = val

    @pl.when(pl.program_id(0) >= first_blocks)
    def _():
        os_ref[...] = val


def _final_norm(x, y, g, n_first, n_second):
    d = x.shape[1]
    tm = _tile(n_second, 128, SUBLANES)
    assert n_first % tm == 0
    fb = n_first // tm
    row = pl.BlockSpec((tm, d), lambda i: (i, 0))
    return pl.pallas_call(
        functools.partial(_final_norm_kernel, first_blocks=fb), grid=(fb + n_second // tm,),
        in_specs=[row, row, pl.BlockSpec((1, d), lambda i: (0, 0))],
        out_specs=[pl.BlockSpec((tm, d), lambda i: (jnp.minimum(i, fb - 1), 0)),
                   pl.BlockSpec((tm, d), lambda i: (jnp.maximum(i - fb, 0), 0))],
        out_shape=[jax.ShapeDtypeStruct((n_first, d), F32), jax.ShapeDtypeStruct((n_second, d), F32)],
        compiler_params=_cparams(("arbitrary",)), name="final_norm")(x, y, g.reshape(1, d))


def _rmsnorm(x, g, out_dtype):
    m, d = x.shape
    tm = _tile(m, 640, SUBLANES)
    row = pl.BlockSpec((tm, d), lambda i: (i, 0))
    return pl.pallas_call(
        _rmsnorm_kernel, grid=(m // tm,), in_specs=[row, pl.BlockSpec((1, d), lambda i: (0, 0))], out_specs=row,
        out_shape=jax.ShapeDtypeStruct((m, d), out_dtype),
        compiler_params=_cparams(("parallel",)), name="rmsnorm")(x, g.reshape(1, d))


def _mm_kernel(a_ref, b_ref, *rest, has_res, cast_b):
    rest = list(rest)
    r_ref = rest.pop(0) if has_res else None
    o_ref = rest.pop(0)
    if cast_b:
        bb_ref = rest.pop(0)

        @pl.when(pl.program_id(1) == 0)
        def _():
            bb_ref[...] = b_ref[...].astype(BF16)

        b = bb_ref[...]
    else:
        b = b_ref[...]
    acc = jnp.dot(a_ref[...], b, preferred_element_type=F32)
    if has_res:
        acc = r_ref[...] + acc
    o_ref[...] = acc.astype(o_ref.dtype)


def _matmul(a, b, res=None, n_cols=None, tm_pref=640, out_dtype=F32, name="matmul"):
    m, k = a.shape
    n = b.shape[-1] if n_cols is None else n_cols
    cast_b = b.dtype != BF16
    tm = _tile(m, tm_pref, 16)
    tn = _tile(n, 512 if cast_b else 1024, 2 * LANES if n % (2 * LANES) == 0 else LANES)
    a_spec = pl.BlockSpec((tm, k), lambda j, i: (i, 0))
    b_spec = (pl.BlockSpec((k, tn), lambda j, i: (0, j)) if b.ndim == 2 else
              pl.BlockSpec((None, k, tn), lambda j, i: (0, 0, j)))
    o_spec = pl.BlockSpec((tm, tn), lambda j, i: (i, j))
    specs, args = [a_spec, b_spec], [a, b]
    if res is not None:
        specs.append(o_spec)
        args.append(res)
    return pl.pallas_call(
        functools.partial(_mm_kernel, has_res=res is not None, cast_b=cast_b),
        grid=(n // tn, m // tm), in_specs=specs, out_specs=o_spec,
        out_shape=jax.ShapeDtypeStruct((m, n), out_dtype),
        scratch_shapes=[pltpu.VMEM((k, tn), BF16)] if cast_b else [],
        compiler_params=_cparams(("parallel", "arbitrary" if cast_b else "parallel")), name=name)(*args)


def _mm_nt_kernel(a_ref, bt_ref, o_ref, bb_ref):
    @pl.when(pl.program_id(1) == 0)
    def _():
        bb_ref[...] = bt_ref[...].astype(BF16)

    nt = (((1,), (1,)), ((), ()))
    o_ref[...] = lax.dot_general(a_ref[...], bb_ref[...], nt, preferred_element_type=F32)


def _matmul_nt(a, bt, row_start, n, name):
    m, k = a.shape
    tm = _tile(m, 640, 16)
    tn = _tile(n, 512, LANES)
    assert row_start % SUBLANES == 0 and row_start + n <= bt.shape[0], (row_start, n, bt.shape)
    return pl.pallas_call(
        _mm_nt_kernel, grid=(n // tn, m // tm),
        in_specs=[pl.BlockSpec((tm, k), lambda j, i: (i, 0)),
                  pl.BlockSpec((pl.Element(tn), pl.Element(k)),
                               lambda j, i: (pl.multiple_of(row_start + j * tn, SUBLANES), 0))],
        out_specs=pl.BlockSpec((tm, tn), lambda j, i: (i, j)),
        out_shape=jax.ShapeDtypeStruct((m, n), F32),
        scratch_shapes=[pltpu.VMEM((tn, k), BF16)],
        compiler_params=_cparams(("parallel", "arbitrary")), name=name)(a, bt)


def _rope_tables(pos, head_dim):
    rot = head_dim // 4
    half = rot // 2
    inv = ROPE_THETA ** (-jnp.arange(half, dtype=F32) / half)
    ang = pos.astype(F32)[:, None] * inv[None, :]
    cos, sin = jnp.cos(ang), jnp.sin(ang)
    n = pos.shape[0]
    ones = jnp.ones((n, head_dim - rot), F32)
    zr = jnp.zeros((n, head_dim - rot), F32)
    zh = jnp.zeros((n, half), F32)
    c = jnp.concatenate([cos, cos, ones], axis=1)
    sa = jnp.concatenate([-sin, zh, zr], axis=1)
    sb = jnp.concatenate([zh, sin, zr], axis=1)
    reps = LANES // head_dim
    return tuple(jnp.tile(t, (1, reps)) for t in (c, sa, sb))


def _rope(x, c, sa, sb, half):
    return x * c + pltpu.roll(x, LANES - half, 1) * sa + pltpu.roll(x, half, 1) * sb


def _prep_kernel(q_ref, k_ref, v_ref, qi_ref, kw_ref, c1_ref, sa1_ref, sb1_ref, c2_ref, sa2_ref, sb2_ref,
                 qo_ref, kf_ref, kt_ref, vb_ref, qio_ref, kif_ref, wi_ref):
    c1, sa1, sb1 = c1_ref[...], sa1_ref[...], sb1_ref[...]
    c2, sa2, sb2 = c2_ref[...], sa2_ref[...], sb2_ref[...]
    h1 = HEAD_DIM // 8
    h2 = IDX_DIM // 8
    for h in range(q_ref.shape[1] // LANES):
        sl = slice(h * LANES, (h + 1) * LANES)
        qo_ref[:, sl] = (_rope(q_ref[:, sl], c1, sa1, sb1, h1) * QK_SCALE_LOG2E).astype(BF16)
    for h in range(k_ref.shape[1] // LANES):
        sl = slice(h * LANES, (h + 1) * LANES)
        kr = _rope(k_ref[:, sl], c1, sa1, sb1, h1)
        kf_ref[:, sl] = kr
        kt_ref[sl, :] = kr.T.astype(BF16)
    vb_ref[...] = v_ref[...].astype(BF16)
    for h in range(qi_ref.shape[1] // LANES):
        sl = slice(h * LANES, (h + 1) * LANES)
        qio_ref[:, sl] = _rope(qi_ref[:, sl], c2, sa2, sb2, h2).astype(BF16)
    t = kw_ref[...]
    kif_ref[...] = _rope(t, c2, sa2, sb2, h2)[:, :IDX_DIM]
    wi_ref[...] = t[:, IDX_DIM:IDX_DIM + IDX_HEADS] * (IDX_HEADS ** -0.5)


def _prep(y, y_kw, tabs1, tabs2, d, kvw, iqw, off_k, off_v, off_qi):
    n = y.shape[0]
    tm = _tile(n, LANES, LANES)
    tab = pl.BlockSpec((tm, LANES), lambda i: (i, 0))
    in_specs = [
        pl.BlockSpec((tm, d), lambda i: (i, 0)),
        pl.BlockSpec((tm, kvw), lambda i: (i, off_k // kvw)),
        pl.BlockSpec((tm, kvw), lambda i: (i, off_v // kvw)),
        pl.BlockSpec((tm, iqw), lambda i: (i, off_qi // iqw)),
        pl.BlockSpec((tm, LANES), lambda i: (i, 0)),
        tab, tab, tab, tab, tab, tab,
    ]
    outs = [
        (d, BF16), (kvw, F32), None, (kvw, BF16), (iqw, BF16), (IDX_DIM, F32), (IDX_HEADS, F32),
    ]
    row_spec = lambda o: pl.BlockSpec((tm, o[0]), lambda i: (i, 0))
    row_shape = lambda o: jax.ShapeDtypeStruct((n, o[0]), o[1])
    return pl.pallas_call(
        _prep_kernel, grid=(n // tm,), in_specs=in_specs,
        out_specs=[row_spec(o) if o else pl.BlockSpec((kvw, tm), lambda i: (0, i)) for o in outs],
        out_shape=[row_shape(o) if o else jax.ShapeDtypeStruct((kvw, n), BF16) for o in outs],
        compiler_params=_cparams(("parallel",)), name="prep")(y, y, y, y, y_kw, *tabs1, *tabs2)


def _num_k_tiles(i, tq, tk, q_off, s_valid):
    last_chunk_end = ((q_off + i * tq + tq - 1) // CHUNK + 1) * CHUNK
    return (jnp.minimum(last_chunk_end, s_valid) + tk - 1) // tk


def _idx_kernel(qi_ref, wi_ref, klo_ref, khi_ref, o_ref, key_ref, *, tq, tk, nk, q_off, s_valid, n_sel):
    i = pl.program_id(1)
    q0 = q_off + i * tq
    nkt = _num_k_tiles(i, tq, tk, q_off, s_valid)
    wi = wi_ref[...] * (IDX_DIM ** -0.5)
    wcols = [wi[:, h:h + 1] for h in range(IDX_HEADS)]
    q_chunk = (q0 + lax.broadcasted_iota(jnp.int32, (tq, tk), 0)) // CHUNK

    def score_tile(j, carry):
        k0 = j * tk
        klo = klo_ref[0, j]
        khi = khi_ref[0, j]
        acc = jnp.zeros((tq, tk), F32)
        for p in range(IDX_HEADS // 2):
            qp = qi_ref[:, p * LANES:(p + 1) * LANES]
            d0 = jnp.dot(qp, klo, preferred_element_type=F32)
            d1 = jnp.dot(qp, khi, preferred_element_type=F32)
            acc = acc + jnp.maximum(d0, 0.0) * wcols[2 * p]
            acc = acc + jnp.maximum(d1, 0.0) * wcols[2 * p + 1]
        k_pos = k0 + lax.broadcasted_iota(jnp.int32, (tq, tk), 1)
        vis = jnp.logical_and(k_pos // CHUNK <= q_chunk, k_pos < s_valid)
        bits = pltpu.bitcast(acc, jnp.int32)
        key = bits ^ ((bits >> 31) & 0x7FFFFFFF)
        key_ref[j] = jnp.where(vis, key, INT_MIN)
        return carry

    lax.fori_loop(0, nkt, score_tile, 0)

    def count(pred):
        def count_tile(j, acc):
            m = jnp.where(pred(j, key_ref[j]), 1.0, 0.0)
            part = m[:, 0:LANES]
            for c in range(1, tk // LANES):
                part = part + m[:, c * LANES:(c + 1) * LANES]
            return acc + part

        acc = lax.fori_loop(0, nkt, count_tile, jnp.zeros((tq, LANES), F32))
        return jnp.sum(acc, axis=1, keepdims=True)

    def bit_step(it, carry):
        r, cnt_r = carry
        cand = r | jnp.left_shift(jnp.int32(1), 31 - it)
        cs = cand ^ INT_MIN
        cnt = count(lambda j, key: key >= cs)
        take = cnt >= n_sel
        return jnp.where(take, cand, r), jnp.where(take, cnt, cnt_r)

    r, cnt_ge = lax.fori_loop(0, 32, bit_step,
                              (jnp.zeros((tq, 1), jnp.int32), jnp.full((tq, 1), float(n_sel), F32)))
    thr = jnp.maximum(r ^ INT_MIN, INT_MIN + 1)
    has_ties = jnp.max(cnt_ge) > float(n_sel)

    @pl.when(jnp.logical_not(has_ties))
    def _write_plain():
        def write_tile(j, carry):
            o_ref[0, j] = jnp.where(key_ref[j] >= thr, 0.0, NEG).astype(BF16)
            return carry

        lax.fori_loop(0, nkt, write_tile, 0)

    @pl.when(has_ties)
    def _write_tie_broken():
        need = float(n_sel) - count(lambda j, key: key > thr)
        lane = lax.broadcasted_iota(jnp.int32, (tq, tk), 1)

        def pos_step(it, p):
            cand = p | jnp.left_shift(jnp.int32(1), pos_bits - 1 - it)
            cnt = count(lambda j, key: jnp.logical_and(key == thr, lane + j * tk < cand))
            return jnp.where(cnt < need, cand, p)

        pos_bits = max(1, (nk * tk - 1).bit_length())
        p = lax.fori_loop(0, pos_bits, pos_step, jnp.zeros((tq, 1), jnp.int32))

        def write_tile(j, carry):
            key = key_ref[j]
            sel = jnp.logical_or(key > thr, jnp.logical_and(key == thr, lane + j * tk <= p))
            o_ref[0, j] = jnp.where(sel, 0.0, NEG).astype(BF16)
            return carry

        lax.fori_loop(0, nkt, write_tile, 0)

    def fill_tile(j, carry):
        o_ref[0, j] = jnp.full((tq, tk), NEG, BF16)
        return carry

    lax.fori_loop(nkt, nk, fill_tile, 0)


def _idx_mask(qi, wi, row_off, t, klo, khi, *, tq, tk, q_off, s_valid, n_sel):
    b, nk = klo.shape[:2]
    assert klo.shape[3] == tk
    iqw = qi.shape[1]
    assert row_off % tq == 0 and t % tq == 0
    qrow = lambda bb, i: (row_off // tq + bb * (t // tq) + i, 0)
    kern = functools.partial(_idx_kernel, tq=tq, tk=tk, nk=nk, q_off=q_off, s_valid=s_valid, n_sel=n_sel)
    return pl.pallas_call(
        kern, grid=(b, t // tq),
        in_specs=[
            pl.BlockSpec((tq, iqw), qrow),
            pl.BlockSpec((tq, IDX_HEADS), qrow),
            pl.BlockSpec((1, nk, LANES, tk), lambda bb, i: (bb, 0, 0, 0)),
            pl.BlockSpec((1, nk, LANES, tk), lambda bb, i: (bb, 0, 0, 0)),
        ],
        out_specs=pl.BlockSpec((1, nk, tq, tk), lambda bb, i: (bb, 0, i, 0)),
        out_shape=jax.ShapeDtypeStruct((b, nk, t, tk), BF16),
        scratch_shapes=[pltpu.VMEM((nk, tq, tk), jnp.int32)],
        compiler_params=_cparams(("parallel", "parallel")), name="idx_mask")(qi, wi, klo, khi)


def _attn_kernel(q_ref, k_ref, v_ref, b_ref, o_ref, qs_ref, m_ref, acc_ref,
                 *, tq, tk, nk, q_off, s_valid, nkv, grp):
    i = pl.program_id(1)
    j = pl.program_id(2)
    nkt = _num_k_tiles(i, tq, tk, q_off, s_valid)

    @pl.when(j == 0)
    def _init():
        eye = (lax.broadcasted_iota(jnp.int32, (tq, LANES), 0) ==
               lax.broadcasted_iota(jnp.int32, (tq, LANES), 1)).astype(F32).astype(BF16)
        for g in range(nkv):
            for hh in range(grp):
                h = g * grp + hh
                qs_ref[g, hh * tq:(hh + 1) * tq, :HEAD_DIM] = q_ref[:, h * HEAD_DIM:(h + 1) * HEAD_DIM]
                qs_ref[g, hh * tq:(hh + 1) * tq, HEAD_DIM:] = eye
        m_ref[...] = jnp.full(m_ref.shape, NEG, F32)
        acc_ref[...] = jnp.zeros(acc_ref.shape, F32)

    @pl.when(j < nkt)
    def _compute():
        mask = jnp.concatenate([b_ref[0, c] for c in range(b_ref.shape[1])], axis=1)
        if tq < LANES:
            mask = jnp.concatenate([mask, jnp.zeros((LANES - tq, tk), BF16)], axis=0)
        nc = tk // LANES
        ones = jnp.ones((tk, LANES), BF16)
        chunks = lambda a: [a[:, c * LANES:(c + 1) * LANES] for c in range(nc)]
        for g in range(nkv):
            kg = jnp.concatenate([k_ref[g * HEAD_DIM:(g + 1) * HEAD_DIM, :], mask], axis=0)
            vg = jnp.concatenate([v_ref[:, g * HEAD_DIM:(g + 1) * HEAD_DIM], ones], axis=1)
            s = jnp.dot(qs_ref[g], kg, preferred_element_type=F32)
            m_prev = m_ref[g]
            m_new = jnp.maximum(m_prev, jnp.max(functools.reduce(jnp.maximum, chunks(s)), axis=1, keepdims=True))
            p = jnp.exp2(s - jnp.tile(m_new, (1, nc)))
            alpha = jnp.exp2(m_prev - m_new)
            acc_ref[g] = jnp.tile(alpha, (1, 2)) * acc_ref[g] + jnp.dot(p.astype(BF16), vg, preferred_element_type=F32)
            m_ref[g] = m_new

    @pl.when(j == nk - 1)
    def _finish():
        for g in range(nkv):
            for hh in range(grp):
                h = g * grp + hh
                rows = slice(hh * tq, (hh + 1) * tq)
                o_ref[:, h * HEAD_DIM:(h + 1) * HEAD_DIM] = acc_ref[g, rows, :HEAD_DIM] / acc_ref[g, rows, HEAD_DIM:]


def _attn_kernel_aliased(q_ref, k_ref, v_ref, b_ref, prev_ref, o_ref, *scratch, **kw):
    del prev_ref
    _attn_kernel(q_ref, k_ref, v_ref, b_ref, o_ref, *scratch, **kw)


def _attention(q, row_off, t, k, v, s_pad, bias, prev, *, tq, tk, q_off, s_valid):
    b = bias.shape[0]
    aw, kvw = q.shape[1], v.shape[1]
    nk = s_pad // tk
    nkv = kvw // HEAD_DIM
    grp = aw // kvw
    tkb = bias.shape[3]
    assert row_off % tq == 0 and t % tq == 0 and tq <= LANES and tk % tkb == 0
    kw = dict(tq=tq, tk=tk, nk=nk, q_off=q_off, s_valid=s_valid, nkv=nkv, grp=grp)

    def kj(i, j):
        return jnp.minimum(j, _num_k_tiles(i, tq, tk, q_off, s_valid) - 1)

    qrow = lambda bb, i, j: (row_off // tq + bb * (t // tq) + i, 0)
    krow = lambda bb, i, j: (bb * nk + kj(i, j), 0)
    in_specs = [
        pl.BlockSpec((tq, aw), qrow),
        pl.BlockSpec((kvw, tk), lambda bb, i, j: (0, bb * nk + kj(i, j))),
        pl.BlockSpec((tk, kvw), krow),
        pl.BlockSpec((1, tk // tkb, tq, tkb), lambda bb, i, j: (bb, kj(i, j), i, 0)),
        pl.BlockSpec(memory_space=pl.ANY),
    ]
    return pl.pallas_call(
        functools.partial(_attn_kernel_aliased, **kw), grid=(b, t // tq, nk), in_specs=in_specs,
        out_specs=pl.BlockSpec((tq, aw), qrow),
        out_shape=jax.ShapeDtypeStruct(prev.shape, F32),
        scratch_shapes=[
            pltpu.VMEM((nkv, grp * tq, HEAD_DIM + LANES), BF16),
            pltpu.VMEM((nkv, grp * tq, LANES), F32),
            pltpu.VMEM((nkv, grp * tq, 2 * HEAD_DIM), F32),
        ],
        input_output_aliases={4: 0},
        compiler_params=_cparams(("parallel", "parallel", "arbitrary")), name="attention")(q, k, v, bias, prev)


def _sigmoid(x):
    return 0.5 * jnp.tanh(0.5 * x) + 0.5


def _gelu(x):
    return 0.5 * x * (1.0 + lax.erf(x * (0.5 ** 0.5)))


def _lru_kernel(x_ref, gl_ref, gb_ref, ga_ref, att_ref, c0_ref, h0_ref, cw_ref, cb_ref, wa_ref, ba_ref, wi_ref, bi_ref,
                lam_ref, o_ref, cn_ref, hn_ref, xp_ref, a_ref, b_ref, hs_ref, h_ref, *, tt, nblk, bw):
    t = pl.program_id(1)
    pad = SUBLANES

    @pl.when(t == 0)
    def _init():
        xp_ref[0:pad, :] = c0_ref[0]
        h_ref[...] = h0_ref[0]

    xp_ref[pad:pad + tt, :] = x_ref[...]
    cw = cw_ref[...]
    base = pad - (CONV_W - 1)
    xc = cb_ref[...] + xp_ref[base:base + tt, :] * cw[0:1]
    for jj in range(1, CONV_W):
        xc = xc + xp_ref[base + jj:base + jj + tt, :] * cw[jj:jj + 1]
    tail = xp_ref[tt:tt + pad, :]
    cn_ref[0] = tail
    xp_ref[0:pad, :] = tail

    xcb = xc.astype(BF16)
    ra, ri = [], []
    for n in range(nblk):
        xs = xcb[:, n * bw:(n + 1) * bw]
        ra.append(jnp.dot(xs, wa_ref[n], preferred_element_type=F32))
        ri.append(jnp.dot(xs, wi_ref[n], preferred_element_type=F32))
    r = _sigmoid(jnp.concatenate(ra, axis=1) + ba_ref[...])
    ig = _sigmoid(jnp.concatenate(ri, axis=1) + bi_ref[...])
    z = -lam_ref[...]
    softplus = jnp.maximum(z, 0.0) + jnp.log1p(jnp.exp(-jnp.abs(z)))
    log_a = (-LRU_C * r) * softplus
    a = jnp.exp(log_a)
    neg_expm1 = -jnp.tanh(log_a) * (a * a + 1.0)
    a_ref[...] = a
    b_ref[...] = jnp.sqrt(neg_expm1) * (ig * xc)

    def step(s, h):
        h = a_ref[pl.ds(s, 1), :] * h + b_ref[pl.ds(s, 1), :]
        hs_ref[pl.ds(s, 1), :] = h
        return h

    h = lax.fori_loop(0, tt, step, h_ref[...], unroll=8)
    h_ref[...] = h
    hn_ref[0] = h
    lru = hs_ref[...] * _gelu(gl_ref[...])
    o_ref[...] = (_sigmoid(ga_ref[...]) * att_ref[...] + _sigmoid(gb_ref[...]) * lru).astype(o_ref.dtype)


def _lru_kernel_aliased(*refs, **kw):
    _lru_kernel(*refs[:14], *refs[15:], **kw)


def _lru(y, row_off, b, t, off_x, off_gl, off_ga, off_gb, att, conv0, h0, conv_w, conv_b, wa, ba, wi, bi, lam,
         prev, *, tt):
    c = conv_w.shape[1]
    nblk, bw = wa.shape[0], wa.shape[1]
    assert row_off % tt == 0 and t % tt == 0 and all(o % c == 0 for o in (off_x, off_gl, off_ga, off_gb))
    rows = lambda bb, i: row_off // tt + bb * (t // tt) + i
    col = lambda off: pl.BlockSpec((tt, c), lambda bb, i: (rows(bb, i), off // c))
    vec = pl.BlockSpec((1, c), lambda bb, i: (0, 0))
    wsp = pl.BlockSpec((nblk, bw, bw), lambda bb, i: (0, 0, 0))
    in_specs = [
        col(off_x), col(off_gl), col(off_gb), col(off_ga), col(0),
        pl.BlockSpec((1, SUBLANES, c), lambda bb, i: (bb, 0, 0)),
        pl.BlockSpec((1, 1, c), lambda bb, i: (bb, 0, 0)),
        pl.BlockSpec((CONV_W, c), lambda bb, i: (0, 0)),
        vec, wsp, vec, wsp, vec, vec,
    ]
    args = (y, y, y, y, att, conv0, h0, conv_w, conv_b.reshape(1, c), wa, ba.reshape(1, c), wi, bi.reshape(1, c),
            lam.reshape(1, c), prev)
    in_specs.append(pl.BlockSpec(memory_space=pl.ANY))
    return pl.pallas_call(
        functools.partial(_lru_kernel_aliased, tt=tt, nblk=nblk, bw=bw), grid=(b, t // tt), in_specs=in_specs,
        out_specs=[
            pl.BlockSpec((tt, c), lambda bb, i: (rows(bb, i), 0)),
            pl.BlockSpec((1, SUBLANES, c), lambda bb, i: (bb, 0, 0)),
            pl.BlockSpec((1, 1, c), lambda bb, i: (bb, 0, 0)),
        ],
        out_shape=[
            jax.ShapeDtypeStruct(prev.shape, BF16),
            jax.ShapeDtypeStruct((b, SUBLANES, c), F32),
            jax.ShapeDtypeStruct((b, 1, c), F32),
        ],
        scratch_shapes=[
            pltpu.VMEM((tt + SUBLANES, c), F32),
            pltpu.VMEM((tt, c), F32),
            pltpu.VMEM((tt, c), F32),
            pltpu.VMEM((tt, c), F32),
            pltpu.VMEM((1, c), F32),
        ],
        input_output_aliases={14: 0},
        compiler_params=_cparams(("arbitrary", "arbitrary")), name="rg_lru")(*args)


def _mem_attn_kernel(q_ref, mk_ref, mv_ref, o_ref, *, heads, hd):
    scale = hd ** -0.5
    nt = (((1,), (1,)), ((), ()))
    for h in range(heads):
        sl = slice(h * hd, (h + 1) * hd)
        s = lax.dot_general(q_ref[:, sl].astype(BF16), mk_ref[0, :, sl], nt, preferred_element_type=F32) * scale
        s = s - jnp.max(s, axis=1, keepdims=True)
        e = jnp.exp(s)
        p = e / jnp.sum(e, axis=1, keepdims=True)
        o_ref[:, sl] = jnp.dot(p.astype(BF16), mv_ref[0, :, sl], preferred_element_type=F32).astype(o_ref.dtype)


def _mem_attn_kernel_aliased(q_ref, mk_ref, mv_ref, prev_ref, o_ref, **kw):
    del prev_ref
    _mem_attn_kernel(q_ref, mk_ref, mv_ref, o_ref, **kw)


def _mem_attention(q, row_off, t, mk, mv, prev, *, tm):
    b, nm, w = mk.shape
    assert row_off % tm == 0 and t % tm == 0
    kw = dict(heads=MEM_HEADS, hd=w // MEM_HEADS)
    qrow = lambda bb, i: (row_off // tm + bb * (t // tm) + i, 0)
    in_specs = [
        pl.BlockSpec((tm, w), qrow),
        pl.BlockSpec((1, nm, w), lambda bb, i: (bb, 0, 0)),
        pl.BlockSpec((1, nm, w), lambda bb, i: (bb, 0, 0)),
        pl.BlockSpec(memory_space=pl.ANY),
    ]
    return pl.pallas_call(
        functools.partial(_mem_attn_kernel_aliased, **kw), grid=(b, t // tm), in_specs=in_specs,
        out_specs=pl.BlockSpec((tm, w), qrow),
        out_shape=jax.ShapeDtypeStruct(prev.shape, BF16),
        input_output_aliases={3: 0},
        compiler_params=_cparams(("parallel", "parallel")), name="mem_attention")(q, mk, mv, prev)


def _top_values(x, k):
    vals = []
    for _ in range(k):
        m = jnp.max(x, axis=0, keepdims=True)
        vals.append(m)
        x = jnp.where(x == m, -jnp.inf, x)
    return vals


def _peer_route_kernel(q_ref, sub_ref, ta_ref, sb_ref, ca_ref, eb_ref, *, heads):
    nt = (((1,), (1,)), ((), ()))
    k = PEER_TOPK
    for h in range(heads):
        st = []
        sv = []
        for c in range(2):
            col = (h * 2 + c) * PEER_HALF
            qh = q_ref[:, col:col + PEER_HALF].astype(BF16)
            s = lax.dot_general(sub_ref[h * 2 + c], qh, nt, preferred_element_type=F32)
            st.append(s)
            sv.append(_top_values(s, k + 1))
        rows = [sv[0][a] + sv[1][b] for a in range(k + 1) for b in range((k + 1) // (a + 1))]
        rows += [jnp.full_like(rows[0], -jnp.inf)] * (-len(rows) % SUBLANES)
        tv = _top_values(jnp.concatenate(rows, axis=0), k + 1)
        z = jnp.zeros_like(tv[0])
        for v in tv[:k]:
            z = z + jnp.exp(v - tv[0])
        ta_ref[h] = 0.5 * (tv[k - 1] + tv[k]) - st[0]
        sb_ref[h] = st[1]
        ca_ref[h] = jnp.exp(st[0] - sv[0][0]) / z
        eb_ref[h] = jnp.exp(st[1] - sv[1][0])


def _peer_route(qp, sub, *, tn):
    n = qp.shape[0]
    heads = PEER_HEADS
    kern = functools.partial(_peer_route_kernel, heads=heads)
    big = pl.BlockSpec((heads, PEER_NKEYS, tn), lambda i: (0, 0, i))
    big_shape = jax.ShapeDtypeStruct((heads, PEER_NKEYS, n), F32)
    return pl.pallas_call(
        kern, grid=(n // tn,),
        in_specs=[
            pl.BlockSpec((tn, qp.shape[1]), lambda i: (i, 0)),
            pl.BlockSpec(sub.shape, lambda i: (0, 0, 0)),
        ],
        out_specs=[big, big, big, big],
        out_shape=[big_shape, big_shape, big_shape, big_shape],
        compiler_params=_cparams(("parallel",)), name="peer_route")(qp, sub)


def _peer_dense_kernel(x_ref, u_ref, v_ref, ta_ref, ca_ref, sb_ref, eb_ref, o_ref, *, heads, rows):
    j = pl.program_id(1)

    @pl.when(j == 0)
    def _init():
        o_ref[...] = jnp.zeros(o_ref.shape, F32)

    act = _gelu(jnp.dot(x_ref[...], u_ref[...], preferred_element_type=F32))
    tiles = []
    for r in range(rows):
        w = None
        for h in range(heads):
            term = jnp.where(sb_ref[h] >= ta_ref[r, h:h + 1, :], eb_ref[h], 0.0) * ca_ref[r, h:h + 1, :]
            w = term if w is None else w + term
        tiles.append(w)
    gate = jnp.concatenate(tiles, axis=0) if rows > 1 else tiles[0]
    coef = (gate.T * act).astype(BF16)
    o_ref[...] += jnp.dot(coef, v_ref[...], preferred_element_type=F32)


def _peer_dense(xn, u, v, ta, ca, sb, eb, *, tn, te):
    n, d = xn.shape
    e = v.shape[0]
    heads = sb.shape[0]
    rows = te // PEER_NKEYS
    kern = functools.partial(_peer_dense_kernel, heads=heads, rows=rows)
    once = pl.Buffered(1)
    row_blk = pl.BlockSpec((rows, heads, tn), lambda i, j: (j, 0, i))
    big = pl.BlockSpec((heads, PEER_NKEYS, tn), lambda i, j: (0, 0, i), pipeline_mode=once)
    return pl.pallas_call(
        kern, grid=(n // tn, e // te),
        in_specs=[
            pl.BlockSpec((tn, d), lambda i, j: (i, 0), pipeline_mode=once),
            pl.BlockSpec((d, te), lambda i, j: (0, j)),
            pl.BlockSpec((te, d), lambda i, j: (j, 0)),
            row_blk, row_blk, big, big,
        ],
        out_specs=pl.BlockSpec((tn, d), lambda i, j: (i, 0)),
        out_shape=jax.ShapeDtypeStruct((n, d), F32),
        compiler_params=_cparams(("parallel", "arbitrary")), name="peer_dense")(xn, u, v, ta, ca, sb, eb)


def _pad_rows(x, n):
    if x.shape[0] == n:
        return x
    return jnp.pad(x, ((0, n - x.shape[0]),) + ((0, 0),) * (x.ndim - 1))


def _index_keys(ki, s_pad, tk):
    b = ki.shape[0]
    kt = jnp.swapaxes(jnp.pad(ki.astype(BF16), ((0, 0), (0, s_pad - ki.shape[1]), (0, 0))), 1, 2)
    tiles = lambda x: jnp.swapaxes(x.reshape(b, LANES, s_pad // tk, tk), 1, 2)
    lo = tiles(jnp.pad(kt, ((0, 0), (0, LANES - IDX_DIM), (0, 0))))
    hi = tiles(jnp.pad(kt, ((0, 0), (LANES - IDX_DIM, 0), (0, 0))))
    return lo, hi


def kernel(x_prompt, x_sample, mem_prompt, cache_k, cache_v, cache_kidx, state_conv, state_lru, cache_mem_k, cache_mem_v, norm_mix_g, w_in, conv_w, conv_b, lru_wa, lru_ba, lru_wi, lru_bi, lru_lambda, w_out, norm_mem_g, norm_memkv_g, mem_wq, mem_wk, mem_wv, mem_wo, norm_ffn_g, peer_wq, peer_subkeys, peer_u, peer_v, norm_final_g):
    assert w_in.shape[0] == 1, "single-layer step"
    assert IDX_DIM * 2 == LANES and PEER_HALF == LANES and PEER_NKEYS == LANES and HEAD_DIM == LANES
    bp, tp, d = x_prompt.shape
    bs, ts, _ = x_sample.shape
    past = cache_k.shape[2]
    n_p, n_s = bp * tp, bs * ts
    n_tok = n_p + n_s
    n_pad = -(-n_tok // TOKEN_TILE) * TOKEN_TILE
    kvw = N_KV_HEADS * HEAD_DIM
    iqw = IDX_HEADS * IDX_DIM
    n_mem = mem_prompt.shape[1]
    mem_w = mem_wq.shape[2]

    x0 = _pad_rows(jnp.concatenate([x_prompt.reshape(n_p, d), x_sample.reshape(n_s, d)], axis=0), n_pad)
    pos = _pad_rows(jnp.concatenate([jnp.tile(jnp.arange(tp, dtype=jnp.int32), bp),
                                     jnp.tile(past + jnp.arange(ts, dtype=jnp.int32), bs)]), n_pad)

    o_k, o_v, o_qi, o_ki = d, d + kvw, d + 2 * kvw, d + 2 * kvw + iqw
    o_xl = o_ki + IDX_DIM + IDX_HEADS
    assert o_xl + 4 * d == w_in.shape[2]
    y_xl, y_gl, y_ga, y_gb = 0, d, 2 * d, 3 * d

    xn = _rmsnorm(x0, norm_mix_g[0], BF16)
    wt = jnp.swapaxes(w_in, 1, 2)[0]
    ya = _matmul_nt(xn, wt, 0, o_ki, "in_proj_qkv")
    yb = _matmul_nt(xn, wt, o_xl, 4 * d, "in_proj_lru")
    yc = _matmul_nt(xn, wt, o_ki, LANES, "in_proj_idx")

    tabs1 = _rope_tables(pos, HEAD_DIM)
    tabs2 = _rope_tables(pos, IDX_DIM)
    q_bf, k_f, k_t, v_bf, qi_bf, ki_f, wi = _prep(ya, yc, tabs1, tabs2, d, kvw, iqw, o_k, o_v, o_qi)
    v_f = ya[:n_tok, o_v:o_v + kvw]

    n_sel_p = min(TOPK_MAX, tp // 4)
    tk_p = _tile(tp, 512, LANES)
    klo, khi = _index_keys(ki_f[:n_p].reshape(bp, tp, IDX_DIM), tp, tk_p)
    bias_p = _idx_mask(qi_bf, wi, 0, tp, klo, khi, tq=_tile(tp, 128, 16), tk=tk_p, q_off=0, s_valid=tp,
                       n_sel=n_sel_p)
    att = _attention(q_bf, 0, tp, k_t, v_bf, tp, bias_p, jnp.zeros((n_pad, d), F32),
                     tq=_tile(tp, 128, 16), tk=_tile(tp, 1024, tk_p), q_off=0, s_valid=tp)

    s_s = past + ts
    s_pad = -(-s_s // LANES) * LANES
    n_sel_s = min(TOPK_MAX, s_s // 4)
    ki_s = jnp.concatenate([cache_kidx[0], ki_f[n_p:n_tok].reshape(bs, ts, IDX_DIM)], axis=1)
    klo_s, khi_s = _index_keys(ki_s, s_pad, s_pad)
    bias_s = _idx_mask(qi_bf, wi, n_p, ts, klo_s, khi_s, tq=ts, tk=s_pad, q_off=past, s_valid=s_s, n_sel=n_sel_s)

    v_s = jnp.concatenate([cache_v[0].reshape(bs, past, kvw).astype(BF16), v_bf[n_p:n_tok].reshape(bs, ts, kvw)],
                          axis=1)
    v_s = jnp.pad(v_s, ((0, 0), (0, s_pad - s_s), (0, 0))).reshape(bs * s_pad, kvw)
    kt_s = jnp.concatenate([jnp.transpose(cache_k[0].reshape(bs, past, kvw).astype(BF16), (2, 0, 1)),
                            k_t[:, n_p:n_tok].reshape(kvw, bs, ts)], axis=2)
    kt_s = jnp.pad(kt_s, ((0, 0), (0, 0), (0, s_pad - s_s))).reshape(kvw, bs * s_pad)
    att = _attention(q_bf, n_p, ts, kt_s, v_s, s_pad, bias_s, att,
                     tq=ts, tk=s_pad, q_off=past, s_valid=s_s)

    wa_bf, wi_bf = lru_wa[0].astype(BF16), lru_wi[0].astype(BF16)
    lru_args = (conv_w[0], conv_b[0], wa_bf, lru_ba[0], wi_bf, lru_bi[0], lru_lambda[0])
    state_pad = SUBLANES - (CONV_W - 1)
    merged, cn_p, hn_p = _lru(yb, 0, bp, tp, y_xl, y_gl, y_ga, y_gb, att, jnp.zeros((bp, SUBLANES, d), F32),
                              jnp.zeros((bp, 1, d), F32), *lru_args, xn, tt=_tile(tp, 128, SUBLANES))
    conv0_s = jnp.pad(state_conv[0], ((0, 0), (state_pad, 0), (0, 0)))
    merged, cn_s, hn_s = _lru(yb, n_p, bs, ts, y_xl, y_gl, y_ga, y_gb, att, conv0_s, state_lru[0].reshape(bs, 1, d),
                              *lru_args, merged, tt=ts)

    x1 = _matmul(merged, w_out[0].astype(BF16), res=x0, name="out_proj")

    mem_n = _rmsnorm(mem_prompt.reshape(bp * n_mem, d), norm_memkv_g[0], BF16)
    mkv = _matmul(mem_n, jnp.concatenate([mem_wk[0], mem_wv[0]], axis=1), name="mem_kv")
    mk_p = mkv[:, :mem_w].reshape(bp, n_mem, mem_w)
    mv_p = mkv[:, mem_w:].reshape(bp, n_mem, mem_w)
    xn2 = _rmsnorm(x1, norm_mem_g[0], BF16)
    qm = _matmul(xn2, mem_wq[0].astype(BF16), name="mem_q")
    om = _mem_attention(qm, 0, tp, mk_p.astype(BF16), mv_p.astype(BF16), jnp.zeros((n_pad, mem_w), BF16),
                        tm=_tile(tp, 512, 16))
    om = _mem_attention(qm, n_p, ts, cache_mem_k[0].reshape(bs, n_mem, mem_w).astype(BF16),
                        cache_mem_v[0].reshape(bs, n_mem, mem_w).astype(BF16), om, tm=ts)
    x2 = _matmul(om, mem_wo[0].astype(BF16), res=x1, name="mem_out")

    xn3 = _rmsnorm(x2, norm_ffn_g[0], BF16)
    qp = _matmul(xn3, peer_wq[0].astype(BF16), name="peer_q")
    sub = peer_subkeys[0].reshape(PEER_HEADS * 2, PEER_NKEYS, PEER_HALF).astype(BF16)
    ta, sb, ca, eb = _peer_route(qp, sub, tn=LANES)
    peer = _peer_dense(xn3, peer_u[0].T.astype(BF16), peer_v[0].astype(BF16),
                       jnp.transpose(ta, (1, 0, 2)), jnp.transpose(ca, (1, 0, 2)), sb, eb,
                       tn=_tile(n_pad, 640, LANES), te=512)

    y_p, y_s = _final_norm(x2, peer, norm_final_g, n_p, n_s)

    kv_shape_p = (1, bp, tp, N_KV_HEADS, HEAD_DIM)
    kv_shape_s = (1, bs, ts, N_KV_HEADS, HEAD_DIM)
    tail = slice(SUBLANES - (CONV_W - 1), SUBLANES)
    return (
        y_p.reshape(bp, tp, d), y_s.reshape(bs, ts, d),
        k_f[:n_p].reshape(kv_shape_p), v_f[:n_p].reshape(kv_shape_p), ki_f[:n_p].reshape(1, bp, tp, IDX_DIM),
        cn_p[:, tail][None], hn_p.reshape(1, bp, d),
        mk_p.reshape(1, bp, n_mem, MEM_HEADS, mem_w // MEM_HEADS),
        mv_p.reshape(1, bp, n_mem, MEM_HEADS, mem_w // MEM_HEADS),
        k_f[n_p:n_tok].reshape(kv_shape_s), v_f[n_p:].reshape(kv_shape_s),
        ki_f[n_p:n_tok].reshape(1, bs, ts, IDX_DIM),
        cn_s[:, tail][None], hn_s.reshape(1, bs, d),
    )
```

```python
import functools

import jax
import jax.numpy as jnp
from jax import lax
from jax.experimental import pallas as pl
from jax.experimental.pallas import tpu as pltpu

F32 = jnp.float32
BF16 = jnp.bfloat16

CHUNK = 64
HEAD_DIM = 128
N_KV_HEADS = 8
IDX_HEADS = 16
IDX_DIM = 64
TOPK_MAX = 256
ROPE_THETA = 500000.0
LRU_BLOCKS = 16
CONV_W = 4
LRU_C = 8.0
MEM_HEADS = 4
PEER_HEADS = 8
PEER_NKEYS = 128
PEER_HALF = 128
PEER_TOPK = 16
EPS = 1e-6

LANES = 128
SUBLANES = 8
TOKEN_TILE = 128
VMEM_LIMIT = 56 * 1024 * 1024
NEG = -1e30
QK_SCALE_LOG2E = (HEAD_DIM ** -0.5) * 1.4426950408889634
INT_MIN = -(2 ** 31)


def _cparams(sem, vmem=VMEM_LIMIT):
    return pltpu.CompilerParams(dimension_semantics=sem, vmem_limit_bytes=vmem)


def _tile(n, pref, mult):
    if n <= pref:
        return n
    t = (pref // mult) * mult
    while t > mult and n % t:
        t -= mult
    assert n % t == 0, (n, pref, mult)
    return t


def _rmsnorm_kernel(x_ref, g_ref, o_ref):
    x = x_ref[...]
    r = lax.rsqrt(jnp.mean(x * x, axis=-1, keepdims=True) + EPS)
    o_ref[...] = ((x * r) * g_ref[...]).astype(o_ref.dtype)


def _final_norm_kernel(x_ref, y_ref, g_ref, op_ref, os_ref, *, first_blocks):
    x = x_ref[...] + y_ref[...]
    r = lax.rsqrt(jnp.mean(x * x, axis=-1, keepdims=True) + EPS)
    val = (x * r) * g_ref[...]

    @pl.when(pl.program_id(0) < first_blocks)
    def _():
        op_ref[...] = val

    @pl.when(pl.program_id(0) >= first_blocks)
    def _():
        os_ref[...] = val


def _final_norm(x, y, g, n_first, n_second):
    d = x.shape[1]
    tm = _tile(n_second, 128, SUBLANES)
    assert n_first % tm == 0
    fb = n_first // tm
    row = pl.BlockSpec((tm, d), lambda i: (i, 0))
    return pl.pallas_call(
        functools.partial(_final_norm_kernel, first_blocks=fb), grid=(fb + n_second // tm,),
        in_specs=[row, row, pl.BlockSpec((1, d), lambda i: (0, 0))],
        out_specs=[pl.BlockSpec((tm, d), lambda i: (jnp.minimum(i, fb - 1), 0)),
                   pl.BlockSpec((tm, d), lambda i: (jnp.maximum(i - fb, 0), 0))],
        out_shape=[jax.ShapeDtypeStruct((n_first, d), F32), jax.ShapeDtypeStruct((n_second, d), F32)],
        compiler_params=_cparams(("arbitrary",)), name="final_norm")(x, y, g.reshape(1, d))


def _rmsnorm(x, g, out_dtype):
    m, d = x.shape
    tm = _tile(m, 640, SUBLANES)
    row = pl.BlockSpec((tm, d), lambda i: (i, 0))
    return pl.pallas_call(
        _rmsnorm_kernel, grid=(m // tm,), in_specs=[row, pl.BlockSpec((1, d), lambda i: (0, 0))], out_specs=row,
        out_shape=jax.ShapeDtypeStruct((m, d), out_dtype),
        compiler_params=_cparams(("parallel",)), name="rmsnorm")(x, g.reshape(1, d))


def _mm_kernel(a_ref, b_ref, *rest, has_res, cast_b):
    rest = list(rest)
    r_ref = rest.pop(0) if has_res else None
    o_ref = rest.pop(0)
    if cast_b:
        bb_ref = rest.pop(0)

        @pl.when(pl.program_id(1) == 0)
        def _():
            bb_ref[...] = b_ref[...].astype(BF16)

        b = bb_ref[...]
    else:
        b = b_ref[...]
    acc = jnp.dot(a_ref[...], b, preferred_element_type=F32)
    if has_res:
        acc = r_ref[...] + acc
    o_ref[...] = acc.astype(o_ref.dtype)


def _matmul(a, b, res=None, n_cols=None, tm_pref=640, out_dtype=F32, name="matmul"):
    m, k = a.shape
    n = b.shape[-1] if n_cols is None else n_cols
    cast_b = b.dtype != BF16
    tm = _tile(m, tm_pref, 16)
    tn = _tile(n, 512 if cast_b else 1024, 2 * LANES if n % (2 * LANES) == 0 else LANES)
    a_spec = pl.BlockSpec((tm, k), lambda j, i: (i, 0))
    b_spec = (pl.BlockSpec((k, tn), lambda j, i: (0, j)) if b.ndim == 2 else
              pl.BlockSpec((None, k, tn), lambda j, i: (0, 0, j)))
    o_spec = pl.BlockSpec((tm, tn), lambda j, i: (i, j))
    specs, args = [a_spec, b_spec], [a, b]
    if res is not None:
        specs.append(o_spec)
        args.append(res)
    return pl.pallas_call(
        functools.partial(_mm_kernel, has_res=res is not None, cast_b=cast_b),
        grid=(n // tn, m // tm), in_specs=specs, out_specs=o_spec,
        out_shape=jax.ShapeDtypeStruct((m, n), out_dtype),
        scratch_shapes=[pltpu.VMEM((k, tn), BF16)] if cast_b else [],
        compiler_params=_cparams(("parallel", "arbitrary" if cast_b else "parallel")), name=name)(*args)


def _mm_nt_kernel(a_ref, bt_ref, o_ref, bb_ref):
    @pl.when(pl.program_id(1) == 0)
    def _():
        bb_ref[...] = bt_ref[...].astype(BF16)

    nt = (((1,), (1,)), ((), ()))
    o_ref[...] = lax.dot_general(a_ref[...], bb_ref[...], nt, preferred_element_type=F32)


def _matmul_nt(a, bt, row_start, n, name):
    m, k = a.shape
    tm = _tile(m, 640, 16)
    tn = _tile(n, 512, LANES)
    assert row_start % SUBLANES == 0 and row_start + n <= bt.shape[0], (row_start, n, bt.shape)
    return pl.pallas_call(
        _mm_nt_kernel, grid=(n // tn, m // tm),
        in_specs=[pl.BlockSpec((tm, k), lambda j, i: (i, 0)),
                  pl.BlockSpec((pl.Element(tn), pl.Element(k)),
                               lambda j, i: (pl.multiple_of(row_start + j * tn, SUBLANES), 0))],
        out_specs=pl.BlockSpec((tm, tn), lambda j, i: (i, j)),
        out_shape=jax.ShapeDtypeStruct((m, n), F32),
        scratch_shapes=[pltpu.VMEM((tn, k), BF16)],
        compiler_params=_cparams(("parallel", "arbitrary")), name=name)(a, bt)


def _rope_tables(pos, head_dim):
    rot = head_dim // 4
    half = rot // 2
    inv = ROPE_THETA ** (-jnp.arange(half, dtype=F32) / half)
    ang = pos.astype(F32)[:, None] * inv[None, :]
    cos, sin = jnp.cos(ang), jnp.sin(ang)
    n = pos.shape[0]
    ones = jnp.ones((n, head_dim - rot), F32)
    zr = jnp.zeros((n, head_dim - rot), F32)
    zh = jnp.zeros((n, half), F32)
    c = jnp.concatenate([cos, cos, ones], axis=1)
    sa = jnp.concatenate([-sin, zh, zr], axis=1)
    sb = jnp.concatenate([zh, sin, zr], axis=1)
    reps = LANES // head_dim
    return tuple(jnp.tile(t, (1, reps)) for t in (c, sa, sb))


def _rope(x, c, sa, sb, half):
    return x * c + pltpu.roll(x, LANES - half, 1) * sa + pltpu.roll(x, half, 1) * sb


def _prep_kernel(q_ref, k_ref, v_ref, qi_ref, kw_ref, c1_ref, sa1_ref, sb1_ref, c2_ref, sa2_ref, sb2_ref,
                 qo_ref, kf_ref, kt_ref, vb_ref, qio_ref, kif_ref, wi_ref):
    c1, sa1, sb1 = c1_ref[...], sa1_ref[...], sb1_ref[...]
    c2, sa2, sb2 = c2_ref[...], sa2_ref[...], sb2_ref[...]
    h1 = HEAD_DIM // 8
    h2 = IDX_DIM // 8
    for h in range(q_ref.shape[1] // LANES):
        sl = slice(h * LANES, (h + 1) * LANES)
        qo_ref[:, sl] = (_rope(q_ref[:, sl], c1, sa1, sb1, h1) * QK_SCALE_LOG2E).astype(BF16)
    for h in range(k_ref.shape[1] // LANES):
        sl = slice(h * LANES, (h + 1) * LANES)
        kr = _rope(k_ref[:, sl], c1, sa1, sb1, h1)
        kf_ref[:, sl] = kr
        kt_ref[sl, :] = kr.T.astype(BF16)
    vb_ref[...] = v_ref[...].astype(BF16)
    for h in range(qi_ref.shape[1] // LANES):
        sl = slice(h * LANES, (h + 1) * LANES)
        qio_ref[:, sl] = _rope(qi_ref[:, sl], c2, sa2, sb2, h2).astype(BF16)
    t = kw_ref[...]
    kif_ref[...] = _rope(t, c2, sa2, sb2, h2)[:, :IDX_DIM]
    wi_ref[...] = t[:, IDX_DIM:IDX_DIM + IDX_HEADS] * (IDX_HEADS ** -0.5)


def _prep(y, y_kw, tabs1, tabs2, d, kvw, iqw, off_k, off_v, off_qi):
    n = y.shape[0]
    tm = _tile(n, LANES, LANES)
    tab = pl.BlockSpec((tm, LANES), lambda i: (i, 0))
    in_specs = [
        pl.BlockSpec((tm, d), lambda i: (i, 0)),
        pl.BlockSpec((tm, kvw), lambda i: (i, off_k // kvw)),
        pl.BlockSpec((tm, kvw), lambda i: (i, off_v // kvw)),
        pl.BlockSpec((tm, iqw), lambda i: (i, off_qi // iqw)),
        pl.BlockSpec((tm, LANES), lambda i: (i, 0)),
        tab, tab, tab, tab, tab, tab,
    ]
    outs = [
        (d, BF16), (kvw, F32), None, (kvw, BF16), (iqw, BF16), (IDX_DIM, F32), (IDX_HEADS, F32),
    ]
    row_spec = lambda o: pl.BlockSpec((tm, o[0]), lambda i: (i, 0))
    row_shape = lambda o: jax.ShapeDtypeStruct((n, o[0]), o[1])
    return pl.pallas_call(
        _prep_kernel, grid=(n // tm,), in_specs=in_specs,
        out_specs=[row_spec(o) if o else pl.BlockSpec((kvw, tm), lambda i: (0, i)) for o in outs],
        out_shape=[row_shape(o) if o else jax.ShapeDtypeStruct((kvw, n), BF16) for o in outs],
        compiler_params=_cparams(("parallel",)), name="prep")(y, y, y, y, y_kw, *tabs1, *tabs2)


def _num_k_tiles(i, tq, tk, q_off, s_valid):
    last_chunk_end = ((q_off + i * tq + tq - 1) // CHUNK + 1) * CHUNK
    return (jnp.minimum(last_chunk_end, s_valid) + tk - 1) // tk


def _idx_kernel(qi_ref, wi_ref, klo_ref, khi_ref, o_ref, key_ref, *, tq, tk, nk, q_off, s_valid, n_sel):
    i = pl.program_id(1)
    q0 = q_off + i * tq
    nkt = _num_k_tiles(i, tq, tk, q_off, s_valid)
    wi = wi_ref[...] * (IDX_DIM ** -0.5)
    wcols = [wi[:, h:h + 1] for h in range(IDX_HEADS)]
    q_chunk = (q0 + lax.broadcasted_iota(jnp.int32, (tq, tk), 0)) // CHUNK

    def score_tile(j, carry):
        k0 = j * tk
        klo = klo_ref[0, j]
        khi = khi_ref[0, j]
        acc = jnp.zeros((tq, tk), F32)
        for p in range(IDX_HEADS // 2):
            qp = qi_ref[:, p * LANES:(p + 1) * LANES]
            d0 = jnp.dot(qp, klo, preferred_element_type=F32)
            d1 = jnp.dot(qp, khi, preferred_element_type=F32)
            acc = acc + jnp.maximum(d0, 0.0) * wcols[2 * p]
            acc = acc + jnp.maximum(d1, 0.0) * wcols[2 * p + 1]
        k_pos = k0 + lax.broadcasted_iota(jnp.int32, (tq, tk), 1)
        vis = jnp.logical_and(k_pos // CHUNK <= q_chunk, k_pos < s_valid)
        bits = pltpu.bitcast(acc, jnp.int32)
        key = bits ^ ((bits >> 31) & 0x7FFFFFFF)
        key_ref[j] = jnp.where(vis, key, INT_MIN)
        return carry

    lax.fori_loop(0, nkt, score_tile, 0)

    def count(pred):
        def count_tile(j, acc):
            m = jnp.where(pred(j, key_ref[j]), 1.0, 0.0)
            part = m[:, 0:LANES]
            for c in range(1, tk // LANES):
                part = part + m[:, c * LANES:(c + 1) * LANES]
            return acc + part

        acc = lax.fori_loop(0, nkt, count_tile, jnp.zeros((tq, LANES), F32))
        return jnp.sum(acc, axis=1, keepdims=True)

    def bit_step(it, carry):
        r, cnt_r = carry
        cand = r | jnp.left_shift(jnp.int32(1), 31 - it)
        cs = cand ^ INT_MIN
        cnt = count(lambda j, key: key >= cs)
        take = cnt >= n_sel
        return jnp.where(take, cand, r), jnp.where(take, cnt, cnt_r)

    r, cnt_ge = lax.fori_loop(0, 32, bit_step,
                              (jnp.zeros((tq, 1), jnp.int32), jnp.full((tq, 1), float(n_sel), F32)))
    thr = jnp.maximum(r ^ INT_MIN, INT_MIN + 1)
    has_ties = jnp.max(cnt_ge) > float(n_sel)

    @pl.when(jnp.logical_not(has_ties))
    def _write_plain():
        def write_tile(j, carry):
            o_ref[0, j] = jnp.where(key_ref[j] >= thr, 0.0, NEG).astype(BF16)
            return carry

        lax.fori_loop(0, nkt, write_tile, 0)

    @pl.when(has_ties)
    def _write_tie_broken():
        need = float(n_sel) - count(lambda j, key: key > thr)
        lane = lax.broadcasted_iota(jnp.int32, (tq, tk), 1)

        def pos_step(it, p):
            cand = p | jnp.left_shift(jnp.int32(1), pos_bits - 1 - it)
            cnt = count(lambda j, key: jnp.logical_and(key == thr, lane + j * tk < cand))
            return jnp.where(cnt < need, cand, p)

        pos_bits = max(1, (nk * tk - 1).bit_length())
        p = lax.fori_loop(0, pos_bits, pos_step, jnp.zeros((tq, 1), jnp.int32))

        def write_tile(j, carry):
            key = key_ref[j]
            sel = jnp.logical_or(key > thr, jnp.logical_and(key == thr, lane + j * tk <= p))
            o_ref[0, j] = jnp.where(sel, 0.0, NEG).astype(BF16)
            return carry

        lax.fori_loop(0, nkt, write_tile, 0)

    def fill_tile(j, carry):
        o_ref[0, j] = jnp.full((tq, tk), NEG, BF16)
        return carry

    lax.fori_loop(nkt, nk, fill_tile, 0)


def _idx_mask(qi, wi, row_off, t, klo, khi, *, tq, tk, q_off, s_valid, n_sel):
    b, nk = klo.shape[:2]
    assert klo.shape[3] == tk
    iqw = qi.shape[1]
    assert row_off % tq == 0 and t % tq == 0
    qrow = lambda bb, i: (row_off // tq + bb * (t // tq) + i, 0)
    kern = functools.partial(_idx_kernel, tq=tq, tk=tk, nk=nk, q_off=q_off, s_valid=s_valid, n_sel=n_sel)
    return pl.pallas_call(
        kern, grid=(b, t // tq),
        in_specs=[
            pl.BlockSpec((tq, iqw), qrow),
            pl.BlockSpec((tq, IDX_HEADS), qrow),
            pl.BlockSpec((1, nk, LANES, tk), lambda bb, i: (bb, 0, 0, 0)),
            pl.BlockSpec((1, nk, LANES, tk), lambda bb, i: (bb, 0, 0, 0)),
        ],
        out_specs=pl.BlockSpec((1, nk, tq, tk), lambda bb, i: (bb, 0, i, 0)),
        out_shape=jax.ShapeDtypeStruct((b, nk, t, tk), BF16),
        scratch_shapes=[pltpu.VMEM((nk, tq, tk), jnp.int32)],
        compiler_params=_cparams(("parallel", "parallel")), name="idx_mask")(qi, wi, klo, khi)


def _attn_kernel(q_ref, k_ref, v_ref, b_ref, o_ref, qs_ref, m_ref, acc_ref,
                 *, tq, tk, nk, q_off, s_valid, nkv, grp):
    i = pl.program_id(1)
    j = pl.program_id(2)
    nkt = _num_k_tiles(i, tq, tk, q_off, s_valid)

    @pl.when(j == 0)
    def _init():
        eye = (lax.broadcasted_iota(jnp.int32, (tq, LANES), 0) ==
               lax.broadcasted_iota(jnp.int32, (tq, LANES), 1)).astype(F32).astype(BF16)
        for g in range(nkv):
            for hh in range(grp):
                h = g * grp + hh
                qs_ref[g, hh * tq:(hh + 1) * tq, :HEAD_DIM] = q_ref[:, h * HEAD_DIM:(h + 1) * HEAD_DIM]
                qs_ref[g, hh * tq:(hh + 1) * tq, HEAD_DIM:] = eye
        m_ref[...] = jnp.full(m_ref.shape, NEG, F32)
        acc_ref[...] = jnp.zeros(acc_ref.shape, F32)

    @pl.when(j < nkt)
    def _compute():
        mask = jnp.concatenate([b_ref[0, c] for c in range(b_ref.shape[1])], axis=1)
        if tq < LANES:
            mask = jnp.concatenate([mask, jnp.zeros((LANES - tq, tk), BF16)], axis=0)
        nc = tk // LANES
        ones = jnp.ones((tk, LANES), BF16)
        chunks = lambda a: [a[:, c * LANES:(c + 1) * LANES] for c in range(nc)]
        for g in range(nkv):
            kg = jnp.concatenate([k_ref[g * HEAD_DIM:(g + 1) * HEAD_DIM, :], mask], axis=0)
            vg = jnp.concatenate([v_ref[:, g * HEAD_DIM:(g + 1) * HEAD_DIM], ones], axis=1)
            s = jnp.dot(qs_ref[g], kg, preferred_element_type=F32)
            m_prev = m_ref[g]
            m_new = jnp.maximum(m_prev, jnp.max(functools.reduce(jnp.maximum, chunks(s)), axis=1, keepdims=True))
            p = jnp.exp2(s - jnp.tile(m_new, (1, nc)))
            alpha = jnp.exp2(m_prev - m_new)
            acc_ref[g] = jnp.tile(alpha, (1, 2)) * acc_ref[g] + jnp.dot(p.astype(BF16), vg, preferred_element_type=F32)
            m_ref[g] = m_new

    @pl.when(j == nk - 1)
    def _finish():
        for g in range(nkv):
            for hh in range(grp):
                h = g * grp + hh
                rows = slice(hh * tq, (hh + 1) * tq)
                o_ref[:, h * HEAD_DIM:(h + 1) * HEAD_DIM] = acc_ref[g, rows, :HEAD_DIM] / acc_ref[g, rows, HEAD_DIM:]


def _attn_kernel_aliased(q_ref, k_ref, v_ref, b_ref, prev_ref, o_ref, *scratch, **kw):
    del prev_ref
    _attn_kernel(q_ref, k_ref, v_ref, b_ref, o_ref, *scratch, **kw)


def _attention(q, row_off, t, k, v, s_pad, bias, prev, *, tq, tk, q_off, s_valid):
    b = bias.shape[0]
    aw, kvw = q.shape[1], v.shape[1]
    nk = s_pad // tk
    nkv = kvw // HEAD_DIM
    grp = aw // kvw
    tkb = bias.shape[3]
    assert row_off % tq == 0 and t % tq == 0 and tq <= LANES and tk % tkb == 0
    kw = dict(tq=tq, tk=tk, nk=nk, q_off=q_off, s_valid=s_valid, nkv=nkv, grp=grp)

    def kj(i, j):
        return jnp.minimum(j, _num_k_tiles(i, tq, tk, q_off, s_valid) - 1)

    qrow = lambda bb, i, j: (row_off // tq + bb * (t // tq) + i, 0)
    krow = lambda bb, i, j: (bb * nk + kj(i, j), 0)
    in_specs = [
        pl.BlockSpec((tq, aw), qrow),
        pl.BlockSpec((kvw, tk), lambda bb, i, j: (0, bb * nk + kj(i, j))),
        pl.BlockSpec((tk, kvw), krow),
        pl.BlockSpec((1, tk // tkb, tq, tkb), lambda bb, i, j: (bb, kj(i, j), i, 0)),
        pl.BlockSpec(memory_space=pl.ANY),
    ]
    return pl.pallas_call(
        functools.partial(_attn_kernel_aliased, **kw), grid=(b, t // tq, nk), in_specs=in_specs,
        out_specs=pl.BlockSpec((tq, aw), qrow),
        out_shape=jax.ShapeDtypeStruct(prev.shape, F32),
        scratch_shapes=[
            pltpu.VMEM((nkv, grp * tq, HEAD_DIM + LANES), BF16),
            pltpu.VMEM((nkv, grp * tq, LANES), F32),
            pltpu.VMEM((nkv, grp * tq, 2 * HEAD_DIM), F32),
        ],
        input_output_aliases={4: 0},
        compiler_params=_cparams(("parallel", "parallel", "arbitrary")), name="attention")(q, k, v, bias, prev)


def _sigmoid(x):
    return 0.5 * jnp.tanh(0.5 * x) + 0.5


def _gelu(x):
    return 0.5 * x * (1.0 + lax.erf(x * (0.5 ** 0.5)))


def _lru_kernel(x_ref, gl_ref, gb_ref, ga_ref, att_ref, c0_ref, h0_ref, cw_ref, cb_ref, wa_ref, ba_ref, wi_ref, bi_ref,
                lam_ref, o_ref, cn_ref, hn_ref, xp_ref, a_ref, b_ref, hs_ref, h_ref, *, tt, nblk, bw):
    t = pl.program_id(1)
    pad = SUBLANES

    @pl.when(t == 0)
    def _init():
        xp_ref[0:pad, :] = c0_ref[0]
        h_ref[...] = h0_ref[0]

    xp_ref[pad:pad + tt, :] = x_ref[...]
    cw = cw_ref[...]
    base = pad - (CONV_W - 1)
    xc = cb_ref[...] + xp_ref[base:base + tt, :] * cw[0:1]
    for jj in range(1, CONV_W):
        xc = xc + xp_ref[base + jj:base + jj + tt, :] * cw[jj:jj + 1]
    tail = xp_ref[tt:tt + pad, :]
    cn_ref[0] = tail
    xp_ref[0:pad, :] = tail

    xcb = xc.astype(BF16)
    ra, ri = [], []
    for n in range(nblk):
        xs = xcb[:, n * bw:(n + 1) * bw]
        ra.append(jnp.dot(xs, wa_ref[n], preferred_element_type=F32))
        ri.append(jnp.dot(xs, wi_ref[n], preferred_element_type=F32))
    r = _sigmoid(jnp.concatenate(ra, axis=1) + ba_ref[...])
    ig = _sigmoid(jnp.concatenate(ri, axis=1) + bi_ref[...])
    z = -lam_ref[...]
    softplus = jnp.maximum(z, 0.0) + jnp.log1p(jnp.exp(-jnp.abs(z)))
    log_a = (-LRU_C * r) * softplus
    a = jnp.exp(log_a)
    neg_expm1 = -jnp.tanh(log_a) * (a * a + 1.0)
    a_ref[...] = a
    b_ref[...] = jnp.sqrt(neg_expm1) * (ig * xc)

    def step(s, h):
        h = a_ref[pl.ds(s, 1), :] * h + b_ref[pl.ds(s, 1), :]
        hs_ref[pl.ds(s, 1), :] = h
        return h

    h = lax.fori_loop(0, tt, step, h_ref[...], unroll=8)
    h_ref[...] = h
    hn_ref[0] = h
    lru = hs_ref[...] * _gelu(gl_ref[...])
    o_ref[...] = (_sigmoid(ga_ref[...]) * att_ref[...] + _sigmoid(gb_ref[...]) * lru).astype(o_ref.dtype)


def _lru_kernel_aliased(*refs, **kw):
    _lru_kernel(*refs[:14], *refs[15:], **kw)


def _lru(y, row_off, b, t, off_x, off_gl, off_ga, off_gb, att, conv0, h0, conv_w, conv_b, wa, ba, wi, bi, lam,
         prev, *, tt):
    c = conv_w.shape[1]
    nblk, bw = wa.shape[0], wa.shape[1]
    assert row_off % tt == 0 and t % tt == 0 and all(o % c == 0 for o in (off_x, off_gl, off_ga, off_gb))
    rows = lambda bb, i: row_off // tt + bb * (t // tt) + i
    col = lambda off: pl.BlockSpec((tt, c), lambda bb, i: (rows(bb, i), off // c))
    vec = pl.BlockSpec((1, c), lambda bb, i: (0, 0))
    wsp = pl.BlockSpec((nblk, bw, bw), lambda bb, i: (0, 0, 0))
    in_specs = [
        col(off_x), col(off_gl), col(off_gb), col(off_ga), col(0),
        pl.BlockSpec((1, SUBLANES, c), lambda bb, i: (bb, 0, 0)),
        pl.BlockSpec((1, 1, c), lambda bb, i: (bb, 0, 0)),
        pl.BlockSpec((CONV_W, c), lambda bb, i: (0, 0)),
        vec, wsp, vec, wsp, vec, vec,
    ]
    args = (y, y, y, y, att, conv0, h0, conv_w, conv_b.reshape(1, c), wa, ba.reshape(1, c), wi, bi.reshape(1, c),
            lam.reshape(1, c), prev)
    in_specs.append(pl.BlockSpec(memory_space=pl.ANY))
    return pl.pallas_call(
        functools.partial(_lru_kernel_aliased, tt=tt, nblk=nblk, bw=bw), grid=(b, t // tt), in_specs=in_specs,
        out_specs=[
            pl.BlockSpec((tt, c), lambda bb, i: (rows(bb, i), 0)),
            pl.BlockSpec((1, SUBLANES, c), lambda bb, i: (bb, 0, 0)),
            pl.BlockSpec((1, 1, c), lambda bb, i: (bb, 0, 0)),
        ],
        out_shape=[
            jax.ShapeDtypeStruct(prev.shape, BF16),
            jax.ShapeDtypeStruct((b, SUBLANES, c), F32),
            jax.ShapeDtypeStruct((b, 1, c), F32),
        ],
        scratch_shapes=[
            pltpu.VMEM((tt + SUBLANES, c), F32),
            pltpu.VMEM((tt, c), F32),
            pltpu.VMEM((tt, c), F32),
            pltpu.VMEM((tt, c), F32),
            pltpu.VMEM((1, c), F32),
        ],
        input_output_aliases={14: 0},
        compiler_params=_cparams(("arbitrary", "arbitrary")), name="rg_lru")(*args)


def _mem_attn_kernel(q_ref, mk_ref, mv_ref, o_ref, *, heads, hd):
    scale = hd ** -0.5
    nt = (((1,), (1,)), ((), ()))
    for h in range(heads):
        sl = slice(h * hd, (h + 1) * hd)
        s = lax.dot_general(q_ref[:, sl].astype(BF16), mk_ref[0, :, sl], nt, preferred_element_type=F32) * scale
        s = s - jnp.max(s, axis=1, keepdims=True)
        e = jnp.exp(s)
        p = e / jnp.sum(e, axis=1, keepdims=True)
        o_ref[:, sl] = jnp.dot(p.astype(BF16), mv_ref[0, :, sl], preferred_element_type=F32).astype(o_ref.dtype)


def _mem_attn_kernel_aliased(q_ref, mk_ref, mv_ref, prev_ref, o_ref, **kw):
    del prev_ref
    _mem_attn_kernel(q_ref, mk_ref, mv_ref, o_ref, **kw)


def _mem_attention(q, row_off, t, mk, mv, prev, *, tm):
    b, nm, w = mk.shape
    assert row_off % tm == 0 and t % tm == 0
    kw = dict(heads=MEM_HEADS, hd=w // MEM_HEADS)
    qrow = lambda bb, i: (row_off // tm + bb * (t // tm) + i, 0)
    in_specs = [
        pl.BlockSpec((tm, w), qrow),
        pl.BlockSpec((1, nm, w), lambda bb, i: (bb, 0, 0)),
        pl.BlockSpec((1, nm, w), lambda bb, i: (bb, 0, 0)),
        pl.BlockSpec(memory_space=pl.ANY),
    ]
    return pl.pallas_call(
        functools.partial(_mem_attn_kernel_aliased, **kw), grid=(b, t // tm), in_specs=in_specs,
        out_specs=pl.BlockSpec((tm, w), qrow),
        out_shape=jax.ShapeDtypeStruct(prev.shape, BF16),
        input_output_aliases={3: 0},
        compiler_params=_cparams(("parallel", "parallel")), name="mem_attention")(q, mk, mv, prev)


def _top_values(x, k):
    vals = []
    for _ in range(k):
        m = jnp.max(x, axis=0, keepdims=True)
        vals.append(m)
        x = jnp.where(x == m, -jnp.inf, x)
    return vals


def _peer_route_kernel(q_ref, sub_ref, ta_ref, sb_ref, ca_ref, eb_ref, *, heads):
    nt = (((1,), (1,)), ((), ()))
    k = PEER_TOPK
    for h in range(heads):
        st = []
        sv = []
        for c in range(2):
            col = (h * 2 + c) * PEER_HALF
            qh = q_ref[:, col:col + PEER_HALF].astype(BF16)
            s = lax.dot_general(sub_ref[h * 2 + c], qh, nt, preferred_element_type=F32)
            st.append(s)
            sv.append(_top_values(s, k + 1))
        rows = [sv[0][a] + sv[1][b] for a in range(k + 1) for b in range((k + 1) // (a + 1))]
        rows += [jnp.full_like(rows[0], -jnp.inf)] * (-len(rows) % SUBLANES)
        tv = _top_values(jnp.concatenate(rows, axis=0), k + 1)
        z = jnp.zeros_like(tv[0])
        for v in tv[:k]:
            z = z + jnp.exp(v - tv[0])
        ta_ref[h] = 0.5 * (tv[k - 1] + tv[k]) - st[0]
        sb_ref[h] = st[1]
        ca_ref[h] = jnp.exp(st[0] - sv[0][0]) / z
        eb_ref[h] = jnp.exp(st[1] - sv[1][0])


def _peer_route(qp, sub, *, tn):
    n = qp.shape[0]
    heads = PEER_HEADS
    kern = functools.partial(_peer_route_kernel, heads=heads)
    big = pl.BlockSpec((heads, PEER_NKEYS, tn), lambda i: (0, 0, i))
    big_shape = jax.ShapeDtypeStruct((heads, PEER_NKEYS, n), F32)
    return pl.pallas_call(
        kern, grid=(n // tn,),
        in_specs=[
            pl.BlockSpec((tn, qp.shape[1]), lambda i: (i, 0)),
            pl.BlockSpec(sub.shape, lambda i: (0, 0, 0)),
        ],
        out_specs=[big, big, big, big],
        out_shape=[big_shape, big_shape, big_shape, big_shape],
        compiler_params=_cparams(("parallel",)), name="peer_route")(qp, sub)


def _peer_dense_kernel(x_ref, u_ref, v_ref, ta_ref, ca_ref, sb_ref, eb_ref, o_ref, *, heads, rows):
    j = pl.program_id(1)

    @pl.when(j == 0)
    def _init():
        o_ref[...] = jnp.zeros(o_ref.shape, F32)

    act = _gelu(jnp.dot(x_ref[...], u_ref[...], preferred_element_type=F32))
    tiles = []
    for r in range(rows):
        w = None
        for h in range(heads):
            term = jnp.where(sb_ref[h] >= ta_ref[r, h:h + 1, :], eb_ref[h], 0.0) * ca_ref[r, h:h + 1, :]
            w = term if w is None else w + term
        tiles.append(w)
    gate = jnp.concatenate(tiles, axis=0) if rows > 1 else tiles[0]
    coef = (gate.T * act).astype(BF16)
    o_ref[...] += jnp.dot(coef, v_ref[...], preferred_element_type=F32)


def _peer_dense(xn, u, v, ta, ca, sb, eb, *, tn, te):
    n, d = xn.shape
    e = v.shape[0]
    heads = sb.shape[0]
    rows = te // PEER_NKEYS
    kern = functools.partial(_peer_dense_kernel, heads=heads, rows=rows)
    once = pl.Buffered(1)
    row_blk = pl.BlockSpec((rows, heads, tn), lambda i, j: (j, 0, i))
    big = pl.BlockSpec((heads, PEER_NKEYS, tn), lambda i, j: (0, 0, i), pipeline_mode=once)
    return pl.pallas_call(
        kern, grid=(n // tn, e // te),
        in_specs=[
            pl.BlockSpec((tn, d), lambda i, j: (i, 0), pipeline_mode=once),
            pl.BlockSpec((d, te), lambda i, j: (0, j)),
            pl.BlockSpec((te, d), lambda i, j: (j, 0)),
            row_blk, row_blk, big, big,
        ],
        out_specs=pl.BlockSpec((tn, d), lambda i, j: (i, 0)),
        out_shape=jax.ShapeDtypeStruct((n, d), F32),
        compiler_params=_cparams(("parallel", "arbitrary")), name="peer_dense")(xn, u, v, ta, ca, sb, eb)


def _pad_rows(x, n):
    if x.shape[0] == n:
        return x
    return jnp.pad(x, ((0, n - x.shape[0]),) + ((0, 0),) * (x.ndim - 1))


def _index_keys(ki, s_pad, tk):
    b = ki.shape[0]
    kt = jnp.swapaxes(jnp.pad(ki.astype(BF16), ((0, 0), (0, s_pad - ki.shape[1]), (0, 0))), 1, 2)
    tiles = lambda x: jnp.swapaxes(x.reshape(b, LANES, s_pad // tk, tk), 1, 2)
    lo = tiles(jnp.pad(kt, ((0, 0), (0, LANES - IDX_DIM), (0, 0))))
    hi = tiles(jnp.pad(kt, ((0, 0), (LANES - IDX_DIM, 0), (0, 0))))
    return lo, hi


def kernel(x_prompt, x_sample, mem_prompt, cache_k, cache_v, cache_kidx, state_conv, state_lru, cache_mem_k, cache_mem_v, norm_mix_g, w_in, conv_w, conv_b, lru_wa, lru_ba, lru_wi, lru_bi, lru_lambda, w_out, norm_mem_g, norm_memkv_g, mem_wq, mem_wk, mem_wv, mem_wo, norm_ffn_g, peer_wq, peer_subkeys, peer_u, peer_v, norm_final_g):
    assert w_in.shape[0] == 1, "single-layer step"
    assert IDX_DIM * 2 == LANES and PEER_HALF == LANES and PEER_NKEYS == LANES and HEAD_DIM == LANES
    bp, tp, d = x_prompt.shape
    bs, ts, _ = x_sample.shape
    past = cache_k.shape[2]
    n_p, n_s = bp * tp, bs * ts
    n_tok = n_p + n_s
    n_pad = -(-n_tok // TOKEN_TILE) * TOKEN_TILE
    kvw = N_KV_HEADS * HEAD_DIM
    iqw = IDX_HEADS * IDX_DIM
    n_mem = mem_prompt.shape[1]
    mem_w = mem_wq.shape[2]

    x0 = _pad_rows(jnp.concatenate([x_prompt.reshape(n_p, d), x_sample.reshape(n_s, d)], axis=0), n_pad)
    pos = _pad_rows(jnp.concatenate([jnp.tile(jnp.arange(tp, dtype=jnp.int32), bp),
                                     jnp.tile(past + jnp.arange(ts, dtype=jnp.int32), bs)]), n_pad)

    o_k, o_v, o_qi, o_ki = d, d + kvw, d + 2 * kvw, d + 2 * kvw + iqw
    o_xl = o_ki + IDX_DIM + IDX_HEADS
    assert o_xl + 4 * d == w_in.shape[2]
    y_xl, y_gl, y_ga, y_gb = 0, d, 2 * d, 3 * d

    xn = _rmsnorm(x0, norm_mix_g[0], BF16)
    wt = jnp.swapaxes(w_in, 1, 2)[0]
    ya = _matmul_nt(xn, wt, 0, o_ki, "in_proj_qkv")
    yb = _matmul_nt(xn, wt, o_xl, 4 * d, "in_proj_lru")
    yc = _matmul_nt(xn, wt, o_ki, LANES, "in_proj_idx")

    tabs1 = _rope_tables(pos, HEAD_DIM)
    tabs2 = _rope_tables(pos, IDX_DIM)
    q_bf, k_f, k_t, v_bf, qi_bf, ki_f, wi = _prep(ya, yc, tabs1, tabs2, d, kvw, iqw, o_k, o_v, o_qi)
    v_f = ya[:n_tok, o_v:o_v + kvw]

    n_sel_p = min(TOPK_MAX, tp // 4)
    tk_p = _tile(tp, 512, LANES)
    klo, khi = _index_keys(ki_f[:n_p].reshape(bp, tp, IDX_DIM), tp, tk_p)
    bias_p = _idx_mask(qi_bf, wi, 0, tp, klo, khi, tq=_tile(tp, 256, 16), tk=tk_p, q_off=0, s_valid=tp,
                       n_sel=n_sel_p)
    att = _attention(q_bf, 0, tp, k_t, v_bf, tp, bias_p, jnp.zeros((n_pad, d), F32),
                     tq=_tile(tp, 128, 16), tk=_tile(tp, 1024, tk_p), q_off=0, s_valid=tp)

    s_s = past + ts
    s_pad = -(-s_s // LANES) * LANES
    n_sel_s = min(TOPK_MAX, s_s // 4)
    ki_s = jnp.concatenate([cache_kidx[0], ki_f[n_p:n_tok].reshape(bs, ts, IDX_DIM)], axis=1)
    klo_s, khi_s = _index_keys(ki_s, s_pad, s_pad)
    bias_s = _idx_mask(qi_bf, wi, n_p, ts, klo_s, khi_s, tq=ts, tk=s_pad, q_off=past, s_valid=s_s, n_sel=n_sel_s)

    v_s = jnp.concatenate([cache_v[0].reshape(bs, past, kvw).astype(BF16), v_bf[n_p:n_tok].reshape(bs, ts, kvw)],
                          axis=1)
    v_s = jnp.pad(v_s, ((0, 0), (0, s_pad - s_s), (0, 0))).reshape(bs * s_pad, kvw)
    kt_s = jnp.concatenate([jnp.transpose(cache_k[0].reshape(bs, past, kvw).astype(BF16), (2, 0, 1)),
                            k_t[:, n_p:n_tok].reshape(kvw, bs, ts)], axis=2)
    kt_s = jnp.pad(kt_s, ((0, 0), (0, 0), (0, s_pad - s_s))).reshape(kvw, bs * s_pad)
    att = _attention(q_bf, n_p, ts, kt_s, v_s, s_pad, bias_s, att,
                     tq=ts, tk=s_pad, q_off=past, s_valid=s_s)

    wa_bf, wi_bf = lru_wa[0].astype(BF16), lru_wi[0].astype(BF16)
    lru_args = (conv_w[0], conv_b[0], wa_bf, lru_ba[0], wi_bf, lru_bi[0], lru_lambda[0])
    state_pad = SUBLANES - (CONV_W - 1)
    merged, cn_p, hn_p = _lru(yb, 0, bp, tp, y_xl, y_gl, y_ga, y_gb, att, jnp.zeros((bp, SUBLANES, d), F32),
                              jnp.zeros((bp, 1, d), F32), *lru_args, xn, tt=_tile(tp, 128, SUBLANES))
    conv0_s = jnp.pad(state_conv[0], ((0, 0), (state_pad, 0), (0, 0)))
    merged, cn_s, hn_s = _lru(yb, n_p, bs, ts, y_xl, y_gl, y_ga, y_gb, att, conv0_s, state_lru[0].reshape(bs, 1, d),
                              *lru_args, merged, tt=ts)

    x1 = _matmul(merged, w_out[0].astype(BF16), res=x0, name="out_proj")

    mem_n = _rmsnorm(mem_prompt.reshape(bp * n_mem, d), norm_memkv_g[0], BF16)
    mkv = _matmul(mem_n, jnp.concatenate([mem_wk[0], mem_wv[0]], axis=1), name="mem_kv")
    mk_p = mkv[:, :mem_w].reshape(bp, n_mem, mem_w)
    mv_p = mkv[:, mem_w:].reshape(bp, n_mem, mem_w)
    xn2 = _rmsnorm(x1, norm_mem_g[0], BF16)
    qm = _matmul(xn2, mem_wq[0].astype(BF16), name="mem_q")
    om = _mem_attention(qm, 0, tp, mk_p.astype(BF16), mv_p.astype(BF16), jnp.zeros((n_pad, mem_w), BF16),
                        tm=_tile(tp, 512, 16))
    om = _mem_attention(qm, n_p, ts, cache_mem_k[0].reshape(bs, n_mem, mem_w).astype(BF16),
                        cache_mem_v[0].reshape(bs, n_mem, mem_w).astype(BF16), om, tm=ts)
    x2 = _matmul(om, mem_wo[0].astype(BF16), res=x1, name="mem_out")

    xn3 = _rmsnorm(x2, norm_ffn_g[0], BF16)
    qp = _matmul(xn3, peer_wq[0].astype(BF16), name="peer_q")
    sub = peer_subkeys[0].reshape(PEER_HEADS * 2, PEER_NKEYS, PEER_HALF).astype(BF16)
    ta, sb, ca, eb = _peer_route(qp, sub, tn=LANES)
    peer = _peer_dense(xn3, peer_u[0].T.astype(BF16), peer_v[0].astype(BF16),
                       jnp.transpose(ta, (1, 0, 2)), jnp.transpose(ca, (1, 0, 2)), sb, eb,
                       tn=_tile(n_pad, 640, LANES), te=512)

    y_p, y_s = _final_norm(x2, peer, norm_final_g, n_p, n_s)

    kv_shape_p = (1, bp, tp, N_KV_HEADS, HEAD_DIM)
    kv_shape_s = (1, bs, ts, N_KV_HEADS, HEAD_DIM)
    tail = slice(SUBLANES - (CONV_W - 1), SUBLANES)
    return (
        y_p.reshape(bp, tp, d), y_s.reshape(bs, ts, d),
        k_f[:n_p].reshape(kv_shape_p), v_f[:n_p].reshape(kv_shape_p), ki_f[:n_p].reshape(1, bp, tp, IDX_DIM),
        cn_p[:, tail][None], hn_p.reshape(1, bp, d),
        mk_p.reshape(1, bp, n_mem, MEM_HEADS, mem_w // MEM_HEADS),
        mv_p.reshape(1, bp, n_mem, MEM_HEADS, mem_w // MEM_HEADS),
        k_f[n_p:n_tok].reshape(kv_shape_s), v_f[n_p:].reshape(kv_shape_s),
        ki_f[n_p:n_tok].reshape(1, bs, ts, IDX_DIM),
        cn_s[:, tail][None], hn_s.reshape(1, bs, d),
    )
```
